```python
import math
import jax, jax.numpy as jnp
from jax import lax
import numpy as np

D_MODEL = 1024
BATCH = 8
SEQ = 4096
DEPTH = 4

HEAD_DIM = 64
RWKV_HEADS = 8
RWKV_DIM = RWKV_HEADS * HEAD_DIM
SB_HEADS = 8
SB_DIM = SB_HEADS * HEAD_DIM
DECAY_RANK = 64
AAA_RANK = 64
GATE_RANK = 128
VRES_RANK = 32
D_FF = 2816
SB_BLOCK = 128
N_MOD = 9
RMS_EPS = 1e-6
GN_EPS = 64e-5

RWKV_COLS = 3 * RWKV_DIM + DECAY_RANK + AAA_RANK + GATE_RANK
SB_COLS = 3 * SB_DIM
GATE_COLS = 2 * D_MODEL
IN_COLS_FIRST = RWKV_COLS + SB_COLS + GATE_COLS
IN_COLS = IN_COLS_FIRST + VRES_RANK
RWKV_SPLITS = [RWKV_DIM, 2 * RWKV_DIM, 3 * RWKV_DIM, 3 * RWKV_DIM + DECAY_RANK, 3 * RWKV_DIM + DECAY_RANK + AAA_RANK]

kernel_name = "hybrid_rwkv7_stickbreaking_macaron_adaln"


def rms_norm(x, g):
    xf = x.astype(jnp.float32)
    y = xf * lax.rsqrt(jnp.mean(xf * xf, axis=-1, keepdims=True) + RMS_EPS)
    return (y * g.astype(jnp.float32)).astype(x.dtype)


def modulate(xn, shift, scale):
    return xn * (1.0 + scale[:, None, :]) + shift[:, None, :]


def swiglu(x, wg, wu, wd):
    return (jax.nn.silu(x @ wg) * (x @ wu)) @ wd


def token_shift(p):
    return jnp.pad(p[:, :-1], ((0, 0), (1, 0), (0, 0)))


def rwkv7_scan(r, w, k, v, a, b):
    bsz, _, nh, dh = r.shape
    xs = tuple(jnp.moveaxis(t, 1, 0) for t in (r, w, k, v, a, b))

    def step(state, inp):
        r_t, w_t, k_t, v_t, a_t, b_t = inp
        sa = jnp.einsum('bhvk,bhk->bhv', state, a_t)
        state = (state * w_t[:, :, None, :] + sa[..., None] * b_t[:, :, None, :]
                 + v_t[..., None] * k_t[:, :, None, :])
        y = jnp.einsum('bhvk,bhk->bhv', state, r_t)
        return state, y

    s0 = jnp.zeros((bsz, nh, dh, dh), jnp.float32)
    _, ys = lax.scan(step, s0, xs)
    return jnp.moveaxis(ys, 0, 1)


def rwkv7_branch(p, p_vres, v_first, mu, w0, w2, a0, a2, g2, k_k, k_a, r_k, lnx_w, lnx_b, v0, v2):
    bsz, slen, _ = p.shape
    f32 = jnp.float32
    p = p + (token_shift(p) - p) * mu
    r, k, v, wl, al, gl = jnp.split(p, RWKV_SPLITS, axis=-1)
    w = -jax.nn.softplus(-(w0 + jnp.tanh(wl) @ w2)) - 0.5
    decay = jnp.exp(-jnp.exp(w.astype(f32)))
    if v_first is None:
        v_first = v
    else:
        v = v + (v_first - v) * jax.nn.sigmoid(v0 + p_vres @ v2)
    a = jax.nn.sigmoid(a0 + al @ a2)
    g = jax.nn.sigmoid(gl) @ g2

    def heads(t):
        return t.astype(f32).reshape(bsz, slen, RWKV_HEADS, HEAD_DIM)

    kk = heads(k * k_k)
    kk = kk / jnp.maximum(jnp.sqrt(jnp.sum(kk * kk, axis=-1, keepdims=True)), 1e-12)
    k = k * (1.0 + (a - 1.0) * k_a)
    rh, kh, vh = heads(r), heads(k), heads(v)
    y = rwkv7_scan(rh, heads(decay), kh, vh, -kk, kk * heads(a))
    mean = jnp.mean(y, axis=-1, keepdims=True)
    var = jnp.mean(jnp.square(y - mean), axis=-1, keepdims=True)
    y = ((y - mean) * lax.rsqrt(var + GN_EPS)).reshape(bsz, slen, RWKV_DIM)
    y = y * lnx_w.astype(f32) + lnx_b.astype(f32)
    bonus = jnp.sum(rh * kh * r_k.astype(f32), axis=-1, keepdims=True) * vh
    y = (y + bonus.reshape(bsz, slen, RWKV_DIM)) * g.astype(f32)
    return y.astype(p.dtype), v_first


def stick_breaking_attention(q, k, v):
    bsz, nh, slen, dh = q.shape
    f32 = jnp.float32
    q, k, v = q.astype(f32), k.astype(f32), v.astype(f32)
    n_blk = slen // SB_BLOCK
    inv_sqrt_d = 1.0 / math.sqrt(dh)
    offs = jnp.arange(SB_BLOCK)

    def q_block(qi):
        q0 = qi * SB_BLOCK
        qb = lax.dynamic_slice_in_dim(q, q0, SB_BLOCK, axis=2) * inv_sqrt_d
        qpos = q0 + offs

        def body(i, carry):
            acc, logsurv = carry
            k0 = (qi - i) * SB_BLOCK
            kb = lax.dynamic_slice_in_dim(k, k0, SB_BLOCK, axis=2)
            vb = lax.dynamic_slice_in_dim(v, k0, SB_BLOCK, axis=2)
            kpos = k0 + offs
            z = jnp.einsum('bhqd,bhkd->bhqk', qb, kb)
            mask = kpos[None, :] < qpos[:, None]
            log_fail = jnp.where(mask, -jax.nn.softplus(z), 0.0)
            csum = jnp.cumsum(log_fail, axis=-1)
            row = csum[..., -1:]
            log_a = jax.nn.log_sigmoid(z) + logsurv[..., None] + (row - csum)
            weight = jnp.where(mask, jnp.exp(log_a), 0.0)
            acc = acc + jnp.einsum('bhqk,bhkd->bhqd', weight, vb)
            return acc, logsurv + row[..., 0]

        acc0 = jnp.zeros((bsz, nh, SB_BLOCK, dh), f32)
        ls0 = jnp.zeros((bsz, nh, SB_BLOCK), f32)
        acc, _ = lax.fori_loop(0, qi + 1, body, (acc0, ls0))
        return acc

    out = lax.map(q_block, jnp.arange(n_blk))
    return out.transpose(1, 0, 3, 2, 4).reshape(bsz, slen, nh * dh)


def setup_inputs(seed: int = 0) -> dict:
    key = jax.random.key(seed)
    ks = iter(jax.random.split(key, 40))
    f32 = jnp.float32

    def nrm(shape, scale):
        return jax.random.normal(next(ks), shape, f32) * scale

    def gain(shape):
        return 1.0 + nrm(shape, 0.02)

    L = DEPTH
    d = D_MODEL
    return {
        "x": nrm((BATCH, SEQ, d), 1.0),
        "c": nrm((BATCH, d), 1.0),
        "mod_w": nrm((L, d, N_MOD * d), d ** -0.5),
        "mod_b": nrm((L, N_MOD * d), 0.01),
        "norm_g": gain((L, 3, d)),
        "ffn1_wg": nrm((L, d, D_FF), d ** -0.5),
        "ffn1_wu": nrm((L, d, D_FF), d ** -0.5),
        "ffn1_wd": nrm((L, D_FF, d), D_FF ** -0.5),
        "w_in_first": nrm((d, IN_COLS_FIRST), d ** -0.5),
        "w_in_rest": nrm((L - 1, d, IN_COLS), d ** -0.5),
        "shift_mu": jax.random.uniform(next(ks), (L, RWKV_COLS), f32),
        "decay_w0": jax.random.uniform(next(ks), (L, RWKV_DIM), f32, -6.0, -1.0),
        "decay_w2": nrm((L, DECAY_RANK, RWKV_DIM), 0.5 * DECAY_RANK ** -0.5),
        "aaa_a0": nrm((L, RWKV_DIM), 0.1),
        "aaa_a2": nrm((L, AAA_RANK, RWKV_DIM), AAA_RANK ** -0.5),
        "gate_g2": nrm((L, GATE_RANK, RWKV_DIM), GATE_RANK ** -0.5),
        "k_k": 0.85 + nrm((L, RWKV_DIM), 0.05),
        "k_a": 1.0 + nrm((L, RWKV_DIM), 0.05),
        "r_k": nrm((L, RWKV_HEADS, HEAD_DIM), 0.1),
        "lnx_w": 1.0 + nrm((L, RWKV_DIM), 0.05),
        "lnx_b": nrm((L, RWKV_DIM), 0.01),
        "vres_v0": 0.5 + nrm((L - 1, RWKV_DIM), 0.1),
        "vres_v2": nrm((L - 1, VRES_RANK, RWKV_DIM), VRES_RANK ** -0.5),
        "q_norm_g": gain((L, HEAD_DIM)),
        "k_norm_g": gain((L, HEAD_DIM)),
        "w_up_a": nrm((L, RWKV_DIM, d), RWKV_DIM ** -0.5),
        "w_up_b": nrm((L, SB_DIM, d), SB_DIM ** -0.5),
        "w_out": nrm((L, d, d), d ** -0.5),
        "ffn2_wg": nrm((L, d, D_FF), d ** -0.5),
        "ffn2_wu": nrm((L, d, D_FF), d ** -0.5),
        "ffn2_wd": nrm((L, D_FF, d), D_FF ** -0.5),
    }


def reference(x, c, mod_w, mod_b, norm_g, ffn1_wg, ffn1_wu, ffn1_wd, w_in_first, w_in_rest,
              shift_mu, decay_w0, decay_w2, aaa_a0, aaa_a2, gate_g2, k_k, k_a, r_k, lnx_w, lnx_b,
              vres_v0, vres_v2, q_norm_g, k_norm_g, w_up_a, w_up_b, w_out,
              ffn2_wg, ffn2_wu, ffn2_wd):
    bsz, slen, _ = x.shape
    h = x
    c_act = jax.nn.silu(c)
    v_first = None
    o_sb = RWKV_COLS
    o_gate = RWKV_COLS + SB_COLS
    for l in range(DEPTH):
        mod = (c_act @ mod_w[l] + mod_b[l]).reshape(bsz, N_MOD, D_MODEL)
        sh1, sc1, gt1, sh2, sc2, gt2, sh3, sc3, gt3 = [mod[:, i] for i in range(N_MOD)]

        n1 = modulate(rms_norm(h, norm_g[l, 0]), sh1, sc1)
        h = h + 0.5 * gt1[:, None, :] * swiglu(n1, ffn1_wg[l], ffn1_wu[l], ffn1_wd[l])

        n2 = modulate(rms_norm(h, norm_g[l, 1]), sh2, sc2)
        if l == 0:
            proj = n2 @ w_in_first
            p_vres, v0, v2 = None, None, None
        else:
            proj = n2 @ w_in_rest[l - 1]
            p_vres, v0, v2 = proj[..., IN_COLS_FIRST:], vres_v0[l - 1], vres_v2[l - 1]

        y_a, v_first = rwkv7_branch(proj[..., :RWKV_COLS], p_vres, v_first, shift_mu[l],
                                    decay_w0[l], decay_w2[l], aaa_a0[l], aaa_a2[l], gate_g2[l],
                                    k_k[l], k_a[l], r_k[l], lnx_w[l], lnx_b[l], v0, v2)

        p_sb = proj[..., o_sb:o_gate].reshape(bsz, slen, 3, SB_HEADS, HEAD_DIM)
        q = rms_norm(p_sb[:, :, 0], q_norm_g[l]).transpose(0, 2, 1, 3)
        k = rms_norm(p_sb[:, :, 1], k_norm_g[l]).transpose(0, 2, 1, 3)
        v = p_sb[:, :, 2].transpose(0, 2, 1, 3)
        y_b = stick_breaking_attention(q, k, v).astype(h.dtype)

        gate_a = jax.nn.sigmoid(proj[..., o_gate:o_gate + D_MODEL])
        gate_b = jax.nn.sigmoid(proj[..., o_gate + D_MODEL:o_gate + 2 * D_MODEL])
        merged = gate_a * (y_a @ w_up_a[l]) + gate_b * (y_b @ w_up_b[l])
        h = h + gt2[:, None, :] * (merged @ w_out[l])

        n3 = modulate(rms_norm(h, norm_g[l, 2]), sh3, sc3)
        h = h + 0.5 * gt3[:, None, :] * swiglu(n3, ffn2_wg[l], ffn2_wu[l], ffn2_wd[l])
    return h
```

```python
import functools
import math

import jax
import jax.numpy as jnp
from jax import lax
from jax.experimental import pallas as pl
from jax.experimental.pallas import tpu as pltpu

F32 = jnp.float32
BF16 = jnp.bfloat16

HEAD_DIM = 64
RMS_EPS = 1e-6
GN_EPS = 64e-5
N_MOD = 9
LANES = 128
VMEM_LIMIT = 56 * 1024 * 1024
RWKV_CHUNK = 64

_NN = (((1,), (0,)), ((), ()))
_NT = (((1,), (1,)), ((), ()))


def _mm(a, b, dims=_NN):
    return lax.dot_general(a, b, dims, preferred_element_type=F32)


def _split2(x):
    hi = x.astype(BF16)
    lo = (x - hi.astype(F32)).astype(BF16)
    return hi, lo


def _split3(x):
    hi = x.astype(BF16)
    r1 = x - hi.astype(F32)
    mid = r1.astype(BF16)
    lo = (r1 - mid.astype(F32)).astype(BF16)
    return hi, mid, lo


def _mm3(a, b, dims=_NN):
    ah, al = _split2(a)
    bh, bl = _split2(b)
    return _mm(ah, bh, dims) + (_mm(ah, bl, dims) + _mm(al, bh, dims))


def _mm_lhs_exact(a_bf16, b):
    hi, mid, lo = _split3(b)
    return _mm(a_bf16, hi) + (_mm(a_bf16, mid) + _mm(a_bf16, lo))


def _mm_rhs_exact(a, b_bf16):
    hi, mid, lo = _split3(a)
    return _mm(hi, b_bf16) + (_mm(mid, b_bf16) + _mm(lo, b_bf16))


def _params(*sem):
    return pltpu.CompilerParams(dimension_semantics=sem, vmem_limit_bytes=VMEM_LIMIT)


def _resident(shape):
    zeros = (0,) * len(shape)
    return pl.BlockSpec(shape, lambda *_: zeros, pipeline_mode=pl.Buffered(1))


def _norm_mod(x, g, sh, sc):
    ms = jnp.mean(x * x, axis=-1, keepdims=True)
    y = x * lax.rsqrt(ms + RMS_EPS) * g
    return y * (1.0 + sc) + sh


def _softplus(x):
    return jnp.maximum(x, 0.0) + jnp.log(1.0 + jnp.exp(-jnp.abs(x)))


def _mod_body(c_ref, w_ref, b_ref, o_ref):
    c = c_ref[...]
    ca = (c * jax.nn.sigmoid(c)).astype(BF16)
    o_ref[0] = _mm(ca, w_ref[0].astype(BF16)) + b_ref[0]


def _mod_call(c, mod_w, mod_b):
    n_layers, d, nd = mod_w.shape
    bsz = c.shape[0]
    tn = d
    return pl.pallas_call(
        _mod_body,
        grid=(n_layers, nd // tn),
        in_specs=[
            pl.BlockSpec((bsz, d), lambda l, j: (0, 0)),
            pl.BlockSpec((1, d, tn), lambda l, j: (l, 0, j)),
            pl.BlockSpec((1, 1, tn), lambda l, j: (l, 0, j)),
        ],
        out_specs=pl.BlockSpec((1, bsz, tn), lambda l, j: (l, 0, j)),
        out_shape=jax.ShapeDtypeStruct((n_layers, bsz, nd), F32),
        compiler_params=_params("parallel", "parallel"),
        name="adaln_mod",
    )(c, mod_w, mod_b.reshape(n_layers, 1, nd))


def _ffn_body(h_ref, g_ref, mod_ref, wg_ref, wu_ref, wd_ref, o_ref, *, sub, tf):
    x = h_ref[0]
    g = g_ref[sub:sub + 1, :]
    sh = mod_ref[0, 3 * sub:3 * sub + 1, :]
    sc = mod_ref[0, 3 * sub + 1:3 * sub + 2, :]
    gt = mod_ref[0, 3 * sub + 2:3 * sub + 3, :]
    n = _norm_mod(x, g, sh, sc).astype(BF16)
    d_ff = wg_ref.shape[1]
    acc = jnp.zeros_like(x)
    for j in range(d_ff // tf):
        gj = _mm(n, wg_ref[:, j * tf:(j + 1) * tf])
        uj = _mm(n, wu_ref[:, j * tf:(j + 1) * tf])
        a = (gj * jax.nn.sigmoid(gj) * uj).astype(BF16)
        acc = acc + _mm(a, wd_ref[j * tf:(j + 1) * tf, :])
    o_ref[0] = x + (0.5 * gt) * acc


def _ffn_call(h, norm_g, mod, wg, wu, wd, *, sub, tm, tf):
    bsz, slen, d = h.shape
    d_ff = wg.shape[1]
    tok = pl.BlockSpec((1, tm, d), lambda b, i: (b, i, 0))
    return pl.pallas_call(
        functools.partial(_ffn_body, sub=sub, tf=tf),
        grid=(bsz, slen // tm),
        in_specs=[
            tok,
            _resident(norm_g.shape),
            pl.BlockSpec((1, N_MOD, d), lambda b, i: (b, 0, 0)),
            _resident((d, d_ff)), _resident((d, d_ff)), _resident((d_ff, d)),
        ],
        out_specs=tok,
        out_shape=jax.ShapeDtypeStruct(h.shape, F32),
        compiler_params=_params("parallel", "parallel"),
        name="ffn",
    )(h, norm_g, mod, wg, wu, wd)


def _head_rms(t, g, gmean):
    hi, lo = _split2(t * t)
    ms = _mm(hi, gmean) + _mm(lo, gmean)
    return t * lax.rsqrt(ms + RMS_EPS) * g


def _inproj_body(h_ref, g_ref, mod_ref, w_ref, gq_ref, gk_ref, gmean_ref, *outs,
                 rw, hd, d, has_vres):
    p_ref, q_ref, k_ref, v_ref, gate_ref = outs[:5]
    x = h_ref[0]
    n = _norm_mod(x, g_ref[1:2, :], mod_ref[0, 3:4, :], mod_ref[0, 4:5, :]).astype(BF16)
    gmean = gmean_ref[...]
    p_ref[0] = _mm(n, w_ref[:, 0:rw])
    q = _mm(n, w_ref[:, rw:rw + hd])
    q_ref[0] = (_head_rms(q, gq_ref[...], gmean) * (1.0 / math.sqrt(HEAD_DIM))).astype(BF16)
    k = _mm(n, w_ref[:, rw + hd:rw + 2 * hd])
    k_ref[0] = _head_rms(k, gk_ref[...], gmean).astype(BF16)
    v_ref[0] = _mm(n, w_ref[:, rw + 2 * hd:rw + 3 * hd]).astype(BF16)
    og = rw + 3 * hd
    gate_ref[0] = jax.nn.sigmoid(_mm(n, w_ref[:, og:og + 2 * d])).astype(BF16)
    if has_vres:
        outs[5][0] = _mm(n, w_ref[:, og + 2 * d:og + 2 * d + LANES])


def _inproj_call(h, norm_g, mod, w, gq, gk, gmean, *, rw, hd, has_vres, tm):
    bsz, slen, d = h.shape

    def tok(width):
        return pl.BlockSpec((1, tm, width), lambda b, i: (b, i, 0))

    widths = [(rw, F32), (hd, BF16), (hd, BF16), (hd, BF16), (2 * d, BF16)]
    if has_vres:
        widths.append((LANES, F32))
    return pl.pallas_call(
        functools.partial(_inproj_body, rw=rw, hd=hd, d=d, has_vres=has_vres),
        grid=(bsz, slen // tm),
        in_specs=[
            tok(d),
            _resident(norm_g.shape),
            pl.BlockSpec((1, N_MOD, d), lambda b, i: (b, 0, 0)),
            _resident(w.shape), _resident(gq.shape), _resident(gk.shape), _resident(gmean.shape),
        ],
        out_specs=[tok(wd) for wd, _ in widths],
        out_shape=[jax.ShapeDtypeStruct((bsz, slen, wd), dt) for wd, dt in widths],
        compiler_params=_params("parallel", "parallel"),
        name="in_proj",
    )(h, norm_g, mod, w, gq, gk, gmean)


def _attn_body(q_ref, k_ref, v_ref, o_ref, *, t):
    qi = pl.program_id(2)
    q = q_ref[0]
    lane = lax.broadcasted_iota(jnp.int32, (1, LANES), 1)
    first = lane < HEAD_DIM
    zero = jnp.zeros_like(q)
    q_heads = (jnp.where(first, q, zero), jnp.where(first, zero, q))
    row = lax.broadcasted_iota(jnp.int32, (t, t), 0)
    col = lax.broadcasted_iota(jnp.int32, (t, t), 1)
    causal = col < row
    later = jnp.where(row > col, 1.0, 0.0).astype(BF16)

    def block(k0, carry, diag):
        kb = k_ref[0, pl.ds(k0, t), :]
        vb = v_ref[0, pl.ds(k0, t), :]
        out = []
        for e in range(2):
            acc, logsurv = carry[2 * e], carry[2 * e + 1]
            z = _mm(q_heads[e], kb, _NT)
            sp = _softplus(z)
            log_fail = -sp
            if diag:
                log_fail = jnp.where(causal, log_fail, 0.0)
            hi, lo = _split2(log_fail)
            suffix = _mm(hi, later) + _mm(lo, later)
            log_a = (z - sp) + logsurv + suffix
            w = jnp.exp(log_a)
            if diag:
                w = jnp.where(causal, w, 0.0)
            acc = acc + _mm(w.astype(BF16), vb)
            logsurv = logsurv + jnp.sum(log_fail, axis=-1, keepdims=True)
            out += [acc, logsurv]
        return tuple(out)

    acc0 = jnp.zeros((t, LANES), F32)
    ls0 = jnp.zeros((t, 1), F32)
    carry = block(pl.multiple_of(qi * t, t), (acc0, ls0, acc0, ls0), True)

    def body(i, carry):
        return block(pl.multiple_of((qi - i) * t, t), carry, False)

    carry = lax.fori_loop(1, qi + 1, body, carry)
    o_ref[0] = jnp.where(first, carry[0], carry[2]).astype(o_ref.dtype)


def _attn_call(q, k, v, *, t):
    bsz, slen, hd = q.shape
    qblk = pl.BlockSpec((1, t, LANES), lambda b, p, i: (b, i, p))
    kvblk = pl.BlockSpec((1, slen, LANES), lambda b, p, i: (b, 0, p))
    return pl.pallas_call(
        functools.partial(_attn_body, t=t),
        grid=(bsz, hd // LANES, slen // t),
        in_specs=[qblk, kvblk, kvblk],
        out_specs=qblk,
        out_shape=jax.ShapeDtypeStruct(q.shape, BF16),
        compiler_params=_params("parallel", "parallel", "arbitrary"),
        name="stick_breaking_attn",
    )(q, k, v)


def _rwkv_prep_body(*refs, hd, has_vres):
    if has_vres:
        (p_ref, prev_ref, tail_ref, vf_ref, mu_ref, w0_ref, lora_ref, a0_ref, g2_ref, kk_ref,
         ka_ref, gsum_ref, v0_ref, v2_ref, r_o, lw_o, k_o, v_o, a_o, b_o, g_o) = refs
    else:
        (p_ref, prev_ref, mu_ref, w0_ref, lora_ref, a0_ref, g2_ref, kk_ref,
         ka_ref, gsum_ref, r_o, lw_o, k_o, v_o, a_o, b_o, g_o) = refs
    i = pl.program_id(1)
    p = p_ref[0]
    ts = p.shape[0]
    prev = prev_ref[0][7:8, :]
    prev = jnp.where(i == 0, jnp.zeros_like(prev), prev)
    shifted = pltpu.roll(p, 1, axis=0)
    rows = lax.broadcasted_iota(jnp.int32, (ts, 1), 0)
    shifted = jnp.where(rows == 0, prev, shifted)
    p = p + (shifted - p) * mu_ref[...]
    r = p[:, 0:hd]
    k = p[:, hd:2 * hd]
    v = p[:, 2 * hd:3 * hd]
    wa = p[:, 3 * hd:3 * hd + LANES]
    gl = p[:, 3 * hd + LANES:3 * hd + 2 * LANES]
    lane = lax.broadcasted_iota(jnp.int32, (1, LANES), 1)
    wa = jnp.where(lane < wa.shape[1] // 2, jnp.tanh(wa), wa)
    lora = _mm3(wa, lora_ref[...])
    w = -_softplus(-(w0_ref[...] + lora[:, 0:hd])) - 0.5
    lw_o[0] = -jnp.exp(w)
    a = jax.nn.sigmoid(a0_ref[...] + lora[:, hd:2 * hd])
    g_o[0] = _mm(jax.nn.sigmoid(gl).astype(BF16), g2_ref[...])
    if has_vres:
        mix = jax.nn.sigmoid(v0_ref[...] + _mm(tail_ref[0].astype(BF16), v2_ref[...]))
        v = v + (vf_ref[0] - v) * mix
    kk = k * kk_ref[...]
    ss = _mm_rhs_exact(kk * kk, gsum_ref[...])
    kk = kk / jnp.maximum(jnp.sqrt(ss), 1e-12)
    r_o[0] = r
    k_o[0] = k * (1.0 + (a - 1.0) * ka_ref[...])
    v_o[0] = v
    a_o[0] = -kk
    b_o[0] = kk * a


def _rwkv_prep_call(p, tail, v_first, mu, w0, lora_w, a0, g2, k_k, k_a, gsum, v0, v2, *, hd, ts):
    bsz, slen, rw = p.shape
    has_vres = tail is not None

    def tok(width):
        return pl.BlockSpec((1, ts, width), lambda b, i: (b, i, 0))

    prev = pl.BlockSpec((1, 8, rw), lambda b, i: (b, jnp.maximum(i * (ts // 8) - 1, 0), 0))
    args = [p, p]
    specs = [tok(rw), prev]
    if has_vres:
        args += [tail, v_first]
        specs += [tok(LANES), tok(hd)]
    consts = [mu, w0, lora_w, a0, g2, k_k, k_a, gsum]
    if has_vres:
        consts += [v0, v2]
    args += consts
    specs += [_resident(c.shape) for c in consts]
    return pl.pallas_call(
        functools.partial(_rwkv_prep_body, hd=hd, has_vres=has_vres),
        grid=(bsz, slen // ts),
        in_specs=specs,
        out_specs=[tok(hd)] * 7,
        out_shape=[jax.ShapeDtypeStruct((bsz, slen, hd), F32)] * 7,
        compiler_params=_params("parallel", "parallel"),
        name="rwkv_prep",
    )(*args)


def _rwkv_chunk_body(r_ref, lw_ref, k_ref, v_ref, a_ref, b_ref, g_ref, tri_ref, gmean_ref,
                     rk_ref, lnw_ref, lnb_ref, o_ref, st_ref, *, n_heads):
    c = pl.program_id(1)

    @pl.when(c == 0)
    def _():
        st_ref[...] = jnp.zeros_like(st_ref)

    r, lw, k, v = r_ref[0], lw_ref[0], k_ref[0], v_ref[0]
    a, b = a_ref[0], b_ref[0]
    n_tok = r.shape[0]
    cum = _mm_lhs_exact(tri_ref[...], lw)
    cum_prev = cum - lw
    last = cum[n_tok - 1:n_tok, :]
    e_pos = jnp.exp(cum)
    e_neg = jnp.exp(-cum)
    e_end = jnp.exp(last - cum)
    at = a * jnp.exp(cum_prev)
    rt = r * e_pos
    bt = b * e_neg
    kt = k * e_neg
    bh = b * e_end
    kh = k * e_end
    p_end = jnp.exp(last)

    row = lax.broadcasted_iota(jnp.int32, (n_tok, n_tok), 0)
    col = lax.broadcasted_iota(jnp.int32, (n_tok, n_tok), 1)
    strict = row > col
    lower = row >= col
    erow = lax.broadcasted_iota(jnp.int32, (HEAD_DIM, HEAD_DIM), 0)
    ecol = lax.broadcasted_iota(jnp.int32, (HEAD_DIM, HEAD_DIM), 1)
    eye_c = jnp.where(row == col, 1.0, 0.0)
    n_double = int(math.log2(n_tok)) - 1

    ys = []
    for h in range(n_heads):
        s = slice(h * HEAD_DIM, (h + 1) * HEAD_DIM)
        at_h, rt_h, bt_h, kt_h, bh_h, kh_h, v_h = (t[:, s] for t in (at, rt, bt, kt, bh, kh, v))
        lhs = jnp.concatenate([at_h, rt_h], axis=0)
        prod_b = _mm3(lhs, bt_h, _NT)
        prod_k = _mm3(lhs, kt_h, _NT)
        a_ab = jnp.where(strict, prod_b[:n_tok], 0.0)
        a_rb = jnp.where(lower, prod_b[n_tok:], 0.0)
        a_ak = jnp.where(strict, prod_k[:n_tok], 0.0)
        a_rk = jnp.where(lower, prod_k[n_tok:], 0.0)
        tinv = eye_c + a_ab
        power = a_ab
        for _ in range(n_double):
            power = _mm3(power, power)
            tinv = tinv + _mm3(power, tinv)
        w_h = _mm3(tinv, at_h)
        x_h = _mm3(tinv, _mm3(a_ak, v_h))
        q_h = rt_h + _mm3(a_rb, w_h)
        y_intra = _mm3(a_rb, x_h) + _mm3(a_rk, v_h)
        st = st_ref[h]
        ys.append(_mm3(q_h, st) + y_intra)
        bh_t = bh_h.T
        m_t = jnp.where(erow == ecol, p_end[:, s], 0.0) + _mm3(bh_t, w_h)
        n_t = _mm3(bh_t, x_h) + _mm3(kh_h.T, v_h)
        st_ref[h] = _mm3(m_t, st) + n_t
    y = jnp.concatenate(ys, axis=1)

    gmean = gmean_ref[...]
    mean = _mm_rhs_exact(y, gmean)
    dev = y - mean
    var = _mm_rhs_exact(dev * dev, gmean)
    yn = dev * lax.rsqrt(var + GN_EPS) * lnw_ref[...] + lnb_ref[...]
    bonus = _mm_rhs_exact(r * k * rk_ref[...], gmean) * float(HEAD_DIM) * v
    o_ref[0] = ((yn + bonus) * g_ref[0]).astype(o_ref.dtype)


def _rwkv_chunk_call(r, lw, k, v, a, b, g, tri, gmean, rk, lnw, lnb):
    bsz, slen, hd = r.shape
    chunk = tri.shape[0]
    n_heads = hd // HEAD_DIM
    tok = pl.BlockSpec((1, chunk, hd), lambda bi, ci: (bi, ci, 0))
    consts = [tri, gmean, rk, lnw, lnb]
    return pl.pallas_call(
        functools.partial(_rwkv_chunk_body, n_heads=n_heads),
        grid=(bsz, slen // chunk),
        in_specs=[tok] * 7 + [_resident(c.shape) for c in consts],
        out_specs=tok,
        out_shape=jax.ShapeDtypeStruct((bsz, slen, hd), BF16),
        scratch_shapes=[pltpu.VMEM((n_heads, HEAD_DIM, HEAD_DIM), F32)],
        compiler_params=_params("parallel", "arbitrary"),
        name="rwkv_chunk_scan",
    )(r, lw, k, v, a, b, g, *consts)


def _merge_body(h_ref, ya_ref, yb_ref, gate_ref, mod_ref, wa_ref, wb_ref, wo_ref, o_ref):
    d = h_ref.shape[-1]
    ua = _mm(ya_ref[0], wa_ref[...])
    ub = _mm(yb_ref[0], wb_ref[...])
    merged = gate_ref[0, :, 0:d].astype(F32) * ua + gate_ref[0, :, d:2 * d].astype(F32) * ub
    out = _mm(merged.astype(BF16), wo_ref[...])
    o_ref[0] = h_ref[0] + mod_ref[0, 5:6, :] * out


def _merge_call(h, ya, yb, gates, mod, wa, wb, wo, *, tm):
    bsz, slen, d = h.shape

    def tok(width):
        return pl.BlockSpec((1, tm, width), lambda b, i: (b, i, 0))

    return pl.pallas_call(
        _merge_body,
        grid=(bsz, slen // tm),
        in_specs=[
            tok(d), tok(ya.shape[-1]), tok(yb.shape[-1]), tok(2 * d),
            pl.BlockSpec((1, N_MOD, d), lambda b, i: (b, 0, 0)),
            _resident(wa.shape), _resident(wb.shape), _resident(wo.shape),
        ],
        out_specs=tok(d),
        out_shape=jax.ShapeDtypeStruct(h.shape, F32),
        compiler_params=_params("parallel", "parallel"),
        name="merge_out",
    )(h, ya, yb, gates, mod, wa, wb, wo)


def _block_diag_const(width, group, value, dtype):
    idx = jnp.arange(width) // group
    return jnp.where(idx[:, None] == idx[None, :], value, 0.0).astype(dtype)


def kernel(x, c, mod_w, mod_b, norm_g, ffn1_wg, ffn1_wu, ffn1_wd, w_in_first, w_in_rest, shift_mu, decay_w0, decay_w2, aaa_a0, aaa_a2, gate_g2, k_k, k_a, r_k, lnx_w, lnx_b, vres_v0, vres_v2, q_norm_g, k_norm_g, w_up_a, w_up_b, w_out, ffn2_wg, ffn2_wu, ffn2_wd):
    bsz, slen, d = x.shape
    n_layers = mod_w.shape[0]
    hd = decay_w0.shape[-1]
    n_heads = hd // HEAD_DIM
    decay_rank, aaa_rank = decay_w2.shape[1], aaa_a2.shape[1]
    gate_rank, vres_rank = gate_g2.shape[1], vres_v2.shape[1]
    assert decay_rank + aaa_rank == LANES and gate_rank == LANES and vres_rank <= LANES
    assert w_up_b.shape[1] == hd and hd % LANES == 0
    rw = 3 * hd + decay_rank + aaa_rank + gate_rank
    cols_first = rw + 3 * hd + 2 * d
    assert w_in_first.shape[1] == cols_first

    tm = min(512, slen)
    t_attn = min(128, slen)
    chunk = min(RWKV_CHUNK, slen)

    gmean = _block_diag_const(hd, HEAD_DIM, 1.0 / HEAD_DIM, BF16)
    gsum = _block_diag_const(hd, HEAD_DIM, 1.0, BF16)
    tri = jnp.tril(jnp.ones((chunk, chunk), F32)).astype(BF16)

    mod_all = _mod_call(c, mod_w, mod_b).reshape(n_layers, bsz, N_MOD, d)

    h = x
    v_first = None
    for l in range(n_layers):
        mod = mod_all[l]
        h = _ffn_call(h, norm_g[l], mod, ffn1_wg[l].astype(BF16), ffn1_wu[l].astype(BF16),
                      ffn1_wd[l].astype(BF16), sub=0, tm=tm, tf=256)

        has_vres = l > 0
        if has_vres:
            w_in = jnp.pad(w_in_rest[l - 1], ((0, 0), (0, LANES - vres_rank)))
        else:
            w_in = w_in_first
        gq = jnp.tile(q_norm_g[l], n_heads)[None, :]
        gk = jnp.tile(k_norm_g[l], n_heads)[None, :]
        proj = _inproj_call(h, norm_g[l], mod, w_in.astype(BF16), gq, gk, gmean,
                            rw=rw, hd=hd, has_vres=has_vres, tm=tm)
        p_rwkv, q, k, v, gates = proj[:5]
        tail = proj[5] if has_vres else None

        lora_w = jnp.zeros((LANES, 2 * hd), F32)
        lora_w = lora_w.at[:decay_rank, :hd].set(decay_w2[l]).at[decay_rank:, hd:].set(aaa_a2[l])
        if has_vres:
            v0 = vres_v0[l - 1][None, :]
            v2 = jnp.pad(vres_v2[l - 1], ((0, LANES - vres_rank), (0, 0))).astype(BF16)
        else:
            v0 = v2 = None
        r_, lw_, k_, v_, a_, b_, g_ = _rwkv_prep_call(
            p_rwkv, tail, v_first, shift_mu[l][None, :], decay_w0[l][None, :], lora_w,
            aaa_a0[l][None, :], gate_g2[l].astype(BF16), k_k[l][None, :], k_a[l][None, :], gsum,
            v0, v2, hd=hd, ts=tm)
        if not has_vres:
            v_first = v_
        y_a = _rwkv_chunk_call(r_, lw_, k_, v_, a_, b_, g_, tri, gmean, r_k[l].reshape(1, hd),
                               lnx_w[l][None, :], lnx_b[l][None, :])

        y_b = _attn_call(q, k, v, t=t_attn)

        h = _merge_call(h, y_a, y_b, gates, mod, w_up_a[l].astype(BF16), w_up_b[l].astype(BF16),
                        w_out[l].astype(BF16), tm=tm)

        h = _ffn_call(h, norm_g[l], mod, ffn2_wg[l].astype(BF16), ffn2_wu[l].astype(BF16),
                      ffn2_wd[l].astype(BF16), sub=2, tm=tm, tf=256)
    return h
```

```python
import functools
import math

import jax
import jax.numpy as jnp
from jax import lax
from jax.experimental import pallas as pl
from jax.experimental.pallas import tpu as pltpu

F32 = jnp.float32
BF16 = jnp.bfloat16

HEAD_DIM = 64
RMS_EPS = 1e-6
GN_EPS = 64e-5
N_MOD = 9
LANES = 128
VMEM_LIMIT = 56 * 1024 * 1024
RWKV_CHUNK = 64
RWKV_PASSES = (1, 1, 1, 1, 3)

_NN = (((1,), (0,)), ((), ()))
_NT = (((1,), (1,)), ((), ()))


def _mm(a, b, dims=_NN):
    return lax.dot_general(a, b, dims, preferred_element_type=F32)


def _split2(x):
    hi = x.astype(BF16)
    lo = (x - hi.astype(F32)).astype(BF16)
    return hi, lo


def _split3(x):
    hi = x.astype(BF16)
    r1 = x - hi.astype(F32)
    mid = r1.astype(BF16)
    lo = (r1 - mid.astype(F32)).astype(BF16)
    return hi, mid, lo


def _mm3(a, b, dims=_NN):
    ah, al = _split2(a)
    bh, bl = _split2(b)
    return _mm(ah, bh, dims) + (_mm(ah, bl, dims) + _mm(al, bh, dims))


def _mmp(a, b, passes, dims=_NN):
    if passes == 1:
        return _mm(a.astype(BF16), b.astype(BF16), dims)
    return _mm3(a, b, dims)


def _mm_lhs_exact(a_bf16, b):
    hi, mid, lo = _split3(b)
    return _mm(a_bf16, hi) + (_mm(a_bf16, mid) + _mm(a_bf16, lo))


def _mm_rhs_exact(a, b_bf16):
    hi, mid, lo = _split3(a)
    return _mm(hi, b_bf16) + (_mm(mid, b_bf16) + _mm(lo, b_bf16))


def _params(*sem):
    return pltpu.CompilerParams(dimension_semantics=sem, vmem_limit_bytes=VMEM_LIMIT)


def _resident(shape):
    zeros = (0,) * len(shape)
    return pl.BlockSpec(shape, lambda *_: zeros, pipeline_mode=pl.Buffered(1))


def _norm_mod(x, g, sh, sc):
    ms = jnp.mean(x * x, axis=-1, keepdims=True)
    y = x * lax.rsqrt(ms + RMS_EPS) * g
    return y * (1.0 + sc) + sh


LOG2E = 1.4426950408889634


def _log2(x):
    return jnp.log(x) * LOG2E


def _softplus(x):
    return jnp.maximum(x, 0.0) + jnp.log(1.0 + jnp.exp(-jnp.abs(x)))


def _mod_body(c_ref, w_ref, b_ref, o_ref):
    c = c_ref[...]
    ca = (c * jax.nn.sigmoid(c)).astype(BF16)
    o_ref[0] = _mm(ca, w_ref[0].astype(BF16)) + b_ref[0]


def _mod_call(c, mod_w, mod_b):
    n_layers, d, nd = mod_w.shape
    bsz = c.shape[0]
    tn = d
    return pl.pallas_call(
        _mod_body,
        grid=(n_layers, nd // tn),
        in_specs=[
            pl.BlockSpec((bsz, d), lambda l, j: (0, 0)),
            pl.BlockSpec((1, d, tn), lambda l, j: (l, 0, j)),
            pl.BlockSpec((1, 1, tn), lambda l, j: (l, 0, j)),
        ],
        out_specs=pl.BlockSpec((1, bsz, tn), lambda l, j: (l, 0, j)),
        out_shape=jax.ShapeDtypeStruct((n_layers, bsz, nd), F32),
        compiler_params=_params("parallel", "parallel"),
        name="adaln_mod",
    )(c, mod_w, mod_b.reshape(n_layers, 1, nd))


def _ffn_body(h_ref, g_ref, mod_ref, wg_ref, wu_ref, wd_ref, o_ref, *, sub, tf):
    x = h_ref[0]
    g = g_ref[sub:sub + 1, :]
    sh = mod_ref[0, 3 * sub:3 * sub + 1, :]
    sc = mod_ref[0, 3 * sub + 1:3 * sub + 2, :]
    gt = mod_ref[0, 3 * sub + 2:3 * sub + 3, :]
    n = _norm_mod(x, g, sh, sc).astype(BF16)
    d_ff = wg_ref.shape[1]
    acc = jnp.zeros_like(x)
    for j in range(d_ff // tf):
        gj = _mm(n, wg_ref[:, j * tf:(j + 1) * tf])
        uj = _mm(n, wu_ref[:, j * tf:(j + 1) * tf])
        a = (gj * jax.nn.sigmoid(gj) * uj).astype(BF16)
        acc = acc + _mm(a, wd_ref[j * tf:(j + 1) * tf, :])
    o_ref[0] = x + (0.5 * gt) * acc


def _ffn_call(h, norm_g, mod, wg, wu, wd, *, sub, tm, tf):
    bsz, slen, d = h.shape
    d_ff = wg.shape[1]
    tok = pl.BlockSpec((1, tm, d), lambda b, i: (b, i, 0))
    return pl.pallas_call(
        functools.partial(_ffn_body, sub=sub, tf=tf),
        grid=(bsz, slen // tm),
        in_specs=[
            tok,
            _resident(norm_g.shape),
            pl.BlockSpec((1, N_MOD, d), lambda b, i: (b, 0, 0)),
            _resident((d, d_ff)), _resident((d, d_ff)), _resident((d_ff, d)),
        ],
        out_specs=tok,
        out_shape=jax.ShapeDtypeStruct(h.shape, F32),
        compiler_params=_params("parallel", "parallel"),
        name="ffn",
    )(h, norm_g, mod, wg, wu, wd)


def _head_rms(t, g, gmean):
    hi, lo = _split2(t * t)
    ms = _mm(hi, gmean) + _mm(lo, gmean)
    return t * lax.rsqrt(ms + RMS_EPS) * g


def _inproj_body(h_ref, g_ref, mod_ref, w_ref, gq_ref, gk_ref, gmean_ref, *outs,
                 rw, hd, d, has_vres):
    p_ref, q_ref, k_ref, v_ref, gate_ref = outs[:5]
    x = h_ref[0]
    n = _norm_mod(x, g_ref[1:2, :], mod_ref[0, 3:4, :], mod_ref[0, 4:5, :]).astype(BF16)
    gmean = gmean_ref[...]
    p_ref[0] = _mm(n, w_ref[:, 0:rw])
    q = _mm(n, w_ref[:, rw:rw + hd])
    q_ref[0] = (_head_rms(q, gq_ref[...], gmean) * (LOG2E / math.sqrt(HEAD_DIM))).astype(BF16)
    k = _mm(n, w_ref[:, rw + hd:rw + 2 * hd])
    k_ref[0] = _head_rms(k, gk_ref[...], gmean).astype(BF16)
    v_ref[0] = _mm(n, w_ref[:, rw + 2 * hd:rw + 3 * hd]).astype(BF16)
    og = rw + 3 * hd
    gate_ref[0] = jax.nn.sigmoid(_mm(n, w_ref[:, og:og + 2 * d])).astype(BF16)
    if has_vres:
        outs[5][0] = _mm(n, w_ref[:, og + 2 * d:og + 2 * d + LANES])


def _inproj_call(h, norm_g, mod, w, gq, gk, gmean, *, rw, hd, has_vres, tm):
    bsz, slen, d = h.shape

    def tok(width):
        return pl.BlockSpec((1, tm, width), lambda b, i: (b, i, 0))

    widths = [(rw, F32), (hd, BF16), (hd, BF16), (hd, BF16), (2 * d, BF16)]
    if has_vres:
        widths.append((LANES, F32))
    return pl.pallas_call(
        functools.partial(_inproj_body, rw=rw, hd=hd, d=d, has_vres=has_vres),
        grid=(bsz, slen // tm),
        in_specs=[
            tok(d),
            _resident(norm_g.shape),
            pl.BlockSpec((1, N_MOD, d), lambda b, i: (b, 0, 0)),
            _resident(w.shape), _resident(gq.shape), _resident(gk.shape), _resident(gmean.shape),
        ],
        out_specs=[tok(wd) for wd, _ in widths],
        out_shape=[jax.ShapeDtypeStruct((bsz, slen, wd), dt) for wd, dt in widths],
        compiler_params=_params("parallel", "parallel"),
        name="in_proj",
    )(h, norm_g, mod, w, gq, gk, gmean)


def _attn_body(q_ref, k_ref, v_ref, o_ref, *, tq, tk, n_pairs):
    assert tk == LANES
    qi = pl.program_id(2)
    lane = lax.broadcasted_iota(jnp.int32, (1, LANES), 1)
    first = lane < HEAD_DIM
    row = lax.broadcasted_iota(jnp.int32, (tq, 2 * tk), 0)
    col = lax.broadcasted_iota(jnp.int32, (tq, 2 * tk), 1)
    col = jnp.where(col >= tk, col - tk, col)
    jj = lax.broadcasted_iota(jnp.int32, (2 * tk, 2 * tk), 0)
    ss = lax.broadcasted_iota(jnp.int32, (2 * tk, 2 * tk), 1)
    jj = jnp.where(jj >= tk, jj - tk, jj)
    suffix_total = jnp.where((ss >= tk) | (jj > ss), 1.0, 0.0).astype(BF16)

    def both_heads(x):
        zero = jnp.zeros_like(x)
        return jnp.concatenate([jnp.where(first, x, zero), jnp.where(first, zero, x)], axis=0)

    def block(k0, carry, diag_off):
        r0 = 0 if diag_off is None else diag_off
        new = []
        for p in range(n_pairs):
            acc_all, logsurv_all = carry[2 * p], carry[2 * p + 1]
            acc, logsurv = acc_all[r0:], logsurv_all[r0:]
            qp = q_ref[0, r0:, p * LANES:(p + 1) * LANES]
            k2 = both_heads(k_ref[0, pl.ds(k0, tk), p * LANES:(p + 1) * LANES])
            v2 = both_heads(v_ref[0, pl.ds(k0, tk), p * LANES:(p + 1) * LANES])
            z = _mm(qp, k2, _NT)
            nz = -z
            log_fail = jnp.minimum(nz, 0.0) - _log2(1.0 + jnp.exp2(jnp.minimum(z, nz)))
            if diag_off is not None:
                causal = (col + diag_off < row)[r0:]
                log_fail = jnp.where(causal, log_fail, 0.0)
            hi, lo = _split2(log_fail)
            st = [_mm(jnp.concatenate([hi[:, e * tk:(e + 1) * tk], lo[:, e * tk:(e + 1) * tk]],
                                      axis=1), suffix_total) for e in range(2)]
            suffix = jnp.concatenate([st[0][:, :tk], st[1][:, :tk]], axis=1)
            total = jnp.concatenate([st[0][:, tk:], st[1][:, tk:]], axis=1)
            w = jnp.exp2((z + log_fail) + logsurv + suffix)
            if diag_off is not None:
                w = jnp.where(causal, w, 0.0)
            acc = acc + _mm(w.astype(BF16), v2)
            logsurv = logsurv + total
            if r0:
                acc = jnp.concatenate([acc_all[:r0], acc], axis=0)
                logsurv = jnp.concatenate([logsurv_all[:r0], logsurv], axis=0)
            new += [acc, logsurv]
        return tuple(new)

    carry = (jnp.zeros((tq, LANES), F32), jnp.zeros((tq, 2 * tk), F32)) * n_pairs
    n_diag = tq // tk
    q0 = qi * tq
    for j in reversed(range(n_diag)):
        carry = block(pl.multiple_of(q0 + j * tk, tk), carry, j * tk)

    def body(i, carry):
        return block(pl.multiple_of(q0 - (i + 1) * tk, tk), carry, None)

    carry = lax.fori_loop(0, qi * n_diag, body, carry)
    for p in range(n_pairs):
        o_ref[0, :, p * LANES:(p + 1) * LANES] = carry[2 * p].astype(o_ref.dtype)


def _attn_call(q, k, v, *, tq, tk, n_pairs):
    bsz, slen, hd = q.shape
    width = n_pairs * LANES
    qblk = pl.BlockSpec((1, tq, width), lambda b, p, i: (b, i, p))
    kvblk = pl.BlockSpec((1, slen, width), lambda b, p, i: (b, 0, p))
    return pl.pallas_call(
        functools.partial(_attn_body, tq=tq, tk=tk, n_pairs=n_pairs),
        grid=(bsz, hd // width, slen // tq),
        in_specs=[qblk, kvblk, kvblk],
        out_specs=qblk,
        out_shape=jax.ShapeDtypeStruct(q.shape, BF16),
        compiler_params=_params("parallel", "parallel", "arbitrary"),
        name="stick_breaking_attn",
    )(q, k, v)


def _rwkv_prep_body(*refs, hd, has_vres):
    if has_vres:
        (p_ref, prev_ref, tail_ref, vf_ref, mu_ref, w0_ref, lora_ref, a0_ref, g2_ref, kk_ref,
         ka_ref, gsum_ref, v0_ref, v2_ref, r_o, lw_o, k_o, v_o, a_o, b_o, g_o) = refs
    else:
        (p_ref, prev_ref, mu_ref, w0_ref, lora_ref, a0_ref, g2_ref, kk_ref,
         ka_ref, gsum_ref, r_o, lw_o, k_o, v_o, a_o, b_o, g_o) = refs
    i = pl.program_id(1)
    p = p_ref[0]
    ts = p.shape[0]
    prev = prev_ref[0][7:8, :]
    prev = jnp.where(i == 0, jnp.zeros_like(prev), prev)
    shifted = pltpu.roll(p, 1, axis=0)
    rows = lax.broadcasted_iota(jnp.int32, (ts, 1), 0)
    shifted = jnp.where(rows == 0, prev, shifted)
    p = p + (shifted - p) * mu_ref[...]
    r = p[:, 0:hd]
    k = p[:, hd:2 * hd]
    v = p[:, 2 * hd:3 * hd]
    wa = p[:, 3 * hd:3 * hd + LANES]
    gl = p[:, 3 * hd + LANES:3 * hd + 2 * LANES]
    lane = lax.broadcasted_iota(jnp.int32, (1, LANES), 1)
    wa = jnp.where(lane < wa.shape[1] // 2, jnp.tanh(wa), wa)
    lora = _mm3(wa, lora_ref[...])
    w = -_softplus(-(w0_ref[...] + lora[:, 0:hd])) - 0.5
    lw_o[0] = -jnp.exp(w)
    a = jax.nn.sigmoid(a0_ref[...] + lora[:, hd:2 * hd])
    g_o[0] = _mm(jax.nn.sigmoid(gl).astype(BF16), g2_ref[...])
    if has_vres:
        mix = jax.nn.sigmoid(v0_ref[...] + _mm(tail_ref[0].astype(BF16), v2_ref[...]))
        v = v + (vf_ref[0] - v) * mix
    kk = k * kk_ref[...]
    ss = _mm_rhs_exact(kk * kk, gsum_ref[...])
    kk = kk / jnp.maximum(jnp.sqrt(ss), 1e-12)
    r_o[0] = r
    k_o[0] = k * (1.0 + (a - 1.0) * ka_ref[...])
    v_o[0] = v
    a_o[0] = -kk
    b_o[0] = kk * a


def _rwkv_prep_call(p, tail, v_first, mu, w0, lora_w, a0, g2, k_k, k_a, gsum, v0, v2, *, hd, ts):
    bsz, slen, rw = p.shape
    has_vres = tail is not None

    def tok(width):
        return pl.BlockSpec((1, ts, width), lambda b, i: (b, i, 0))

    prev = pl.BlockSpec((1, 8, rw), lambda b, i: (b, jnp.maximum(i * (ts // 8) - 1, 0), 0))
    args = [p, p]
    specs = [tok(rw), prev]
    if has_vres:
        args += [tail, v_first]
        specs += [tok(LANES), tok(hd)]
    consts = [mu, w0, lora_w, a0, g2, k_k, k_a, gsum]
    if has_vres:
        consts += [v0, v2]
    args += consts
    specs += [_resident(c.shape) for c in consts]
    return pl.pallas_call(
        functools.partial(_rwkv_prep_body, hd=hd, has_vres=has_vres),
        grid=(bsz, slen // ts),
        in_specs=specs,
        out_specs=[tok(hd)] * 7,
        out_shape=[jax.ShapeDtypeStruct((bsz, slen, hd), F32)] * 7,
        compiler_params=_params("parallel", "parallel"),
        name="rwkv_prep",
    )(*args)


def _rwkv_chunk_body(r_ref, lw_ref, k_ref, v_ref, a_ref, b_ref, g_ref, tri_ref, gmean_ref,
                     rk_ref, lnw_ref, lnb_ref, o_ref, st_ref, *, n_heads, passes):
    c = pl.program_id(1)

    @pl.when(c == 0)
    def _():
        st_ref[...] = jnp.zeros_like(st_ref)

    r, lw, k, v = r_ref[0], lw_ref[0], k_ref[0], v_ref[0]
    a, b = a_ref[0], b_ref[0]
    n_tok = r.shape[0]
    cum = _mm_lhs_exact(tri_ref[...], lw)
    cum_prev = cum - lw
    last = cum[n_tok - 1:n_tok, :]
    e_pos = jnp.exp(cum)
    e_neg = jnp.exp(-cum)
    e_end = jnp.exp(last - cum)
    at = a * jnp.exp(cum_prev)
    rt = r * e_pos
    bt = b * e_neg
    kt = k * e_neg
    bh = b * e_end
    kh = k * e_end
    p_end = jnp.exp(last)

    row = lax.broadcasted_iota(jnp.int32, (n_tok, n_tok), 0)
    col = lax.broadcasted_iota(jnp.int32, (n_tok, n_tok), 1)
    strict = row > col
    lower = row >= col
    erow = lax.broadcasted_iota(jnp.int32, (HEAD_DIM, HEAD_DIM), 0)
    ecol = lax.broadcasted_iota(jnp.int32, (HEAD_DIM, HEAD_DIM), 1)
    eye_c = jnp.where(row == col, 1.0, 0.0)
    n_levels = int(math.log2(n_tok))

    heads = range(n_heads)
    p_a, p_t, p_w, p_y, p_s = passes

    def per_head(t):
        return [t[:, h * HEAD_DIM:(h + 1) * HEAD_DIM] for h in heads]

    at_h, rt_h, bt_h, kt_h, bh_h, kh_h, v_h = map(per_head, (at, rt, bt, kt, bh, kh, v))
    lhs = [jnp.concatenate([at_h[h], rt_h[h]], axis=0) for h in heads]
    prod_b = [_mmp(lhs[h], bt_h[h], p_a, _NT) for h in heads]
    prod_k = [_mmp(lhs[h], kt_h[h], p_a, _NT) for h in heads]
    a_ab = [jnp.where(strict, prod_b[h][:n_tok], 0.0) for h in heads]
    a_rb = [jnp.where(lower, prod_b[h][n_tok:], 0.0) for h in heads]
    a_ak = [jnp.where(strict, prod_k[h][:n_tok], 0.0) for h in heads]
    a_rk = [jnp.where(lower, prod_k[h][n_tok:], 0.0) for h in heads]
    def sub_diag(level):
        size, half = 1 << level, 1 << (level - 1)
        inside = (row >> level) == (col >> level)
        return inside & ((row & (size - 1)) >= half) & ((col & (size - 1)) < half)

    lvl = sub_diag(1)
    tinv = [eye_c + jnp.where(lvl, a_ab[h], 0.0) for h in heads]
    for level in range(2, n_levels + 1):
        lvl = sub_diag(level)
        right = [_mmp(jnp.where(lvl, a_ab[h], 0.0), tinv[h], p_t) for h in heads]
        tinv = [tinv[h] + _mmp(tinv[h], right[h], p_t) for h in heads]
    akv = [_mmp(a_ak[h], v_h[h], p_w) for h in heads]
    w_h = [_mmp(tinv[h], at_h[h], p_w) for h in heads]
    x_h = [_mmp(tinv[h], akv[h], p_w) for h in heads]
    q_h = [rt_h[h] + _mmp(a_rb[h], w_h[h], p_y) for h in heads]
    y_intra = [_mmp(a_rb[h], x_h[h], p_y) + _mmp(a_rk[h], v_h[h], p_y) for h in heads]
    st = [st_ref[h] for h in heads]
    ys = [_mmp(q_h[h], st[h], p_y) + y_intra[h] for h in heads]
    bh_t = [bh_h[h].T for h in heads]
    kh_t = [kh_h[h].T for h in heads]
    p_end_h = per_head(p_end)
    m_t = [jnp.where(erow == ecol, p_end_h[h], 0.0) + _mmp(bh_t[h], w_h[h], p_s) for h in heads]
    n_t = [_mmp(bh_t[h], x_h[h], p_s) + _mmp(kh_t[h], v_h[h], p_s) for h in heads]
    for h in heads:
        st_ref[h] = _mmp(m_t[h], st[h], p_s) + n_t[h]
    y = jnp.concatenate(ys, axis=1)

    gmean = gmean_ref[...]
    mean = _mm_rhs_exact(y, gmean)
    dev = y - mean
    var = _mm_rhs_exact(dev * dev, gmean)
    yn = dev * lax.rsqrt(var + GN_EPS) * lnw_ref[...] + lnb_ref[...]
    bonus = _mm_rhs_exact(r * k * rk_ref[...], gmean) * float(HEAD_DIM) * v
    o_ref[0] = ((yn + bonus) * g_ref[0]).astype(o_ref.dtype)


def _rwkv_chunk_call(r, lw, k, v, a, b, g, tri, gmean, rk, lnw, lnb, passes=(3, 3, 3, 3, 3)):
    bsz, slen, hd = r.shape
    chunk = tri.shape[0]
    n_heads = hd // HEAD_DIM
    tok = pl.BlockSpec((1, chunk, hd), lambda bi, ci: (bi, ci, 0))
    consts = [tri, gmean, rk, lnw, lnb]
    return pl.pallas_call(
        functools.partial(_rwkv_chunk_body, n_heads=n_heads, passes=passes),
        grid=(bsz, slen // chunk),
        in_specs=[tok] * 7 + [_resident(c.shape) for c in consts],
        out_specs=tok,
        out_shape=jax.ShapeDtypeStruct((bsz, slen, hd), BF16),
        scratch_shapes=[pltpu.VMEM((n_heads, HEAD_DIM, HEAD_DIM), F32)],
        compiler_params=_params("parallel", "arbitrary"),
        name="rwkv_chunk_scan",
    )(r, lw, k, v, a, b, g, *consts)


def _merge_body(h_ref, ya_ref, yb_ref, gate_ref, mod_ref, wa_ref, wb_ref, wo_ref, o_ref):
    d = h_ref.shape[-1]
    ua = _mm(ya_ref[0], wa_ref[...])
    ub = _mm(yb_ref[0], wb_ref[...])
    merged = gate_ref[0, :, 0:d].astype(F32) * ua + gate_ref[0, :, d:2 * d].astype(F32) * ub
    out = _mm(merged.astype(BF16), wo_ref[...])
    o_ref[0] = h_ref[0] + mod_ref[0, 5:6, :] * out


def _merge_call(h, ya, yb, gates, mod, wa, wb, wo, *, tm):
    bsz, slen, d = h.shape

    def tok(width):
        return pl.BlockSpec((1, tm, width), lambda b, i: (b, i, 0))

    return pl.pallas_call(
        _merge_body,
        grid=(bsz, slen // tm),
        in_specs=[
            tok(d), tok(ya.shape[-1]), tok(yb.shape[-1]), tok(2 * d),
            pl.BlockSpec((1, N_MOD, d), lambda b, i: (b, 0, 0)),
            _resident(wa.shape), _resident(wb.shape), _resident(wo.shape),
        ],
        out_specs=tok(d),
        out_shape=jax.ShapeDtypeStruct(h.shape, F32),
        compiler_params=_params("parallel", "parallel"),
        name="merge_out",
    )(h, ya, yb, gates, mod, wa, wb, wo)


def _block_diag_const(width, group, value, dtype):
    idx = jnp.arange(width) // group
    return jnp.where(idx[:, None] == idx[None, :], value, 0.0).astype(dtype)


def kernel(x, c, mod_w, mod_b, norm_g, ffn1_wg, ffn1_wu, ffn1_wd, w_in_first, w_in_rest, shift_mu, decay_w0, decay_w2, aaa_a0, aaa_a2, gate_g2, k_k, k_a, r_k, lnx_w, lnx_b, vres_v0, vres_v2, q_norm_g, k_norm_g, w_up_a, w_up_b, w_out, ffn2_wg, ffn2_wu, ffn2_wd):
    bsz, slen, d = x.shape
    n_layers = mod_w.shape[0]
    hd = decay_w0.shape[-1]
    n_heads = hd // HEAD_DIM
    decay_rank, aaa_rank = decay_w2.shape[1], aaa_a2.shape[1]
    gate_rank, vres_rank = gate_g2.shape[1], vres_v2.shape[1]
    assert decay_rank + aaa_rank == LANES and gate_rank == LANES and vres_rank <= LANES
    assert w_up_b.shape[1] == hd and hd % LANES == 0
    rw = 3 * hd + decay_rank + aaa_rank + gate_rank
    cols_first = rw + 3 * hd + 2 * d
    assert w_in_first.shape[1] == cols_first

    tm = min(512, slen)
    tq_attn = min(512, slen)
    tk_attn = LANES
    chunk = min(RWKV_CHUNK, slen)

    gmean = _block_diag_const(hd, HEAD_DIM, 1.0 / HEAD_DIM, BF16)
    gsum = _block_diag_const(hd, HEAD_DIM, 1.0, BF16)
    tri = jnp.tril(jnp.ones((chunk, chunk), F32)).astype(BF16)

    mod_all = _mod_call(c, mod_w, mod_b).reshape(n_layers, bsz, N_MOD, d)

    h = x
    v_first = None
    for l in range(n_layers):
        mod = mod_all[l]
        h = _ffn_call(h, norm_g[l], mod, ffn1_wg[l].astype(BF16), ffn1_wu[l].astype(BF16),
                      ffn1_wd[l].astype(BF16), sub=0, tm=tm, tf=256)

        has_vres = l > 0
        if has_vres:
            w_in = jnp.pad(w_in_rest[l - 1], ((0, 0), (0, LANES - vres_rank)))
        else:
            w_in = w_in_first
        gq = jnp.tile(q_norm_g[l], n_heads)[None, :]
        gk = jnp.tile(k_norm_g[l], n_heads)[None, :]
        proj = _inproj_call(h, norm_g[l], mod, w_in.astype(BF16), gq, gk, gmean,
                            rw=rw, hd=hd, has_vres=has_vres, tm=tm)
        p_rwkv, q, k, v, gates = proj[:5]
        tail = proj[5] if has_vres else None

        lora_w = jnp.zeros((LANES, 2 * hd), F32)
        lora_w = lora_w.at[:decay_rank, :hd].set(decay_w2[l]).at[decay_rank:, hd:].set(aaa_a2[l])
        if has_vres:
            v0 = vres_v0[l - 1][None, :]
            v2 = jnp.pad(vres_v2[l - 1], ((0, LANES - vres_rank), (0, 0))).astype(BF16)
        else:
            v0 = v2 = None
        r_, lw_, k_, v_, a_, b_, g_ = _rwkv_prep_call(
            p_rwkv, tail, v_first, shift_mu[l][None, :], decay_w0[l][None, :], lora_w,
            aaa_a0[l][None, :], gate_g2[l].astype(BF16), k_k[l][None, :], k_a[l][None, :], gsum,
            v0, v2, hd=hd, ts=tm)
        if not has_vres:
            v_first = v_
        y_a = _rwkv_chunk_call(r_, lw_, k_, v_, a_, b_, g_, tri, gmean, r_k[l].reshape(1, hd),
                               lnx_w[l][None, :], lnx_b[l][None, :], passes=RWKV_PASSES)

        y_b = _attn_call(q, k, v, tq=tq_attn, tk=tk_attn, n_pairs=hd // LANES)

        h = _merge_call(h, y_a, y_b, gates, mod, w_up_a[l].astype(BF16), w_up_b[l].astype(BF16),
                        w_out[l].astype(BF16), tm=tm)

        h = _ffn_call(h, norm_g[l], mod, ffn2_wg[l].astype(BF16), ffn2_wu[l].astype(BF16),
                      ffn2_wd[l].astype(BF16), sub=2, tm=tm, tf=256)
    return h
```

```python
import functools
import math

import jax
import jax.numpy as jnp
from jax import lax
from jax.experimental import pallas as pl
from jax.experimental.pallas import tpu as pltpu

F32 = jnp.float32
BF16 = jnp.bfloat16

HEAD_DIM = 64
RMS_EPS = 1e-6
GN_EPS = 64e-5
N_MOD = 9
LANES = 128
VMEM_LIMIT = 56 * 1024 * 1024
RWKV_CHUNK = 64
RWKV_SUBCHUNKS = 4

_NN = (((1,), (0,)), ((), ()))
_NT = (((1,), (1,)), ((), ()))


def _mm(a, b, dims=_NN):
    return lax.dot_general(a, b, dims, preferred_element_type=F32)


def _split2(x):
    hi = x.astype(BF16)
    lo = (x - hi.astype(F32)).astype(BF16)
    return hi, lo


def _split3(x):
    hi = x.astype(BF16)
    r1 = x - hi.astype(F32)
    mid = r1.astype(BF16)
    lo = (r1 - mid.astype(F32)).astype(BF16)
    return hi, mid, lo


def _mm3(a, b, dims=_NN):
    ah, al = _split2(a)
    bh, bl = _split2(b)
    return _mm(ah, bh, dims) + (_mm(ah, bl, dims) + _mm(al, bh, dims))


def _mmp(a, b, passes, dims=_NN):
    if passes == 1:
        return _mm(a.astype(BF16), b.astype(BF16), dims)
    return _mm3(a, b, dims)


def _mm_lhs_exact(a_bf16, b):
    hi, mid, lo = _split3(b)
    return _mm(a_bf16, hi) + (_mm(a_bf16, mid) + _mm(a_bf16, lo))


def _mm_rhs_exact(a, b_bf16):
    hi, mid, lo = _split3(a)
    return _mm(hi, b_bf16) + (_mm(mid, b_bf16) + _mm(lo, b_bf16))


def _params(*sem):
    return pltpu.CompilerParams(dimension_semantics=sem, vmem_limit_bytes=VMEM_LIMIT)


def _resident(shape):
    zeros = (0,) * len(shape)
    return pl.BlockSpec(shape, lambda *_: zeros, pipeline_mode=pl.Buffered(1))


def _norm_mod(x, g, sh, sc):
    ms = jnp.mean(x * x, axis=-1, keepdims=True)
    y = x * lax.rsqrt(ms + RMS_EPS) * g
    return y * (1.0 + sc) + sh


LOG2E = 1.4426950408889634


def _log2(x):
    return jnp.log(x) * LOG2E


def _softplus(x):
    return jnp.maximum(x, 0.0) + jnp.log(1.0 + jnp.exp(-jnp.abs(x)))


def _mod_body(c_ref, w_ref, b_ref, o_ref):
    c = c_ref[...]
    ca = (c * jax.nn.sigmoid(c)).astype(BF16)
    o_ref[0] = _mm(ca, w_ref[0].astype(BF16)) + b_ref[0]


def _mod_call(c, mod_w, mod_b):
    n_layers, d, nd = mod_w.shape
    bsz = c.shape[0]
    tn = d
    return pl.pallas_call(
        _mod_body,
        grid=(n_layers, nd // tn),
        in_specs=[
            pl.BlockSpec((bsz, d), lambda l, j: (0, 0)),
            pl.BlockSpec((1, d, tn), lambda l, j: (l, 0, j)),
            pl.BlockSpec((1, 1, tn), lambda l, j: (l, 0, j)),
        ],
        out_specs=pl.BlockSpec((1, bsz, tn), lambda l, j: (l, 0, j)),
        out_shape=jax.ShapeDtypeStruct((n_layers, bsz, nd), F32),
        compiler_params=_params("parallel", "parallel"),
        name="adaln_mod",
    )(c, mod_w, mod_b.reshape(n_layers, 1, nd))


def _ffn_body(h_ref, g_ref, mod_ref, wg_ref, wu_ref, wd_ref, o_ref, *, sub, tf):
    x = h_ref[0]
    g = g_ref[sub:sub + 1, :]
    sh = mod_ref[0, 3 * sub:3 * sub + 1, :]
    sc = mod_ref[0, 3 * sub + 1:3 * sub + 2, :]
    gt = mod_ref[0, 3 * sub + 2:3 * sub + 3, :]
    n = _norm_mod(x, g, sh, sc).astype(BF16)
    d_ff = wg_ref.shape[1]
    acc = jnp.zeros_like(x)
    for j in range(d_ff // tf):
        gj = _mm(n, wg_ref[:, j * tf:(j + 1) * tf])
        uj = _mm(n, wu_ref[:, j * tf:(j + 1) * tf])
        a = (gj * jax.nn.sigmoid(gj) * uj).astype(BF16)
        acc = acc + _mm(a, wd_ref[j * tf:(j + 1) * tf, :])
    o_ref[0] = x + (0.5 * gt) * acc


def _ffn_call(h, norm_g, mod, wg, wu, wd, *, sub, tm, tf):
    bsz, slen, d = h.shape
    d_ff = wg.shape[1]
    tok = pl.BlockSpec((1, tm, d), lambda b, i: (b, i, 0))
    return pl.pallas_call(
        functools.partial(_ffn_body, sub=sub, tf=tf),
        grid=(bsz, slen // tm),
        in_specs=[
            tok,
            _resident(norm_g.shape),
            pl.BlockSpec((1, N_MOD, d), lambda b, i: (b, 0, 0)),
            _resident((d, d_ff)), _resident((d, d_ff)), _resident((d_ff, d)),
        ],
        out_specs=tok,
        out_shape=jax.ShapeDtypeStruct(h.shape, F32),
        compiler_params=_params("parallel", "parallel"),
        name="ffn",
    )(h, norm_g, mod, wg, wu, wd)


def _head_rms(t, g, gmean):
    hi, lo = _split2(t * t)
    ms = _mm(hi, gmean) + _mm(lo, gmean)
    return t * lax.rsqrt(ms + RMS_EPS) * g


def _inproj_body(h_ref, g_ref, mod_ref, w_ref, gq_ref, gk_ref, gmean_ref, *outs,
                 rw, hd, d, has_vres):
    p_ref, q_ref, k_ref, v_ref, gate_ref = outs[:5]
    x = h_ref[0]
    n = _norm_mod(x, g_ref[1:2, :], mod_ref[0, 3:4, :], mod_ref[0, 4:5, :]).astype(BF16)
    gmean = gmean_ref[...]
    p_ref[0] = _mm(n, w_ref[:, 0:rw])
    q = _mm(n, w_ref[:, rw:rw + hd])
    q_ref[0] = (_head_rms(q, gq_ref[...], gmean) * (LOG2E / math.sqrt(HEAD_DIM))).astype(BF16)
    k = _mm(n, w_ref[:, rw + hd:rw + 2 * hd])
    k_ref[0] = _head_rms(k, gk_ref[...], gmean).astype(BF16)
    v_ref[0] = _mm(n, w_ref[:, rw + 2 * hd:rw + 3 * hd]).astype(BF16)
    og = rw + 3 * hd
    gate_ref[0] = jax.nn.sigmoid(_mm(n, w_ref[:, og:og + 2 * d])).astype(BF16)
    if has_vres:
        outs[5][0] = _mm(n, w_ref[:, og + 2 * d:og + 2 * d + LANES])


def _inproj_call(h, norm_g, mod, w, gq, gk, gmean, *, rw, hd, has_vres, tm):
    bsz, slen, d = h.shape

    def tok(width):
        return pl.BlockSpec((1, tm, width), lambda b, i: (b, i, 0))

    widths = [(rw, F32), (hd, BF16), (hd, BF16), (hd, BF16), (2 * d, BF16)]
    if has_vres:
        widths.append((LANES, F32))
    return pl.pallas_call(
        functools.partial(_inproj_body, rw=rw, hd=hd, d=d, has_vres=has_vres),
        grid=(bsz, slen // tm),
        in_specs=[
            tok(d),
            _resident(norm_g.shape),
            pl.BlockSpec((1, N_MOD, d), lambda b, i: (b, 0, 0)),
            _resident(w.shape), _resident(gq.shape), _resident(gk.shape), _resident(gmean.shape),
        ],
        out_specs=[tok(wd) for wd, _ in widths],
        out_shape=[jax.ShapeDtypeStruct((bsz, slen, wd), dt) for wd, dt in widths],
        compiler_params=_params("parallel", "parallel"),
        name="in_proj",
    )(h, norm_g, mod, w, gq, gk, gmean)


def _attn_body(q_ref, k_ref, v_ref, o_ref, *, tq, tk, n_pairs):
    assert tk == LANES
    qi = pl.program_id(2)
    lane = lax.broadcasted_iota(jnp.int32, (1, LANES), 1)
    first = lane < HEAD_DIM
    row = lax.broadcasted_iota(jnp.int32, (tq, 2 * tk), 0)
    col = lax.broadcasted_iota(jnp.int32, (tq, 2 * tk), 1)
    col = jnp.where(col >= tk, col - tk, col)
    jj = lax.broadcasted_iota(jnp.int32, (2 * tk, 2 * tk), 0)
    ss = lax.broadcasted_iota(jnp.int32, (2 * tk, 2 * tk), 1)
    jj = jnp.where(jj >= tk, jj - tk, jj)
    suffix_total = jnp.where((ss >= tk) | (jj > ss), 1.0, 0.0).astype(BF16)

    def both_heads(x):
        zero = jnp.zeros_like(x)
        return jnp.concatenate([jnp.where(first, x, zero), jnp.where(first, zero, x)], axis=0)

    def block(k0, carry, diag_off):
        r0 = 0 if diag_off is None else diag_off
        new = []
        for p in range(n_pairs):
            acc_all, logsurv_all = carry[2 * p], carry[2 * p + 1]
            acc, logsurv = acc_all[r0:], logsurv_all[r0:]
            qp = q_ref[0, r0:, p * LANES:(p + 1) * LANES]
            k2 = both_heads(k_ref[0, pl.ds(k0, tk), p * LANES:(p + 1) * LANES])
            v2 = both_heads(v_ref[0, pl.ds(k0, tk), p * LANES:(p + 1) * LANES])
            z = _mm(qp, k2, _NT)
            nz = -z
            log_fail = jnp.minimum(nz, 0.0) - _log2(1.0 + jnp.exp2(jnp.minimum(z, nz)))
            if diag_off is not None:
                causal = (col + diag_off < row)[r0:]
                log_fail = jnp.where(causal, log_fail, 0.0)
            hi, lo = _split2(log_fail)
            st = [_mm(jnp.concatenate([hi[:, e * tk:(e + 1) * tk], lo[:, e * tk:(e + 1) * tk]],
                                      axis=1), suffix_total) for e in range(2)]
            suffix = jnp.concatenate([st[0][:, :tk], st[1][:, :tk]], axis=1)
            total = jnp.concatenate([st[0][:, tk:], st[1][:, tk:]], axis=1)
            w = jnp.exp2((z + log_fail) + logsurv + suffix)
            if diag_off is not None:
                w = jnp.where(causal, w, 0.0)
            acc = acc + _mm(w.astype(BF16), v2)
            logsurv = logsurv + total
            if r0:
                acc = jnp.concatenate([acc_all[:r0], acc], axis=0)
                logsurv = jnp.concatenate([logsurv_all[:r0], logsurv], axis=0)
            new += [acc, logsurv]
        return tuple(new)

    carry = (jnp.zeros((tq, LANES), F32), jnp.zeros((tq, 2 * tk), F32)) * n_pairs
    n_diag = tq // tk
    q0 = qi * tq
    for j in reversed(range(n_diag)):
        carry = block(pl.multiple_of(q0 + j * tk, tk), carry, j * tk)

    def body(i, carry):
        return block(pl.multiple_of(q0 - (i + 1) * tk, tk), carry, None)

    carry = lax.fori_loop(0, qi * n_diag, body, carry)
    for p in range(n_pairs):
        o_ref[0, :, p * LANES:(p + 1) * LANES] = carry[2 * p].astype(o_ref.dtype)


def _attn_call(q, k, v, *, tq, tk, n_pairs):
    bsz, slen, hd = q.shape
    width = n_pairs * LANES
    qblk = pl.BlockSpec((1, tq, width), lambda b, p, i: (b, i, p))
    kvblk = pl.BlockSpec((1, slen, width), lambda b, p, i: (b, 0, p))
    return pl.pallas_call(
        functools.partial(_attn_body, tq=tq, tk=tk, n_pairs=n_pairs),
        grid=(bsz, hd // width, slen // tq),
        in_specs=[qblk, kvblk, kvblk],
        out_specs=qblk,
        out_shape=jax.ShapeDtypeStruct(q.shape, BF16),
        compiler_params=_params("parallel", "parallel", "arbitrary"),
        name="stick_breaking_attn",
    )(q, k, v)


def _rwkv_prep_body(*refs, hd, has_vres):
    if has_vres:
        (p_ref, prev_ref, tail_ref, vf_ref, mu_ref, w0_ref, lora_ref, a0_ref, g2_ref, kk_ref,
         ka_ref, gsum_ref, v0_ref, v2_ref, r_o, lw_o, k_o, v_o, a_o, b_o, g_o) = refs
    else:
        (p_ref, prev_ref, mu_ref, w0_ref, lora_ref, a0_ref, g2_ref, kk_ref,
         ka_ref, gsum_ref, r_o, lw_o, k_o, v_o, a_o, b_o, g_o) = refs
    i = pl.program_id(1)
    p = p_ref[0]
    ts = p.shape[0]
    prev = prev_ref[0][7:8, :]
    prev = jnp.where(i == 0, jnp.zeros_like(prev), prev)
    shifted = pltpu.roll(p, 1, axis=0)
    rows = lax.broadcasted_iota(jnp.int32, (ts, 1), 0)
    shifted = jnp.where(rows == 0, prev, shifted)
    p = p + (shifted - p) * mu_ref[...]
    r = p[:, 0:hd]
    k = p[:, hd:2 * hd]
    v = p[:, 2 * hd:3 * hd]
    wa = p[:, 3 * hd:3 * hd + LANES]
    gl = p[:, 3 * hd + LANES:3 * hd + 2 * LANES]
    lane = lax.broadcasted_iota(jnp.int32, (1, LANES), 1)
    wa = jnp.where(lane < wa.shape[1] // 2, jnp.tanh(wa), wa)
    lora = _mm3(wa, lora_ref[...])
    w = -_softplus(-(w0_ref[...] + lora[:, 0:hd])) - 0.5
    lw_o[0] = -jnp.exp(w)
    a = jax.nn.sigmoid(a0_ref[...] + lora[:, hd:2 * hd])
    g_o[0] = _mm(jax.nn.sigmoid(gl).astype(BF16), g2_ref[...])
    if has_vres:
        mix = jax.nn.sigmoid(v0_ref[...] + _mm(tail_ref[0].astype(BF16), v2_ref[...]))
        v = v + (vf_ref[0] - v) * mix
    kk = k * kk_ref[...]
    ss = _mm_rhs_exact(kk * kk, gsum_ref[...])
    kk = kk / jnp.maximum(jnp.sqrt(ss), 1e-12)
    r_o[0] = r
    k_o[0] = k * (1.0 + (a - 1.0) * ka_ref[...])
    v_o[0] = v
    a_o[0] = -kk
    b_o[0] = kk * a


def _rwkv_prep_call(p, tail, v_first, mu, w0, lora_w, a0, g2, k_k, k_a, gsum, v0, v2, *, hd, ts):
    bsz, slen, rw = p.shape
    has_vres = tail is not None

    def tok(width):
        return pl.BlockSpec((1, ts, width), lambda b, i: (b, i, 0))

    prev = pl.BlockSpec((1, 8, rw), lambda b, i: (b, jnp.maximum(i * (ts // 8) - 1, 0), 0))
    args = [p, p]
    specs = [tok(rw), prev]
    if has_vres:
        args += [tail, v_first]
        specs += [tok(LANES), tok(hd)]
    consts = [mu, w0, lora_w, a0, g2, k_k, k_a, gsum]
    if has_vres:
        consts += [v0, v2]
    args += consts
    specs += [_resident(c.shape) for c in consts]
    return pl.pallas_call(
        functools.partial(_rwkv_prep_body, hd=hd, has_vres=has_vres),
        grid=(bsz, slen // ts),
        in_specs=specs,
        out_specs=[tok(hd)] * 7,
        out_shape=[jax.ShapeDtypeStruct((bsz, slen, hd), F32)] * 7,
        compiler_params=_params("parallel", "parallel"),
        name="rwkv_prep",
    )(*args)


def _rwkv_chunk_body(r_ref, lw_ref, k_ref, v_ref, a_ref, b_ref, g_ref, tri_ref, gmean_ref,
                     rk_ref, lnw_ref, lnb_ref, o_ref, st_ref, *, n_sub):
    c = pl.program_id(1)

    @pl.when(c == 0)
    def _():
        st_ref[...] = jnp.zeros_like(st_ref)

    r, lw, k, v = r_ref[0], lw_ref[0], k_ref[0], v_ref[0]
    a, b = a_ref[0], b_ref[0]
    ch = RWKV_CHUNK
    n_pairs = r.shape[1] // LANES
    cum = _mm_lhs_exact(tri_ref[...], lw)
    cum_prev = cum - lw
    lasts = [cum[(s + 1) * ch - 1:(s + 1) * ch, :] for s in range(n_sub)]
    last = jnp.concatenate([jnp.broadcast_to(t, (ch, t.shape[1])) for t in lasts], axis=0)
    e_pos = jnp.exp(cum)
    e_neg = jnp.exp(-cum)
    e_end = jnp.exp(last - cum)
    at = a * jnp.exp(cum_prev)
    rt = r * e_pos
    bt = b * e_neg
    kt = k * e_neg
    bh = b * e_end
    kh = k * e_end
    p_end = [jnp.exp(t) for t in lasts]

    lane = lax.broadcasted_iota(jnp.int32, (1, LANES), 1)
    first = lane < HEAD_DIM

    def stacked(x, s, p):
        x = x[s * ch:(s + 1) * ch, p * LANES:(p + 1) * LANES]
        zero = jnp.zeros_like(x)
        return jnp.concatenate([jnp.where(first, x, zero), jnp.where(first, zero, x)], axis=0)

    row = lax.broadcasted_iota(jnp.int32, (2 * ch, 2 * ch), 0)
    col = lax.broadcasted_iota(jnp.int32, (2 * ch, 2 * ch), 1)
    strict = (row & (ch - 1)) > (col & (ch - 1))
    row2 = lax.broadcasted_iota(jnp.int32, (2 * ch, 4 * ch), 0)
    col2 = lax.broadcasted_iota(jnp.int32, (2 * ch, 4 * ch), 1)
    lower2 = (row2 & (ch - 1)) >= (col2 & (ch - 1))
    diag = row == col
    eye = jnp.where(diag, 1.0, 0.0)
    n_levels = int(math.log2(ch))

    def sub_diag(level):
        size, half = 1 << level, 1 << (level - 1)
        inside = (row >> level) == (col >> level)
        return inside & ((row & (size - 1)) >= half) & ((col & (size - 1)) < half)

    units = [(s, p) for s in range(n_sub) for p in range(n_pairs)]
    at_s, rt_s, bt_s, kt_s, bh_s, kh_s, v_s = (
        [stacked(t, s, p) for s, p in units] for t in (at, rt, bt, kt, bh, kh, v))
    idx = range(len(units))
    prod = [_mm(jnp.concatenate([at_s[u], rt_s[u]], axis=0).astype(BF16),
                jnp.concatenate([bt_s[u], kt_s[u]], axis=0).astype(BF16), _NT) for u in idx]
    a_ab = [jnp.where(strict, prod[u][:2 * ch, :2 * ch], 0.0) for u in idx]
    a_ak = [jnp.where(strict, prod[u][:2 * ch, 2 * ch:], 0.0).astype(BF16) for u in idx]
    a_rbk = [jnp.where(lower2, prod[u][2 * ch:, :], 0.0).astype(BF16) for u in idx]
    lvl = sub_diag(1)
    tinv = [eye + jnp.where(lvl, a_ab[u], 0.0) for u in idx]
    for level in range(2, n_levels + 1):
        lvl = sub_diag(level)
        tinv_b = [tinv[u].astype(BF16) for u in idx]
        right = [_mm(jnp.where(lvl, a_ab[u], 0.0).astype(BF16), tinv_b[u]) for u in idx]
        tinv = [tinv[u] + _mm(tinv_b[u], right[u].astype(BF16)) for u in idx]
    v_b = [v_s[u].astype(BF16) for u in idx]
    akv = [_mm(a_ak[u], v_b[u]) for u in idx]
    wx = [_mm(tinv[u].astype(BF16),
              jnp.concatenate([at_s[u], akv[u]], axis=1).astype(BF16)) for u in idx]
    zero_b = jnp.zeros((2 * ch, LANES), BF16)
    wxv = [jnp.concatenate([wx[u].astype(BF16), jnp.concatenate([zero_b, v_b[u]], axis=1)], axis=0)
           for u in idx]
    qy = [_mm(a_rbk[u], wxv[u]) for u in idx]
    mn = [_mm(jnp.concatenate([bh_s[u], kh_s[u]], axis=0).T.astype(BF16), wxv[u]) for u in idx]
    ys = []
    for s in range(n_sub):
        rows = []
        for p in range(n_pairs):
            u = s * n_pairs + p
            st = st_ref[p]
            y_st = _mm((rt_s[u] + qy[u][:, :LANES]).astype(BF16), st.astype(BF16)) + qy[u][:, LANES:]
            rows.append(y_st[:ch] + y_st[ch:])
            m_t = jnp.where(diag, p_end[s][:, p * LANES:(p + 1) * LANES], 0.0) + mn[u][:, :LANES]
            st_ref[p] = _mm3(m_t, st) + mn[u][:, LANES:]
        ys.append(jnp.concatenate(rows, axis=1))
    y = jnp.concatenate(ys, axis=0)

    gmean = gmean_ref[...]
    y_hi, y_lo = _split2(y)
    mean = _mm(y_hi, gmean) + _mm(y_lo, gmean)
    dev = y - mean
    var = _mm((dev * dev).astype(BF16), gmean)
    yn = dev * lax.rsqrt(var + GN_EPS) * lnw_ref[...] + lnb_ref[...]
    bonus = _mm((r * k * rk_ref[...]).astype(BF16), gmean) * float(HEAD_DIM) * v
    o_ref[0] = ((yn + bonus) * g_ref[0]).astype(o_ref.dtype)


def _rwkv_chunk_call(r, lw, k, v, a, b, g, gmean, rk, lnw, lnb, *, n_sub):
    bsz, slen, hd = r.shape
    rows = n_sub * RWKV_CHUNK
    idx = jnp.arange(rows)
    same_chunk = idx[:, None] // RWKV_CHUNK == idx[None, :] // RWKV_CHUNK
    tri = (same_chunk & (idx[:, None] >= idx[None, :])).astype(BF16)
    tok = pl.BlockSpec((1, rows, hd), lambda bi, ci: (bi, ci, 0))
    consts = [tri, gmean, rk, lnw, lnb]
    return pl.pallas_call(
        functools.partial(_rwkv_chunk_body, n_sub=n_sub),
        grid=(bsz, slen // rows),
        in_specs=[tok] * 7 + [_resident(c.shape) for c in consts],
        out_specs=tok,
        out_shape=jax.ShapeDtypeStruct((bsz, slen, hd), BF16),
        scratch_shapes=[pltpu.VMEM((hd // LANES, LANES, LANES), F32)],
        compiler_params=_params("parallel", "arbitrary"),
        name="rwkv_chunk_scan",
    )(r, lw, k, v, a, b, g, *consts)


def _merge_body(h_ref, ya_ref, yb_ref, gate_ref, mod_ref, wa_ref, wb_ref, wo_ref, o_ref):
    d = h_ref.shape[-1]
    ua = _mm(ya_ref[0], wa_ref[...])
    ub = _mm(yb_ref[0], wb_ref[...])
    merged = gate_ref[0, :, 0:d].astype(F32) * ua + gate_ref[0, :, d:2 * d].astype(F32) * ub
    out = _mm(merged.astype(BF16), wo_ref[...])
    o_ref[0] = h_ref[0] + mod_ref[0, 5:6, :] * out


def _merge_call(h, ya, yb, gates, mod, wa, wb, wo, *, tm):
    bsz, slen, d = h.shape

    def tok(width):
        return pl.BlockSpec((1, tm, width), lambda b, i: (b, i, 0))

    return pl.pallas_call(
        _merge_body,
        grid=(bsz, slen // tm),
        in_specs=[
            tok(d), tok(ya.shape[-1]), tok(yb.shape[-1]), tok(2 * d),
            pl.BlockSpec((1, N_MOD, d), lambda b, i: (b, 0, 0)),
            _resident(wa.shape), _resident(wb.shape), _resident(wo.shape),
        ],
        out_specs=tok(d),
        out_shape=jax.ShapeDtypeStruct(h.shape, F32),
        compiler_params=_params("parallel", "parallel"),
        name="merge_out",
    )(h, ya, yb, gates, mod, wa, wb, wo)


def _block_diag_const(width, group, value, dtype):
    idx = jnp.arange(width) // group
    return jnp.where(idx[:, None] == idx[None, :], value, 0.0).astype(dtype)


def kernel(x, c, mod_w, mod_b, norm_g, ffn1_wg, ffn1_wu, ffn1_wd, w_in_first, w_in_rest, shift_mu, decay_w0, decay_w2, aaa_a0, aaa_a2, gate_g2, k_k, k_a, r_k, lnx_w, lnx_b, vres_v0, vres_v2, q_norm_g, k_norm_g, w_up_a, w_up_b, w_out, ffn2_wg, ffn2_wu, ffn2_wd):
    bsz, slen, d = x.shape
    n_layers = mod_w.shape[0]
    hd = decay_w0.shape[-1]
    n_heads = hd // HEAD_DIM
    decay_rank, aaa_rank = decay_w2.shape[1], aaa_a2.shape[1]
    gate_rank, vres_rank = gate_g2.shape[1], vres_v2.shape[1]
    assert decay_rank + aaa_rank == LANES and gate_rank == LANES and vres_rank <= LANES
    assert w_up_b.shape[1] == hd and hd % LANES == 0
    rw = 3 * hd + decay_rank + aaa_rank + gate_rank
    cols_first = rw + 3 * hd + 2 * d
    assert w_in_first.shape[1] == cols_first

    tm = min(512, slen)
    tq_attn = min(512, slen)
    tk_attn = LANES

    gmean = _block_diag_const(hd, HEAD_DIM, 1.0 / HEAD_DIM, BF16)
    gsum = _block_diag_const(hd, HEAD_DIM, 1.0, BF16)

    mod_all = _mod_call(c, mod_w, mod_b).reshape(n_layers, bsz, N_MOD, d)

    h = x
    v_first = None
    for l in range(n_layers):
        mod = mod_all[l]
        h = _ffn_call(h, norm_g[l], mod, ffn1_wg[l].astype(BF16), ffn1_wu[l].astype(BF16),
                      ffn1_wd[l].astype(BF16), sub=0, tm=tm, tf=256)

        has_vres = l > 0
        if has_vres:
            w_in = jnp.pad(w_in_rest[l - 1], ((0, 0), (0, LANES - vres_rank)))
        else:
            w_in = w_in_first
        gq = jnp.tile(q_norm_g[l], n_heads)[None, :]
        gk = jnp.tile(k_norm_g[l], n_heads)[None, :]
        proj = _inproj_call(h, norm_g[l], mod, w_in.astype(BF16), gq, gk, gmean,
                            rw=rw, hd=hd, has_vres=has_vres, tm=tm)
        p_rwkv, q, k, v, gates = proj[:5]
        tail = proj[5] if has_vres else None

        lora_w = jnp.zeros((LANES, 2 * hd), F32)
        lora_w = lora_w.at[:decay_rank, :hd].set(decay_w2[l]).at[decay_rank:, hd:].set(aaa_a2[l])
        if has_vres:
            v0 = vres_v0[l - 1][None, :]
            v2 = jnp.pad(vres_v2[l - 1], ((0, LANES - vres_rank), (0, 0))).astype(BF16)
        else:
            v0 = v2 = None
        r_, lw_, k_, v_, a_, b_, g_ = _rwkv_prep_call(
            p_rwkv, tail, v_first, shift_mu[l][None, :], decay_w0[l][None, :], lora_w,
            aaa_a0[l][None, :], gate_g2[l].astype(BF16), k_k[l][None, :], k_a[l][None, :], gsum,
            v0, v2, hd=hd, ts=tm)
        if not has_vres:
            v_first = v_
        y_a = _rwkv_chunk_call(r_, lw_, k_, v_, a_, b_, g_, gmean, r_k[l].reshape(1, hd),
                               lnx_w[l][None, :], lnx_b[l][None, :], n_sub=RWKV_SUBCHUNKS)

        y_b = _attn_call(q, k, v, tq=tq_attn, tk=tk_attn, n_pairs=hd // LANES)

        h = _merge_call(h, y_a, y_b, gates, mod, w_up_a[l].astype(BF16), w_up_b[l].astype(BF16),
                        w_out[l].astype(BF16), tm=tm)

        h = _ffn_call(h, norm_g[l], mod, ffn2_wg[l].astype(BF16), ffn2_wu[l].astype(BF16),
                      ffn2_wd[l].astype(BF16), sub=2, tm=tm, tf=256)
    return h
```

```python
import functools
import math

import jax
import jax.numpy as jnp
from jax import lax
from jax.experimental import pallas as pl
from jax.experimental.pallas import tpu as pltpu

F32 = jnp.float32
BF16 = jnp.bfloat16

HEAD_DIM = 64
RMS_EPS = 1e-6
GN_EPS = 64e-5
N_MOD = 9
LANES = 128
VMEM_LIMIT = 56 * 1024 * 1024
RWKV_CHUNK = 64
RWKV_SUBCHUNKS = 4

_NN = (((1,), (0,)), ((), ()))
_NT = (((1,), (1,)), ((), ()))


def _mm(a, b, dims=_NN):
    return lax.dot_general(a, b, dims, preferred_element_type=F32)


def _split2(x):
    hi = x.astype(BF16)
    lo = (x - hi.astype(F32)).astype(BF16)
    return hi, lo


def _split3(x):
    hi = x.astype(BF16)
    r1 = x - hi.astype(F32)
    mid = r1.astype(BF16)
    lo = (r1 - mid.astype(F32)).astype(BF16)
    return hi, mid, lo


def _mm3(a, b, dims=_NN):
    ah, al = _split2(a)
    bh, bl = _split2(b)
    return _mm(ah, bh, dims) + (_mm(ah, bl, dims) + _mm(al, bh, dims))


def _mmp(a, b, passes, dims=_NN):
    if passes == 1:
        return _mm(a.astype(BF16), b.astype(BF16), dims)
    return _mm3(a, b, dims)


def _mm_lhs_exact(a_bf16, b):
    hi, mid, lo = _split3(b)
    return _mm(a_bf16, hi) + (_mm(a_bf16, mid) + _mm(a_bf16, lo))


def _mm_rhs_exact(a, b_bf16):
    hi, mid, lo = _split3(a)
    return _mm(hi, b_bf16) + (_mm(mid, b_bf16) + _mm(lo, b_bf16))


def _params(*sem):
    return pltpu.CompilerParams(dimension_semantics=sem, vmem_limit_bytes=VMEM_LIMIT)


def _resident(shape):
    zeros = (0,) * len(shape)
    return pl.BlockSpec(shape, lambda *_: zeros, pipeline_mode=pl.Buffered(1))


def _norm_mod(x, g, sh, sc):
    ms = jnp.mean(x * x, axis=-1, keepdims=True)
    y = x * lax.rsqrt(ms + RMS_EPS) * g
    return y * (1.0 + sc) + sh


LOG2E = 1.4426950408889634
ATTN_DEAD_LOG2 = -160.0


def _log2(x):
    return jnp.log(x) * LOG2E


def _softplus(x):
    return jnp.maximum(x, 0.0) + jnp.log(1.0 + jnp.exp(-jnp.abs(x)))


def _mod_body(c_ref, w_ref, b_ref, o_ref):
    c = c_ref[...]
    ca = (c * jax.nn.sigmoid(c)).astype(BF16)
    o_ref[0] = _mm(ca, w_ref[0].astype(BF16)) + b_ref[0]


def _mod_call(c, mod_w, mod_b):
    n_layers, d, nd = mod_w.shape
    bsz = c.shape[0]
    tn = d
    return pl.pallas_call(
        _mod_body,
        grid=(n_layers, nd // tn),
        in_specs=[
            pl.BlockSpec((bsz, d), lambda l, j: (0, 0)),
            pl.BlockSpec((1, d, tn), lambda l, j: (l, 0, j)),
            pl.BlockSpec((1, 1, tn), lambda l, j: (l, 0, j)),
        ],
        out_specs=pl.BlockSpec((1, bsz, tn), lambda l, j: (l, 0, j)),
        out_shape=jax.ShapeDtypeStruct((n_layers, bsz, nd), F32),
        compiler_params=_params("parallel", "parallel"),
        name="adaln_mod",
    )(c, mod_w, mod_b.reshape(n_layers, 1, nd))


def _ffn_body(h_ref, g_ref, mod_ref, wg_ref, wu_ref, wd_ref, o_ref, *, sub, tf):
    x = h_ref[0]
    g = g_ref[sub:sub + 1, :]
    sh = mod_ref[0, 3 * sub:3 * sub + 1, :]
    sc = mod_ref[0, 3 * sub + 1:3 * sub + 2, :]
    gt = mod_ref[0, 3 * sub + 2:3 * sub + 3, :]
    n = _norm_mod(x, g, sh, sc).astype(BF16)
    d_ff = wg_ref.shape[1]
    acc = jnp.zeros_like(x)
    for j in range(d_ff // tf):
        gj = _mm(n, wg_ref[:, j * tf:(j + 1) * tf])
        uj = _mm(n, wu_ref[:, j * tf:(j + 1) * tf])
        a = (gj * jax.nn.sigmoid(gj) * uj).astype(BF16)
        acc = acc + _mm(a, wd_ref[j * tf:(j + 1) * tf, :])
    o_ref[0] = x + (0.5 * gt) * acc


def _ffn_call(h, norm_g, mod, wg, wu, wd, *, sub, tm, tf):
    bsz, slen, d = h.shape
    d_ff = wg.shape[1]
    tok = pl.BlockSpec((1, tm, d), lambda b, i: (b, i, 0))
    return pl.pallas_call(
        functools.partial(_ffn_body, sub=sub, tf=tf),
        grid=(bsz, slen // tm),
        in_specs=[
            tok,
            _resident(norm_g.shape),
            pl.BlockSpec((1, N_MOD, d), lambda b, i: (b, 0, 0)),
            _resident((d, d_ff)), _resident((d, d_ff)), _resident((d_ff, d)),
        ],
        out_specs=tok,
        out_shape=jax.ShapeDtypeStruct(h.shape, F32),
        compiler_params=_params("parallel", "parallel"),
        name="ffn",
    )(h, norm_g, mod, wg, wu, wd)


def _head_rms(t, g, gmean):
    hi, lo = _split2(t * t)
    ms = _mm(hi, gmean) + _mm(lo, gmean)
    return t * lax.rsqrt(ms + RMS_EPS) * g


def _inproj_body(h_ref, g_ref, mod_ref, w_ref, gq_ref, gk_ref, gmean_ref, *outs,
                 rw, hd, d, has_vres):
    p_ref, q_ref, k_ref, v_ref, gate_ref = outs[:5]
    x = h_ref[0]
    n = _norm_mod(x, g_ref[1:2, :], mod_ref[0, 3:4, :], mod_ref[0, 4:5, :]).astype(BF16)
    gmean = gmean_ref[...]
    p_ref[0] = _mm(n, w_ref[:, 0:rw])
    q = _mm(n, w_ref[:, rw:rw + hd])
    q_ref[0] = (_head_rms(q, gq_ref[...], gmean) * (LOG2E / math.sqrt(HEAD_DIM))).astype(BF16)
    k = _mm(n, w_ref[:, rw + hd:rw + 2 * hd])
    k_ref[0] = _head_rms(k, gk_ref[...], gmean).astype(BF16)
    v_ref[0] = _mm(n, w_ref[:, rw + 2 * hd:rw + 3 * hd]).astype(BF16)
    og = rw + 3 * hd
    gate_ref[0] = jax.nn.sigmoid(_mm(n, w_ref[:, og:og + 2 * d])).astype(BF16)
    if has_vres:
        outs[5][0] = _mm(n, w_ref[:, og + 2 * d:og + 2 * d + LANES])


def _inproj_call(h, norm_g, mod, w, gq, gk, gmean, *, rw, hd, has_vres, tm):
    bsz, slen, d = h.shape

    def tok(width):
        return pl.BlockSpec((1, tm, width), lambda b, i: (b, i, 0))

    widths = [(rw, F32), (hd, BF16), (hd, BF16), (hd, BF16), (2 * d, BF16)]
    if has_vres:
        widths.append((LANES, F32))
    return pl.pallas_call(
        functools.partial(_inproj_body, rw=rw, hd=hd, d=d, has_vres=has_vres),
        grid=(bsz, slen // tm),
        in_specs=[
            tok(d),
            _resident(norm_g.shape),
            pl.BlockSpec((1, N_MOD, d), lambda b, i: (b, 0, 0)),
            _resident(w.shape), _resident(gq.shape), _resident(gk.shape), _resident(gmean.shape),
        ],
        out_specs=[tok(wd) for wd, _ in widths],
        out_shape=[jax.ShapeDtypeStruct((bsz, slen, wd), dt) for wd, dt in widths],
        compiler_params=_params("parallel", "parallel"),
        name="in_proj",
    )(h, norm_g, mod, w, gq, gk, gmean)


def _attn_body(q_ref, k_ref, v_ref, o_ref, *, tq, tk, n_pairs):
    assert tk == LANES
    qi = pl.program_id(2)
    lane = lax.broadcasted_iota(jnp.int32, (1, LANES), 1)
    first = lane < HEAD_DIM
    row = lax.broadcasted_iota(jnp.int32, (tq, 2 * tk), 0)
    col = lax.broadcasted_iota(jnp.int32, (tq, 2 * tk), 1)
    col = jnp.where(col >= tk, col - tk, col)
    jj = lax.broadcasted_iota(jnp.int32, (2 * tk, 2 * tk), 0)
    ss = lax.broadcasted_iota(jnp.int32, (2 * tk, 2 * tk), 1)
    jj = jnp.where(jj >= tk, jj - tk, jj)
    suffix_total = jnp.where((ss >= tk) | (jj > ss), -1.0, 0.0).astype(BF16)

    def both_heads(x):
        zero = jnp.zeros_like(x)
        return jnp.concatenate([jnp.where(first, x, zero), jnp.where(first, zero, x)], axis=0)

    def block(k0, carry, diag_off):
        r0 = 0 if diag_off is None else diag_off
        new = []
        for p in range(n_pairs):
            acc_all, logsurv_all = carry[2 * p], carry[2 * p + 1]
            acc, logsurv = acc_all[r0:], logsurv_all[r0:]
            qp = q_ref[0, r0:, p * LANES:(p + 1) * LANES]
            k2 = both_heads(k_ref[0, pl.ds(k0, tk), p * LANES:(p + 1) * LANES])
            v2 = both_heads(v_ref[0, pl.ds(k0, tk), p * LANES:(p + 1) * LANES])
            z = _mm(qp, k2, _NT)
            neg_abs = lax.bitcast_convert_type(
                lax.bitcast_convert_type(z, jnp.uint32) | jnp.uint32(0x80000000), F32)
            sp = jnp.maximum(z, 0.0) + _log2(1.0 + jnp.exp2(neg_abs))
            if diag_off is not None:
                causal = (col + diag_off < row)[r0:]
                sp = jnp.where(causal, sp, 0.0)
            hi, lo = _split2(sp)
            st = [_mm(jnp.concatenate([hi[:, e * tk:(e + 1) * tk], lo[:, e * tk:(e + 1) * tk]],
                                      axis=1), suffix_total) for e in range(2)]
            suffix = jnp.concatenate([st[0][:, :tk], st[1][:, :tk]], axis=1)
            total = jnp.concatenate([st[0][:, tk:], st[1][:, tk:]], axis=1)
            w = jnp.exp2((z - sp) + logsurv + suffix)
            if diag_off is not None:
                w = jnp.where(causal, w, 0.0)
            acc = acc + _mm(w.astype(BF16), v2)
            logsurv = logsurv + total
            if r0:
                acc = jnp.concatenate([acc_all[:r0], acc], axis=0)
                logsurv = jnp.concatenate([logsurv_all[:r0], logsurv], axis=0)
            new += [acc, logsurv]
        return tuple(new)

    carry = (jnp.zeros((tq, LANES), F32), jnp.zeros((tq, 2 * tk), F32)) * n_pairs
    n_diag = tq // tk
    q0 = qi * tq
    for j in reversed(range(n_diag)):
        carry = block(pl.multiple_of(q0 + j * tk, tk), carry, j * tk)

    def alive(carry):
        worst = carry[1]
        for p in range(1, n_pairs):
            worst = jnp.maximum(worst, carry[2 * p + 1])
        return (jnp.max(worst) > ATTN_DEAD_LOG2).astype(jnp.int32)

    n_off = qi * n_diag

    def cond(state):
        return (state[0] < n_off) & (state[1] > 0)

    def body(state):
        i = state[0]
        carry = block(pl.multiple_of(q0 - (i + 1) * tk, tk), state[2:], None)
        return (i + 1, alive(carry)) + carry

    carry = lax.while_loop(cond, body, (jnp.int32(0), alive(carry)) + carry)[2:]
    for p in range(n_pairs):
        o_ref[0, :, p * LANES:(p + 1) * LANES] = carry[2 * p].astype(o_ref.dtype)


def _attn_call(q, k, v, *, tq, tk, n_pairs):
    bsz, slen, hd = q.shape
    width = n_pairs * LANES
    qblk = pl.BlockSpec((1, tq, width), lambda b, p, i: (b, i, p))
    kvblk = pl.BlockSpec((1, slen, width), lambda b, p, i: (b, 0, p))
    return pl.pallas_call(
        functools.partial(_attn_body, tq=tq, tk=tk, n_pairs=n_pairs),
        grid=(bsz, hd // width, slen // tq),
        in_specs=[qblk, kvblk, kvblk],
        out_specs=qblk,
        out_shape=jax.ShapeDtypeStruct(q.shape, BF16),
        compiler_params=_params("parallel", "parallel", "arbitrary"),
        name="stick_breaking_attn",
    )(q, k, v)


def _rwkv_prep_body(*refs, hd, has_vres):
    if has_vres:
        (p_ref, prev_ref, tail_ref, vf_ref, mu_ref, w0_ref, lora_ref, a0_ref, g2_ref, kk_ref,
         ka_ref, gsum_ref, v0_ref, v2_ref, r_o, lw_o, k_o, v_o, a_o, b_o, g_o) = refs
    else:
        (p_ref, prev_ref, mu_ref, w0_ref, lora_ref, a0_ref, g2_ref, kk_ref,
         ka_ref, gsum_ref, r_o, lw_o, k_o, v_o, a_o, b_o, g_o) = refs
    i = pl.program_id(1)
    p = p_ref[0]
    ts = p.shape[0]
    prev = prev_ref[0][7:8, :]
    prev = jnp.where(i == 0, jnp.zeros_like(prev), prev)
    shifted = pltpu.roll(p, 1, axis=0)
    rows = lax.broadcasted_iota(jnp.int32, (ts, 1), 0)
    shifted = jnp.where(rows == 0, prev, shifted)
    p = p + (shifted - p) * mu_ref[...]
    r = p[:, 0:hd]
    k = p[:, hd:2 * hd]
    v = p[:, 2 * hd:3 * hd]
    wa = p[:, 3 * hd:3 * hd + LANES]
    gl = p[:, 3 * hd + LANES:3 * hd + 2 * LANES]
    lane = lax.broadcasted_iota(jnp.int32, (1, LANES), 1)
    wa = jnp.where(lane < wa.shape[1] // 2, jnp.tanh(wa), wa)
    lora = _mm3(wa, lora_ref[...])
    w = -_softplus(-(w0_ref[...] + lora[:, 0:hd])) - 0.5
    lw_o[0] = -jnp.exp(w)
    a = jax.nn.sigmoid(a0_ref[...] + lora[:, hd:2 * hd])
    g_o[0] = _mm(jax.nn.sigmoid(gl).astype(BF16), g2_ref[...])
    if has_vres:
        mix = jax.nn.sigmoid(v0_ref[...] + _mm(tail_ref[0].astype(BF16), v2_ref[...]))
        v = v + (vf_ref[0] - v) * mix
    kk = k * kk_ref[...]
    ss = _mm_rhs_exact(kk * kk, gsum_ref[...])
    kk = kk / jnp.maximum(jnp.sqrt(ss), 1e-12)
    r_o[0] = r
    k_o[0] = k * (1.0 + (a - 1.0) * ka_ref[...])
    v_o[0] = v
    a_o[0] = -kk
    b_o[0] = kk * a


def _rwkv_prep_call(p, tail, v_first, mu, w0, lora_w, a0, g2, k_k, k_a, gsum, v0, v2, *, hd, ts):
    bsz, slen, rw = p.shape
    has_vres = tail is not None

    def tok(width):
        return pl.BlockSpec((1, ts, width), lambda b, i: (b, i, 0))

    prev = pl.BlockSpec((1, 8, rw), lambda b, i: (b, jnp.maximum(i * (ts // 8) - 1, 0), 0))
    args = [p, p]
    specs = [tok(rw), prev]
    if has_vres:
        args += [tail, v_first]
        specs += [tok(LANES), tok(hd)]
    consts = [mu, w0, lora_w, a0, g2, k_k, k_a, gsum]
    if has_vres:
        consts += [v0, v2]
    args += consts
    specs += [_resident(c.shape) for c in consts]
    return pl.pallas_call(
        functools.partial(_rwkv_prep_body, hd=hd, has_vres=has_vres),
        grid=(bsz, slen // ts),
        in_specs=specs,
        out_specs=[tok(hd)] * 7,
        out_shape=[jax.ShapeDtypeStruct((bsz, slen, hd), F32)] * 7,
        compiler_params=_params("parallel", "parallel"),
        name="rwkv_prep",
    )(*args)


def _rwkv_chunk_body(r_ref, lw_ref, k_ref, v_ref, a_ref, b_ref, g_ref, tri_ref, gmean_ref,
                     rk_ref, lnw_ref, lnb_ref, o_ref, st_ref, *, n_sub):
    c = pl.program_id(1)

    @pl.when(c == 0)
    def _():
        st_ref[...] = jnp.zeros_like(st_ref)

    r, lw, k, v = r_ref[0], lw_ref[0], k_ref[0], v_ref[0]
    a, b = a_ref[0], b_ref[0]
    ch = RWKV_CHUNK
    n_pairs = r.shape[1] // LANES
    cum = _mm_lhs_exact(tri_ref[...], lw)
    cum_prev = cum - lw
    lasts = [cum[(s + 1) * ch - 1:(s + 1) * ch, :] for s in range(n_sub)]
    last = jnp.concatenate([jnp.broadcast_to(t, (ch, t.shape[1])) for t in lasts], axis=0)
    e_pos = jnp.exp(cum)
    e_neg = jnp.exp(-cum)
    e_end = jnp.exp(last - cum)
    at = a * jnp.exp(cum_prev)
    rt = r * e_pos
    bt = b * e_neg
    kt = k * e_neg
    bh = b * e_end
    kh = k * e_end
    p_end = [jnp.exp(t) for t in lasts]

    lane = lax.broadcasted_iota(jnp.int32, (1, LANES), 1)
    first = lane < HEAD_DIM

    def stacked(x, s, p):
        x = x[s * ch:(s + 1) * ch, p * LANES:(p + 1) * LANES]
        zero = jnp.zeros_like(x)
        return jnp.concatenate([jnp.where(first, x, zero), jnp.where(first, zero, x)], axis=0)

    row = lax.broadcasted_iota(jnp.int32, (2 * ch, 2 * ch), 0)
    col = lax.broadcasted_iota(jnp.int32, (2 * ch, 2 * ch), 1)
    strict = (row & (ch - 1)) > (col & (ch - 1))
    row2 = lax.broadcasted_iota(jnp.int32, (2 * ch, 4 * ch), 0)
    col2 = lax.broadcasted_iota(jnp.int32, (2 * ch, 4 * ch), 1)
    lower2 = (row2 & (ch - 1)) >= (col2 & (ch - 1))
    diag = row == col
    eye = jnp.where(diag, 1.0, 0.0)
    n_levels = int(math.log2(ch))

    def sub_diag(level):
        size, half = 1 << level, 1 << (level - 1)
        inside = (row >> level) == (col >> level)
        return inside & ((row & (size - 1)) >= half) & ((col & (size - 1)) < half)

    units = [(s, p) for s in range(n_sub) for p in range(n_pairs)]
    at_s, rt_s, bt_s, kt_s, bh_s, kh_s, v_s = (
        [stacked(t, s, p) for s, p in units] for t in (at, rt, bt, kt, bh, kh, v))
    idx = range(len(units))
    prod = [_mm(jnp.concatenate([at_s[u], rt_s[u]], axis=0).astype(BF16),
                jnp.concatenate([bt_s[u], kt_s[u]], axis=0).astype(BF16), _NT) for u in idx]
    a_ab = [jnp.where(strict, prod[u][:2 * ch, :2 * ch], 0.0) for u in idx]
    a_ak = [jnp.where(strict, prod[u][:2 * ch, 2 * ch:], 0.0).astype(BF16) for u in idx]
    a_rbk = [jnp.where(lower2, prod[u][2 * ch:, :], 0.0).astype(BF16) for u in idx]
    lvl = sub_diag(1)
    tinv = [eye + jnp.where(lvl, a_ab[u], 0.0) for u in idx]
    for level in range(2, n_levels + 1):
        lvl = sub_diag(level)
        tinv_b = [tinv[u].astype(BF16) for u in idx]
        right = [_mm(jnp.where(lvl, a_ab[u], 0.0).astype(BF16), tinv_b[u]) for u in idx]
        tinv = [tinv[u] + _mm(tinv_b[u], right[u].astype(BF16)) for u in idx]
    v_b = [v_s[u].astype(BF16) for u in idx]
    akv = [_mm(a_ak[u], v_b[u]) for u in idx]
    wx = [_mm(tinv[u].astype(BF16),
              jnp.concatenate([at_s[u], akv[u]], axis=1).astype(BF16)) for u in idx]
    zero_b = jnp.zeros((2 * ch, LANES), BF16)
    wxv = [jnp.concatenate([wx[u].astype(BF16), jnp.concatenate([zero_b, v_b[u]], axis=1)], axis=0)
           for u in idx]
    qy = [_mm(a_rbk[u], wxv[u]) for u in idx]
    mn = [_mm(jnp.concatenate([bh_s[u], kh_s[u]], axis=0).T.astype(BF16), wxv[u]) for u in idx]
    ys = []
    for s in range(n_sub):
        rows = []
        for p in range(n_pairs):
            u = s * n_pairs + p
            st = st_ref[p]
            y_st = _mm((rt_s[u] + qy[u][:, :LANES]).astype(BF16), st.astype(BF16)) + qy[u][:, LANES:]
            rows.append(y_st[:ch] + y_st[ch:])
            m_t = jnp.where(diag, p_end[s][:, p * LANES:(p + 1) * LANES], 0.0) + mn[u][:, :LANES]
            st_ref[p] = _mm3(m_t, st) + mn[u][:, LANES:]
        ys.append(jnp.concatenate(rows, axis=1))
    y = jnp.concatenate(ys, axis=0)

    gmean = gmean_ref[...]
    y_hi, y_lo = _split2(y)
    mean = _mm(y_hi, gmean) + _mm(y_lo, gmean)
    dev = y - mean
    var = _mm((dev * dev).astype(BF16), gmean)
    yn = dev * lax.rsqrt(var + GN_EPS) * lnw_ref[...] + lnb_ref[...]
    bonus = _mm((r * k * rk_ref[...]).astype(BF16), gmean) * float(HEAD_DIM) * v
    o_ref[0] = ((yn + bonus) * g_ref[0]).astype(o_ref.dtype)


def _rwkv_chunk_call(r, lw, k, v, a, b, g, gmean, rk, lnw, lnb, *, n_sub):
    bsz, slen, hd = r.shape
    rows = n_sub * RWKV_CHUNK
    idx = jnp.arange(rows)
    same_chunk = idx[:, None] // RWKV_CHUNK == idx[None, :] // RWKV_CHUNK
    tri = (same_chunk & (idx[:, None] >= idx[None, :])).astype(BF16)
    tok = pl.BlockSpec((1, rows, hd), lambda bi, ci: (bi, ci, 0))
    consts = [tri, gmean, rk, lnw, lnb]
    return pl.pallas_call(
        functools.partial(_rwkv_chunk_body, n_sub=n_sub),
        grid=(bsz, slen // rows),
        in_specs=[tok] * 7 + [_resident(c.shape) for c in consts],
        out_specs=tok,
        out_shape=jax.ShapeDtypeStruct((bsz, slen, hd), BF16),
        scratch_shapes=[pltpu.VMEM((hd // LANES, LANES, LANES), F32)],
        compiler_params=_params("parallel", "arbitrary"),
        name="rwkv_chunk_scan",
    )(r, lw, k, v, a, b, g, *consts)


def _merge_body(h_ref, ya_ref, yb_ref, gate_ref, mod_ref, wa_ref, wb_ref, wo_ref, o_ref):
    d = h_ref.shape[-1]
    ua = _mm(ya_ref[0], wa_ref[...])
    ub = _mm(yb_ref[0], wb_ref[...])
    merged = gate_ref[0, :, 0:d].astype(F32) * ua + gate_ref[0, :, d:2 * d].astype(F32) * ub
    out = _mm(merged.astype(BF16), wo_ref[...])
    o_ref[0] = h_ref[0] + mod_ref[0, 5:6, :] * out


def _merge_call(h, ya, yb, gates, mod, wa, wb, wo, *, tm):
    bsz, slen, d = h.shape

    def tok(width):
        return pl.BlockSpec((1, tm, width), lambda b, i: (b, i, 0))

    return pl.pallas_call(
        _merge_body,
        grid=(bsz, slen // tm),
        in_specs=[
            tok(d), tok(ya.shape[-1]), tok(yb.shape[-1]), tok(2 * d),
            pl.BlockSpec((1, N_MOD, d), lambda b, i: (b, 0, 0)),
            _resident(wa.shape), _resident(wb.shape), _resident(wo.shape),
        ],
        out_specs=tok(d),
        out_shape=jax.ShapeDtypeStruct(h.shape, F32),
        compiler_params=_params("parallel", "parallel"),
        name="merge_out",
    )(h, ya, yb, gates, mod, wa, wb, wo)


def _block_diag_const(width, group, value, dtype):
    idx = jnp.arange(width) // group
    return jnp.where(idx[:, None] == idx[None, :], value, 0.0).astype(dtype)


def kernel(x, c, mod_w, mod_b, norm_g, ffn1_wg, ffn1_wu, ffn1_wd, w_in_first, w_in_rest, shift_mu, decay_w0, decay_w2, aaa_a0, aaa_a2, gate_g2, k_k, k_a, r_k, lnx_w, lnx_b, vres_v0, vres_v2, q_norm_g, k_norm_g, w_up_a, w_up_b, w_out, ffn2_wg, ffn2_wu, ffn2_wd):
    bsz, slen, d = x.shape
    n_layers = mod_w.shape[0]
    hd = decay_w0.shape[-1]
    n_heads = hd // HEAD_DIM
    decay_rank, aaa_rank = decay_w2.shape[1], aaa_a2.shape[1]
    gate_rank, vres_rank = gate_g2.shape[1], vres_v2.shape[1]
    assert decay_rank + aaa_rank == LANES and gate_rank == LANES and vres_rank <= LANES
    assert w_up_b.shape[1] == hd and hd % LANES == 0
    rw = 3 * hd + decay_rank + aaa_rank + gate_rank
    cols_first = rw + 3 * hd + 2 * d
    assert w_in_first.shape[1] == cols_first

    tm = min(512, slen)
    tq_attn = min(512, slen)
    tk_attn = LANES

    gmean = _block_diag_const(hd, HEAD_DIM, 1.0 / HEAD_DIM, BF16)
    gsum = _block_diag_const(hd, HEAD_DIM, 1.0, BF16)

    mod_all = _mod_call(c, mod_w, mod_b).reshape(n_layers, bsz, N_MOD, d)

    h = x
    v_first = None
    for l in range(n_layers):
        mod = mod_all[l]
        h = _ffn_call(h, norm_g[l], mod, ffn1_wg[l].astype(BF16), ffn1_wu[l].astype(BF16),
                      ffn1_wd[l].astype(BF16), sub=0, tm=tm, tf=256)

        has_vres = l > 0
        if has_vres:
            w_in = jnp.pad(w_in_rest[l - 1], ((0, 0), (0, LANES - vres_rank)))
        else:
            w_in = w_in_first
        gq = jnp.tile(q_norm_g[l], n_heads)[None, :]
        gk = jnp.tile(k_norm_g[l], n_heads)[None, :]
        proj = _inproj_call(h, norm_g[l], mod, w_in.astype(BF16), gq, gk, gmean,
                            rw=rw, hd=hd, has_vres=has_vres, tm=tm)
        p_rwkv, q, k, v, gates = proj[:5]
        tail = proj[5] if has_vres else None

        lora_w = jnp.zeros((LANES, 2 * hd), F32)
        lora_w = lora_w.at[:decay_rank, :hd].set(decay_w2[l]).at[decay_rank:, hd:].set(aaa_a2[l])
        if has_vres:
            v0 = vres_v0[l - 1][None, :]
            v2 = jnp.pad(vres_v2[l - 1], ((0, LANES - vres_rank), (0, 0))).astype(BF16)
        else:
            v0 = v2 = None
        r_, lw_, k_, v_, a_, b_, g_ = _rwkv_prep_call(
            p_rwkv, tail, v_first, shift_mu[l][None, :], decay_w0[l][None, :], lora_w,
            aaa_a0[l][None, :], gate_g2[l].astype(BF16), k_k[l][None, :], k_a[l][None, :], gsum,
            v0, v2, hd=hd, ts=tm)
        if not has_vres:
            v_first = v_
        y_a = _rwkv_chunk_call(r_, lw_, k_, v_, a_, b_, g_, gmean, r_k[l].reshape(1, hd),
                               lnx_w[l][None, :], lnx_b[l][None, :], n_sub=RWKV_SUBCHUNKS)

        y_b = _attn_call(q, k, v, tq=tq_attn, tk=tk_attn, n_pairs=hd // LANES)

        h = _merge_call(h, y_a, y_b, gates, mod, w_up_a[l].astype(BF16), w_up_b[l].astype(BF16),
                        w_out[l].astype(BF16), tm=tm)

        h = _ffn_call(h, norm_g[l], mod, ffn2_wg[l].astype(BF16), ffn2_wu[l].astype(BF16),
                      ffn2_wd[l].astype(BF16), sub=2, tm=tm, tf=256)
    return h
```

```python
import functools
import math

import jax
import jax.numpy as jnp
from jax import lax
from jax.experimental import pallas as pl
from jax.experimental.pallas import tpu as pltpu

F32 = jnp.float32
BF16 = jnp.bfloat16

HEAD_DIM = 64
RMS_EPS = 1e-6
GN_EPS = 64e-5
N_MOD = 9
LANES = 128
MXU_WIDTH = 256
VMEM_LIMIT = 56 * 1024 * 1024
RWKV_CHUNK = 64
RWKV_SUBCHUNKS = 4

_NN = (((1,), (0,)), ((), ()))
_NT = (((1,), (1,)), ((), ()))


def _mm(a, b, dims=_NN):
    return lax.dot_general(a, b, dims, preferred_element_type=F32)


def _split2(x):
    hi = x.astype(BF16)
    lo = (x - hi.astype(F32)).astype(BF16)
    return hi, lo


def _split3(x):
    hi = x.astype(BF16)
    r1 = x - hi.astype(F32)
    mid = r1.astype(BF16)
    lo = (r1 - mid.astype(F32)).astype(BF16)
    return hi, mid, lo


def _mm3(a, b, dims=_NN):
    ah, al = _split2(a)
    bh, bl = _split2(b)
    return _mm(ah, bh, dims) + (_mm(ah, bl, dims) + _mm(al, bh, dims))


def _mm_lhs_exact(a_bf16, b):
    hi, mid, lo = _split3(b)
    return _mm(a_bf16, hi) + (_mm(a_bf16, mid) + _mm(a_bf16, lo))


def _groups(x_bf16, g):
    n = g.shape[0]
    slabs = [_mm(x_bf16[:, i * n:(i + 1) * n], g) for i in range(x_bf16.shape[1] // n)]
    return jnp.concatenate(slabs, axis=1)


def _params(*sem):
    return pltpu.CompilerParams(dimension_semantics=sem, vmem_limit_bytes=VMEM_LIMIT)


def _resident(shape):
    zeros = (0,) * len(shape)
    return pl.BlockSpec(shape, lambda *_: zeros, pipeline_mode=pl.Buffered(1))


def _norm_mod(x, g, sh, sc):
    ms = jnp.mean(x * x, axis=-1, keepdims=True)
    y = x * lax.rsqrt(ms + RMS_EPS) * g
    return y * (1.0 + sc) + sh


LOG2E = 1.4426950408889634
ATTN_DEAD_LOG2 = -160.0


def _log2(x):
    return jnp.log(x) * LOG2E


def _softplus(x):
    return jnp.maximum(x, 0.0) + jnp.log(1.0 + jnp.exp(-jnp.abs(x)))


def _mod_body(c_ref, w_ref, b_ref, o_ref):
    c = c_ref[...]
    ca = (c * jax.nn.sigmoid(c)).astype(BF16)
    o_ref[0] = _mm(ca, w_ref[0].astype(BF16)) + b_ref[0]


def _mod_call(c, mod_w, mod_b):
    n_layers, d, nd = mod_w.shape
    bsz = c.shape[0]
    tn = d
    return pl.pallas_call(
        _mod_body,
        grid=(n_layers, nd // tn),
        in_specs=[
            pl.BlockSpec((bsz, d), lambda l, j: (0, 0)),
            pl.BlockSpec((1, d, tn), lambda l, j: (l, 0, j)),
            pl.BlockSpec((1, 1, tn), lambda l, j: (l, 0, j)),
        ],
        out_specs=pl.BlockSpec((1, bsz, tn), lambda l, j: (l, 0, j)),
        out_shape=jax.ShapeDtypeStruct((n_layers, bsz, nd), F32),
        compiler_params=_params("parallel", "parallel"),
        name="adaln_mod",
    )(c, mod_w, mod_b.reshape(n_layers, 1, nd))


def _ffn_body(h_ref, g_ref, mod_ref, wg_ref, wu_ref, wd_ref, o_ref, *, sub, tf):
    x = h_ref[0]
    g = g_ref[sub:sub + 1, :]
    sh = mod_ref[0, 3 * sub:3 * sub + 1, :]
    sc = mod_ref[0, 3 * sub + 1:3 * sub + 2, :]
    gt = mod_ref[0, 3 * sub + 2:3 * sub + 3, :]
    n = _norm_mod(x, g, sh, sc).astype(BF16)
    d_ff = wg_ref.shape[1]
    acc = jnp.zeros_like(x)
    for j in range(d_ff // tf):
        gj = _mm(n, wg_ref[:, j * tf:(j + 1) * tf])
        uj = _mm(n, wu_ref[:, j * tf:(j + 1) * tf])
        a = (gj * jax.nn.sigmoid(gj) * uj).astype(BF16)
        acc = acc + _mm(a, wd_ref[j * tf:(j + 1) * tf, :])
    o_ref[0] = x + (0.5 * gt) * acc


def _ffn_call(h, norm_g, mod, wg, wu, wd, *, sub, tm, tf):
    bsz, slen, d = h.shape
    d_ff = wg.shape[1]
    tok = pl.BlockSpec((1, tm, d), lambda b, i: (b, i, 0))
    return pl.pallas_call(
        functools.partial(_ffn_body, sub=sub, tf=tf),
        grid=(bsz, slen // tm),
        in_specs=[
            tok,
            _resident(norm_g.shape),
            pl.BlockSpec((1, N_MOD, d), lambda b, i: (b, 0, 0)),
            _resident((d, d_ff)), _resident((d, d_ff)), _resident((d_ff, d)),
        ],
        out_specs=tok,
        out_shape=jax.ShapeDtypeStruct(h.shape, F32),
        compiler_params=_params("parallel", "parallel"),
        name="ffn",
    )(h, norm_g, mod, wg, wu, wd)


def _head_rms(t, g, gmean):
    ms = _groups((t * t).astype(BF16), gmean)
    return t * lax.rsqrt(ms + RMS_EPS) * g


def _inproj_body(h_ref, g_ref, mod_ref, w_ref, gq_ref, gk_ref, gmean_ref, *outs,
                 rw, hd, d, has_vres):
    p_ref, q_ref, k_ref, v_ref, gate_ref = outs[:5]
    x = h_ref[0]
    n = _norm_mod(x, g_ref[1:2, :], mod_ref[0, 3:4, :], mod_ref[0, 4:5, :]).astype(BF16)
    gmean = gmean_ref[...]
    p_ref[0] = _mm(n, w_ref[:, 0:rw])
    q = _mm(n, w_ref[:, rw:rw + hd])
    q_ref[0] = (_head_rms(q, gq_ref[...], gmean) * (LOG2E / math.sqrt(HEAD_DIM))).astype(BF16)
    k = _mm(n, w_ref[:, rw + hd:rw + 2 * hd])
    k_ref[0] = _head_rms(k, gk_ref[...], gmean).astype(BF16)
    v_ref[0] = _mm(n, w_ref[:, rw + 2 * hd:rw + 3 * hd]).astype(BF16)
    og = rw + 3 * hd
    gate_ref[0] = jax.nn.sigmoid(_mm(n, w_ref[:, og:og + 2 * d])).astype(BF16)
    if has_vres:
        outs[5][0] = _mm(n, w_ref[:, og + 2 * d:og + 2 * d + LANES])


def _inproj_call(h, norm_g, mod, w, gq, gk, gmean, *, rw, hd, has_vres, tm):
    bsz, slen, d = h.shape

    def tok(width):
        return pl.BlockSpec((1, tm, width), lambda b, i: (b, i, 0))

    widths = [(rw, F32), (hd, BF16), (hd, BF16), (hd, BF16), (2 * d, BF16)]
    if has_vres:
        widths.append((LANES, F32))
    return pl.pallas_call(
        functools.partial(_inproj_body, rw=rw, hd=hd, d=d, has_vres=has_vres),
        grid=(bsz, slen // tm),
        in_specs=[
            tok(d),
            _resident(norm_g.shape),
            pl.BlockSpec((1, N_MOD, d), lambda b, i: (b, 0, 0)),
            _resident(w.shape), _resident(gq.shape), _resident(gk.shape), _resident(gmean.shape),
        ],
        out_specs=[tok(wd) for wd, _ in widths],
        out_shape=[jax.ShapeDtypeStruct((bsz, slen, wd), dt) for wd, dt in widths],
        compiler_params=_params("parallel", "parallel"),
        name="in_proj",
    )(h, norm_g, mod, w, gq, gk, gmean)


def _attn_body(q_ref, k_ref, v_ref, o_ref, *, tq, ts, n_pairs):
    tk = LANES
    n_strips = tq // ts
    pad = tq - ts
    qi = pl.program_id(2)
    q0 = qi * tq
    lane = lax.broadcasted_iota(jnp.int32, (1, LANES), 1)
    first = lane < HEAD_DIM
    row = lax.broadcasted_iota(jnp.int32, (ts, 2 * tk), 0)
    col = lax.broadcasted_iota(jnp.int32, (ts, 2 * tk), 1)
    col = jnp.where(col >= tk, col - tk, col)
    jj = lax.broadcasted_iota(jnp.int32, (2 * tk, 2 * tk), 0)
    ss = lax.broadcasted_iota(jnp.int32, (2 * tk, 2 * tk), 1)
    jj = jnp.where(jj >= tk, jj - tk, jj)
    suffix_total = jnp.where((ss >= tk) | (jj > ss), -1.0, 0.0).astype(BF16)

    def both_heads(x):
        zero = jnp.zeros_like(x)
        return jnp.concatenate([jnp.where(first, x, zero), jnp.where(first, zero, x)], axis=0)

    def strips(x, r0):
        return jnp.concatenate([x[s * ts + r0:(s + 1) * ts] for s in range(n_strips)], axis=0)

    def unstrips(old, part, r0):
        n = ts - r0
        pieces = []
        for s in range(n_strips):
            if r0:
                pieces.append(old[s * ts:s * ts + r0])
            pieces.append(part[s * n:(s + 1) * n])
        return jnp.concatenate(pieces, axis=0)

    def step(key_start, carry, diag_off):
        r0 = 0 if diag_off is None else diag_off
        n = ts - r0
        new = []
        for p in range(n_pairs):
            pair = slice(p * LANES, (p + 1) * LANES)
            acc_all, logsurv_all = carry[2 * p], carry[2 * p + 1]
            acc, logsurv = strips(acc_all, r0), strips(logsurv_all, r0)
            zs, v2 = [], []
            for s in range(n_strips):
                k0 = pl.multiple_of(key_start + s * ts, tk)
                k2 = both_heads(k_ref[0, pl.ds(k0, tk), pair])
                v2.append(both_heads(v_ref[0, pl.ds(k0, tk), pair]))
                zs.append(_mm(q_ref[0, s * ts + r0:(s + 1) * ts, pair], k2, _NT))
            z = jnp.concatenate(zs, axis=0)
            neg_abs = lax.bitcast_convert_type(
                lax.bitcast_convert_type(z, jnp.uint32) | jnp.uint32(0x80000000), F32)
            sp = jnp.maximum(z, 0.0) + _log2(1.0 + jnp.exp2(neg_abs))
            if diag_off is not None:
                causal = jnp.concatenate([(col + diag_off < row)[r0:]] * n_strips, axis=0)
                sp = jnp.where(causal, sp, 0.0)
            hi, lo = _split2(sp)
            st = [_mm(jnp.concatenate([hi[:, e * tk:(e + 1) * tk], lo[:, e * tk:(e + 1) * tk]],
                                      axis=1), suffix_total) for e in range(2)]
            suffix = jnp.concatenate([st[0][:, :tk], st[1][:, :tk]], axis=1)
            total = jnp.concatenate([st[0][:, tk:], st[1][:, tk:]], axis=1)
            w = jnp.exp2((z - sp) + logsurv + suffix)
            if diag_off is not None:
                w = jnp.where(causal, w, 0.0)
            w = w.astype(BF16)
            acc = acc + jnp.concatenate(
                [_mm(w[s * n:(s + 1) * n], v2[s]) for s in range(n_strips)], axis=0)
            new += [unstrips(acc_all, acc, r0), unstrips(logsurv_all, logsurv + total, r0)]
        return tuple(new)

    carry = (jnp.zeros((tq, LANES), F32), jnp.zeros((tq, 2 * tk), F32)) * n_pairs
    for j in reversed(range(ts // tk)):
        carry = step(q0 + pad + j * tk, carry, j * tk)

    def alive(carry):
        worst = carry[1]
        for p in range(1, n_pairs):
            worst = jnp.maximum(worst, carry[2 * p + 1])
        return (jnp.max(worst) > ATTN_DEAD_LOG2).astype(jnp.int32)

    n_off = (q0 + pad) // tk

    def cond(state):
        return (state[0] < n_off) & (state[1] > 0)

    def body(state):
        i = state[0]
        carry = step(q0 + pad - (i + 1) * tk, state[2:], None)
        return (i + 1, alive(carry)) + carry

    carry = lax.while_loop(cond, body, (jnp.int32(0), alive(carry)) + carry)[2:]
    for p in range(n_pairs):
        o_ref[0, :, p * LANES:(p + 1) * LANES] = carry[2 * p].astype(o_ref.dtype)


def _attn_call(q, k, v, *, tq, ts, n_pairs):
    bsz, slen, hd = q.shape
    width = n_pairs * LANES
    pad = tq - ts
    k, v = (jnp.pad(t, ((0, 0), (pad, 0), (0, 0))) for t in (k, v))
    qblk = pl.BlockSpec((1, tq, width), lambda b, p, i: (b, i, p))
    kvblk = pl.BlockSpec((1, slen + pad, width), lambda b, p, i: (b, 0, p))
    return pl.pallas_call(
        functools.partial(_attn_body, tq=tq, ts=ts, n_pairs=n_pairs),
        grid=(bsz, hd // width, slen // tq),
        in_specs=[qblk, kvblk, kvblk],
        out_specs=qblk,
        out_shape=jax.ShapeDtypeStruct(q.shape, BF16),
        compiler_params=_params("parallel", "parallel", "arbitrary"),
        name="stick_breaking_attn",
    )(q, k, v)


def _rwkv_prep_body(*refs, hd, has_vres):
    if has_vres:
        (p_ref, prev_ref, tail_ref, vf_ref, mu_ref, w0_ref, lora_ref, a0_ref, g2_ref, kk_ref,
         ka_ref, gsum_ref, v0_ref, v2_ref, r_o, lw_o, k_o, v_o, a_o, b_o, g_o) = refs
    else:
        (p_ref, prev_ref, mu_ref, w0_ref, lora_ref, a0_ref, g2_ref, kk_ref,
         ka_ref, gsum_ref, r_o, lw_o, k_o, v_o, a_o, b_o, g_o) = refs
    i = pl.program_id(1)
    p = p_ref[0]
    ts = p.shape[0]
    prev = prev_ref[0][7:8, :]
    prev = jnp.where(i == 0, jnp.zeros_like(prev), prev)
    shifted = pltpu.roll(p, 1, axis=0)
    rows = lax.broadcasted_iota(jnp.int32, (ts, 1), 0)
    shifted = jnp.where(rows == 0, prev, shifted)
    p = p + (shifted - p) * mu_ref[...]
    r = p[:, 0:hd]
    k = p[:, hd:2 * hd]
    v = p[:, 2 * hd:3 * hd]
    wa = p[:, 3 * hd:3 * hd + LANES]
    gl = p[:, 3 * hd + LANES:3 * hd + 2 * LANES]
    lane = lax.broadcasted_iota(jnp.int32, (1, LANES), 1)
    wa = jnp.where(lane < wa.shape[1] // 2, jnp.tanh(wa), wa)
    lora = _mm3(wa, lora_ref[...])
    w = -_softplus(-(w0_ref[...] + lora[:, 0:hd])) - 0.5
    lw_o[0] = -jnp.exp(w)
    a = jax.nn.sigmoid(a0_ref[...] + lora[:, hd:2 * hd])
    g_o[0] = _mm(jax.nn.sigmoid(gl).astype(BF16), g2_ref[...]).astype(g_o.dtype)
    if has_vres:
        mix = jax.nn.sigmoid(v0_ref[...] + _mm(tail_ref[0].astype(BF16), v2_ref[...]))
        v = v + (vf_ref[0].astype(F32) - v) * mix
    kk = k * kk_ref[...]
    sq_hi, sq_lo = _split2(kk * kk)
    ss = _groups(sq_hi, gsum_ref[...]) + _groups(sq_lo, gsum_ref[...])
    kk = kk / jnp.maximum(jnp.sqrt(ss), 1e-12)
    r_o[0] = r.astype(r_o.dtype)
    k_o[0] = (k * (1.0 + (a - 1.0) * ka_ref[...])).astype(k_o.dtype)
    v_o[0] = v.astype(v_o.dtype)
    a_o[0] = (-kk).astype(a_o.dtype)
    b_o[0] = (kk * a).astype(b_o.dtype)


def _rwkv_prep_call(p, tail, v_first, mu, w0, lora_w, a0, g2, k_k, k_a, gsum, v0, v2, *, hd, ts):
    bsz, slen, rw = p.shape
    has_vres = tail is not None

    def tok(width):
        return pl.BlockSpec((1, ts, width), lambda b, i: (b, i, 0))

    prev = pl.BlockSpec((1, 8, rw), lambda b, i: (b, jnp.maximum(i * (ts // 8) - 1, 0), 0))
    args = [p, p]
    specs = [tok(rw), prev]
    if has_vres:
        args += [tail, v_first]
        specs += [tok(LANES), tok(hd)]
    consts = [mu, w0, lora_w, a0, g2, k_k, k_a, gsum]
    if has_vres:
        consts += [v0, v2]
    args += consts
    specs += [_resident(c.shape) for c in consts]
    return pl.pallas_call(
        functools.partial(_rwkv_prep_body, hd=hd, has_vres=has_vres),
        grid=(bsz, slen // ts),
        in_specs=specs,
        out_specs=[tok(hd)] * 7,
        out_shape=[jax.ShapeDtypeStruct((bsz, slen, hd), F32 if i == 1 else BF16) for i in range(7)],
        compiler_params=_params("parallel", "parallel"),
        name="rwkv_prep",
    )(*args)


def _rwkv_chunk_body(r_ref, lw_ref, k_ref, v_ref, a_ref, b_ref, g_ref, tri_ref, gmean_ref,
                     rk_ref, lnw_ref, lnb_ref, o_ref, st_ref, *, n_sub):
    c = pl.program_id(1)

    @pl.when(c == 0)
    def _():
        st_ref[...] = jnp.zeros_like(st_ref)

    r, lw, k, v = r_ref[0].astype(F32), lw_ref[0], k_ref[0].astype(F32), v_ref[0].astype(F32)
    a, b = a_ref[0].astype(F32), b_ref[0].astype(F32)
    ch = RWKV_CHUNK
    n_pairs = r.shape[1] // LANES
    cum = _mm_lhs_exact(tri_ref[...], lw)
    cum_prev = cum - lw
    lasts = [cum[(s + 1) * ch - 1:(s + 1) * ch, :] for s in range(n_sub)]
    last = jnp.concatenate([jnp.broadcast_to(t, (ch, t.shape[1])) for t in lasts], axis=0)
    e_pos = jnp.exp(cum)
    e_neg = jnp.exp(-cum)
    e_end = jnp.exp(last - cum)
    at = a * jnp.exp(cum_prev)
    rt = r * e_pos
    bt = b * e_neg
    kt = k * e_neg
    bh = b * e_end
    kh = k * e_end
    p_end = [jnp.exp(t) for t in lasts]

    lane = lax.broadcasted_iota(jnp.int32, (1, LANES), 1)
    first = lane < HEAD_DIM

    def stacked(x, s, p):
        x = x[s * ch:(s + 1) * ch, p * LANES:(p + 1) * LANES]
        zero = jnp.zeros_like(x)
        return jnp.concatenate([jnp.where(first, x, zero), jnp.where(first, zero, x)], axis=0)

    row = lax.broadcasted_iota(jnp.int32, (2 * ch, 2 * ch), 0)
    col = lax.broadcasted_iota(jnp.int32, (2 * ch, 2 * ch), 1)
    strict = (row & (ch - 1)) > (col & (ch - 1))
    row2 = lax.broadcasted_iota(jnp.int32, (2 * ch, 4 * ch), 0)
    col2 = lax.broadcasted_iota(jnp.int32, (2 * ch, 4 * ch), 1)
    lower2 = (row2 & (ch - 1)) >= (col2 & (ch - 1))
    diag = row == col
    eye = jnp.where(diag, 1.0, 0.0)
    n_levels = int(math.log2(ch))

    def sub_diag(level):
        size, half = 1 << level, 1 << (level - 1)
        inside = (row >> level) == (col >> level)
        return inside & ((row & (size - 1)) >= half) & ((col & (size - 1)) < half)

    units = [(s, p) for s in range(n_sub) for p in range(n_pairs)]
    at_s, rt_s, bt_s, kt_s, bh_s, kh_s, v_s = (
        [stacked(t, s, p) for s, p in units] for t in (at, rt, bt, kt, bh, kh, v))
    idx = range(len(units))
    prod = [_mm(jnp.concatenate([at_s[u], rt_s[u]], axis=0).astype(BF16),
                jnp.concatenate([bt_s[u], kt_s[u]], axis=0).astype(BF16), _NT) for u in idx]
    a_ab = [jnp.where(strict, prod[u][:2 * ch, :2 * ch], 0.0) for u in idx]
    a_ak = [jnp.where(strict, prod[u][:2 * ch, 2 * ch:], 0.0).astype(BF16) for u in idx]
    a_rbk = [jnp.where(lower2, prod[u][2 * ch:, :], 0.0).astype(BF16) for u in idx]
    lvl = sub_diag(1)
    tinv = [eye + jnp.where(lvl, a_ab[u], 0.0) for u in idx]
    for level in range(2, n_levels + 1):
        lvl = sub_diag(level)
        tinv_b = [tinv[u].astype(BF16) for u in idx]
        right = [_mm(jnp.where(lvl, a_ab[u], 0.0).astype(BF16), tinv_b[u]) for u in idx]
        tinv = [tinv[u] + _mm(tinv_b[u], right[u].astype(BF16)) for u in idx]
    v_b = [v_s[u].astype(BF16) for u in idx]
    akv = [_mm(a_ak[u], v_b[u]) for u in idx]
    wx = [_mm(tinv[u].astype(BF16),
              jnp.concatenate([at_s[u], akv[u]], axis=1).astype(BF16)) for u in idx]
    zero_b = jnp.zeros((2 * ch, LANES), BF16)
    wxv = [jnp.concatenate([wx[u].astype(BF16), jnp.concatenate([zero_b, v_b[u]], axis=1)], axis=0)
           for u in idx]
    qy = [_mm(a_rbk[u], wxv[u]) for u in idx]
    mn = [_mm(jnp.concatenate([bh_s[u], kh_s[u]], axis=0).T.astype(BF16), wxv[u]) for u in idx]
    ys = []
    for s in range(n_sub):
        rows = []
        for p in range(n_pairs):
            u = s * n_pairs + p
            st = st_ref[p]
            y_st = _mm((rt_s[u] + qy[u][:, :LANES]).astype(BF16), st.astype(BF16)) + qy[u][:, LANES:]
            rows.append(y_st[:ch] + y_st[ch:])
            m_t = jnp.where(diag, p_end[s][:, p * LANES:(p + 1) * LANES], 0.0) + mn[u][:, :LANES]
            st_ref[p] = _mm3(m_t, st) + mn[u][:, LANES:]
        ys.append(jnp.concatenate(rows, axis=1))
    y = jnp.concatenate(ys, axis=0)

    gmean = gmean_ref[...]
    y_hi, y_lo = _split2(y)
    mean = _groups(y_hi, gmean) + _groups(y_lo, gmean)
    dev = y - mean
    var = _groups((dev * dev).astype(BF16), gmean)
    yn = dev * lax.rsqrt(var + GN_EPS) * lnw_ref[...] + lnb_ref[...]
    bonus = _groups((r * k * rk_ref[...]).astype(BF16), gmean) * float(HEAD_DIM) * v
    o_ref[0] = ((yn + bonus) * g_ref[0].astype(F32)).astype(o_ref.dtype)


def _rwkv_chunk_call(r, lw, k, v, a, b, g, gmean, rk, lnw, lnb, *, n_sub):
    bsz, slen, hd = r.shape
    rows = n_sub * RWKV_CHUNK
    idx = jnp.arange(rows)
    same_chunk = idx[:, None] // RWKV_CHUNK == idx[None, :] // RWKV_CHUNK
    tri = (same_chunk & (idx[:, None] >= idx[None, :])).astype(BF16)
    tok = pl.BlockSpec((1, rows, hd), lambda bi, ci: (bi, ci, 0))
    consts = [tri, gmean, rk, lnw, lnb]
    return pl.pallas_call(
        functools.partial(_rwkv_chunk_body, n_sub=n_sub),
        grid=(bsz, slen // rows),
        in_specs=[tok] * 7 + [_resident(c.shape) for c in consts],
        out_specs=tok,
        out_shape=jax.ShapeDtypeStruct((bsz, slen, hd), BF16),
        scratch_shapes=[pltpu.VMEM((hd // LANES, LANES, LANES), F32)],
        compiler_params=_params("parallel", "arbitrary"),
        name="rwkv_chunk_scan",
    )(r, lw, k, v, a, b, g, *consts)


def _merge_body(h_ref, ya_ref, yb_ref, gate_ref, mod_ref, wa_ref, wb_ref, wo_ref, o_ref):
    d = h_ref.shape[-1]
    ua = _mm(ya_ref[0], wa_ref[...])
    ub = _mm(yb_ref[0], wb_ref[...])
    merged = gate_ref[0, :, 0:d].astype(F32) * ua + gate_ref[0, :, d:2 * d].astype(F32) * ub
    out = _mm(merged.astype(BF16), wo_ref[...])
    o_ref[0] = h_ref[0] + mod_ref[0, 5:6, :] * out


def _merge_call(h, ya, yb, gates, mod, wa, wb, wo, *, tm):
    bsz, slen, d = h.shape

    def tok(width):
        return pl.BlockSpec((1, tm, width), lambda b, i: (b, i, 0))

    return pl.pallas_call(
        _merge_body,
        grid=(bsz, slen // tm),
        in_specs=[
            tok(d), tok(ya.shape[-1]), tok(yb.shape[-1]), tok(2 * d),
            pl.BlockSpec((1, N_MOD, d), lambda b, i: (b, 0, 0)),
            _resident(wa.shape), _resident(wb.shape), _resident(wo.shape),
        ],
        out_specs=tok(d),
        out_shape=jax.ShapeDtypeStruct(h.shape, F32),
        compiler_params=_params("parallel", "parallel"),
        name="merge_out",
    )(h, ya, yb, gates, mod, wa, wb, wo)


def _block_diag_const(width, group, value, dtype):
    idx = jnp.arange(width) // group
    return jnp.where(idx[:, None] == idx[None, :], value, 0.0).astype(dtype)


def kernel(x, c, mod_w, mod_b, norm_g, ffn1_wg, ffn1_wu, ffn1_wd, w_in_first, w_in_rest, shift_mu, decay_w0, decay_w2, aaa_a0, aaa_a2, gate_g2, k_k, k_a, r_k, lnx_w, lnx_b, vres_v0, vres_v2, q_norm_g, k_norm_g, w_up_a, w_up_b, w_out, ffn2_wg, ffn2_wu, ffn2_wd):
    bsz, slen, d = x.shape
    n_layers = mod_w.shape[0]
    hd = decay_w0.shape[-1]
    n_heads = hd // HEAD_DIM
    decay_rank, aaa_rank = decay_w2.shape[1], aaa_a2.shape[1]
    gate_rank, vres_rank = gate_g2.shape[1], vres_v2.shape[1]
    assert decay_rank + aaa_rank == LANES and gate_rank == LANES and vres_rank <= LANES
    assert w_up_b.shape[1] == hd and hd % LANES == 0
    rw = 3 * hd + decay_rank + aaa_rank + gate_rank
    cols_first = rw + 3 * hd + 2 * d
    assert w_in_first.shape[1] == cols_first

    tm = min(512, slen)
    tq_attn = min(512, slen)
    ts_attn = LANES

    gmean = _block_diag_const(MXU_WIDTH, HEAD_DIM, 1.0 / HEAD_DIM, BF16)
    gsum = _block_diag_const(MXU_WIDTH, HEAD_DIM, 1.0, BF16)

    mod_all = _mod_call(c, mod_w, mod_b).reshape(n_layers, bsz, N_MOD, d)

    h = x
    v_first = None
    for l in range(n_layers):
        mod = mod_all[l]
        h = _ffn_call(h, norm_g[l], mod, ffn1_wg[l].astype(BF16), ffn1_wu[l].astype(BF16),
                      ffn1_wd[l].astype(BF16), sub=0, tm=tm, tf=256)

        has_vres = l > 0
        if has_vres:
            w_in = jnp.pad(w_in_rest[l - 1], ((0, 0), (0, LANES - vres_rank)))
        else:
            w_in = w_in_first
        gq = jnp.tile(q_norm_g[l], n_heads)[None, :]
        gk = jnp.tile(k_norm_g[l], n_heads)[None, :]
        proj = _inproj_call(h, norm_g[l], mod, w_in.astype(BF16), gq, gk, gmean,
                            rw=rw, hd=hd, has_vres=has_vres, tm=tm)
        p_rwkv, q, k, v, gates = proj[:5]
        tail = proj[5] if has_vres else None

        lora_w = jnp.zeros((LANES, 2 * hd), F32)
        lora_w = lora_w.at[:decay_rank, :hd].set(decay_w2[l]).at[decay_rank:, hd:].set(aaa_a2[l])
        if has_vres:
            v0 = vres_v0[l - 1][None, :]
            v2 = jnp.pad(vres_v2[l - 1], ((0, LANES - vres_rank), (0, 0))).astype(BF16)
        else:
            v0 = v2 = None
        r_, lw_, k_, v_, a_, b_, g_ = _rwkv_prep_call(
            p_rwkv, tail, v_first, shift_mu[l][None, :], decay_w0[l][None, :], lora_w,
            aaa_a0[l][None, :], gate_g2[l].astype(BF16), k_k[l][None, :], k_a[l][None, :], gsum,
            v0, v2, hd=hd, ts=tm)
        if not has_vres:
            v_first = v_
        y_a = _rwkv_chunk_call(r_, lw_, k_, v_, a_, b_, g_, gmean, r_k[l].reshape(1, hd),
                               lnx_w[l][None, :], lnx_b[l][None, :], n_sub=RWKV_SUBCHUNKS)

        y_b = _attn_call(q, k, v, tq=tq_attn, ts=ts_attn, n_pairs=hd // LANES)

        h = _merge_call(h, y_a, y_b, gates, mod, w_up_a[l].astype(BF16), w_up_b[l].astype(BF16),
                        w_out[l].astype(BF16), tm=tm)

        h = _ffn_call(h, norm_g[l], mod, ffn2_wg[l].astype(BF16), ffn2_wu[l].astype(BF16),
                      ffn2_wd[l].astype(BF16), sub=2, tm=tm, tf=256)
    return h
```

```python
import functools
import math

import jax
import jax.numpy as jnp
from jax import lax
from jax.experimental import pallas as pl
from jax.experimental.pallas import tpu as pltpu

F32 = jnp.float32
BF16 = jnp.bfloat16

HEAD_DIM = 64
RMS_EPS = 1e-6
GN_EPS = 64e-5
N_MOD = 9
LANES = 128
MXU_WIDTH = 256
VMEM_LIMIT = 56 * 1024 * 1024
RWKV_CHUNK = 64
RWKV_SUBCHUNKS = 4

_NN = (((1,), (0,)), ((), ()))
_NT = (((1,), (1,)), ((), ()))


def _mm(a, b, dims=_NN):
    return lax.dot_general(a, b, dims, preferred_element_type=F32)


def _split2(x):
    hi = x.astype(BF16)
    lo = (x - hi.astype(F32)).astype(BF16)
    return hi, lo


def _split3(x):
    hi = x.astype(BF16)
    r1 = x - hi.astype(F32)
    mid = r1.astype(BF16)
    lo = (r1 - mid.astype(F32)).astype(BF16)
    return hi, mid, lo


def _mm3(a, b, dims=_NN):
    ah, al = _split2(a)
    bh, bl = _split2(b)
    return _mm(ah, bh, dims) + (_mm(ah, bl, dims) + _mm(al, bh, dims))


def _mm_lhs_exact(a_bf16, b):
    hi, mid, lo = _split3(b)
    return _mm(a_bf16, hi) + (_mm(a_bf16, mid) + _mm(a_bf16, lo))


def _groups(x_bf16, g):
    n = g.shape[0]
    slabs = [_mm(x_bf16[:, i * n:(i + 1) * n], g) for i in range(x_bf16.shape[1] // n)]
    return jnp.concatenate(slabs, axis=1)


def _params(*sem):
    return pltpu.CompilerParams(dimension_semantics=sem, vmem_limit_bytes=VMEM_LIMIT)


def _resident(shape):
    zeros = (0,) * len(shape)
    return pl.BlockSpec(shape, lambda *_: zeros, pipeline_mode=pl.Buffered(1))


def _norm_mod(x, g, sh, sc):
    ms = jnp.mean(x * x, axis=-1, keepdims=True)
    y = x * lax.rsqrt(ms + RMS_EPS) * g
    return y * (1.0 + sc) + sh


LOG2E = 1.4426950408889634
ATTN_DEAD_LOG2 = -160.0


def _log2(x):
    return jnp.log(x) * LOG2E


def _softplus(x):
    return jnp.maximum(x, 0.0) + jnp.log(1.0 + jnp.exp(-jnp.abs(x)))


def _mod_body(c_ref, w_ref, b_ref, o_ref):
    c = c_ref[...]
    ca = (c * jax.nn.sigmoid(c)).astype(BF16)
    o_ref[0] = _mm(ca, w_ref[0].astype(BF16)) + b_ref[0]


def _mod_call(c, mod_w, mod_b):
    n_layers, d, nd = mod_w.shape
    bsz = c.shape[0]
    tn = d
    return pl.pallas_call(
        _mod_body,
        grid=(n_layers, nd // tn),
        in_specs=[
            pl.BlockSpec((bsz, d), lambda l, j: (0, 0)),
            pl.BlockSpec((1, d, tn), lambda l, j: (l, 0, j)),
            pl.BlockSpec((1, 1, tn), lambda l, j: (l, 0, j)),
        ],
        out_specs=pl.BlockSpec((1, bsz, tn), lambda l, j: (l, 0, j)),
        out_shape=jax.ShapeDtypeStruct((n_layers, bsz, nd), F32),
        compiler_params=_params("parallel", "parallel"),
        name="adaln_mod",
    )(c, mod_w, mod_b.reshape(n_layers, 1, nd))


def _ffn_body(h_ref, g_ref, mod_ref, wg_ref, wu_ref, wd_ref, o_ref, *, sub, tf):
    x = h_ref[0]
    g = g_ref[sub:sub + 1, :]
    sh = mod_ref[0, 3 * sub:3 * sub + 1, :]
    sc = mod_ref[0, 3 * sub + 1:3 * sub + 2, :]
    gt = mod_ref[0, 3 * sub + 2:3 * sub + 3, :]
    n = _norm_mod(x, g, sh, sc).astype(BF16)
    d_ff = wg_ref.shape[1]
    acc = jnp.zeros_like(x)
    for j in range(d_ff // tf):
        gj = _mm(n, wg_ref[:, j * tf:(j + 1) * tf])
        uj = _mm(n, wu_ref[:, j * tf:(j + 1) * tf])
        a = (gj * jax.nn.sigmoid(gj) * uj).astype(BF16)
        acc = acc + _mm(a, wd_ref[j * tf:(j + 1) * tf, :])
    o_ref[0] = x + (0.5 * gt) * acc


def _ffn_call(h, norm_g, mod, wg, wu, wd, *, sub, tm, tf):
    bsz, slen, d = h.shape
    d_ff = wg.shape[1]
    tok = pl.BlockSpec((1, tm, d), lambda b, i: (b, i, 0))
    return pl.pallas_call(
        functools.partial(_ffn_body, sub=sub, tf=tf),
        grid=(bsz, slen // tm),
        in_specs=[
            tok,
            _resident(norm_g.shape),
            pl.BlockSpec((1, N_MOD, d), lambda b, i: (b, 0, 0)),
            _resident((d, d_ff)), _resident((d, d_ff)), _resident((d_ff, d)),
        ],
        out_specs=tok,
        out_shape=jax.ShapeDtypeStruct(h.shape, F32),
        compiler_params=_params("parallel", "parallel"),
        name="ffn",
    )(h, norm_g, mod, wg, wu, wd)


def _head_rms(t, g, gmean):
    ms = _groups((t * t).astype(BF16), gmean)
    return t * lax.rsqrt(ms + RMS_EPS) * g


def _inproj_body(h_ref, g_ref, mod_ref, w_ref, gq_ref, gk_ref, gmean_ref, *outs,
                 rw, hd, d, has_vres):
    p_ref, q_ref, k_ref, v_ref, gate_ref = outs[:5]
    x = h_ref[0]
    n = _norm_mod(x, g_ref[1:2, :], mod_ref[0, 3:4, :], mod_ref[0, 4:5, :]).astype(BF16)
    gmean = gmean_ref[...]
    p_ref[0] = _mm(n, w_ref[:, 0:rw])
    q = _mm(n, w_ref[:, rw:rw + hd])
    q_ref[0] = (_head_rms(q, gq_ref[...], gmean) * (LOG2E / math.sqrt(HEAD_DIM))).astype(BF16)
    k = _mm(n, w_ref[:, rw + hd:rw + 2 * hd])
    k_ref[0] = _head_rms(k, gk_ref[...], gmean).astype(BF16)
    v_ref[0] = _mm(n, w_ref[:, rw + 2 * hd:rw + 3 * hd]).astype(BF16)
    og = rw + 3 * hd
    gate_ref[0] = jax.nn.sigmoid(_mm(n, w_ref[:, og:og + 2 * d])).astype(BF16)
    if has_vres:
        outs[5][0] = _mm(n, w_ref[:, og + 2 * d:og + 2 * d + LANES])


def _inproj_call(h, norm_g, mod, w, gq, gk, gmean, *, rw, hd, has_vres, tm):
    bsz, slen, d = h.shape

    def tok(width):
        return pl.BlockSpec((1, tm, width), lambda b, i: (b, i, 0))

    widths = [(rw, F32), (hd, BF16), (hd, BF16), (hd, BF16), (2 * d, BF16)]
    if has_vres:
        widths.append((LANES, F32))
    return pl.pallas_call(
        functools.partial(_inproj_body, rw=rw, hd=hd, d=d, has_vres=has_vres),
        grid=(bsz, slen // tm),
        in_specs=[
            tok(d),
            _resident(norm_g.shape),
            pl.BlockSpec((1, N_MOD, d), lambda b, i: (b, 0, 0)),
            _resident(w.shape), _resident(gq.shape), _resident(gk.shape), _resident(gmean.shape),
        ],
        out_specs=[tok(wd) for wd, _ in widths],
        out_shape=[jax.ShapeDtypeStruct((bsz, slen, wd), dt) for wd, dt in widths],
        compiler_params=_params("parallel", "parallel"),
        name="in_proj",
    )(h, norm_g, mod, w, gq, gk, gmean)


def _attn_body(q_ref, k_ref, v_ref, o_ref, *, tq, ts, n_pairs):
    tk = LANES
    n_strips = tq // ts
    pad = tq - ts
    qi = pl.program_id(2)
    q0 = qi * tq
    lane = lax.broadcasted_iota(jnp.int32, (1, LANES), 1)
    first = lane < HEAD_DIM
    row = lax.broadcasted_iota(jnp.int32, (ts, 2 * tk), 0)
    col = lax.broadcasted_iota(jnp.int32, (ts, 2 * tk), 1)
    col = jnp.where(col >= tk, col - tk, col)
    jj = lax.broadcasted_iota(jnp.int32, (2 * tk, 2 * tk), 0)
    ss = lax.broadcasted_iota(jnp.int32, (2 * tk, 2 * tk), 1)
    jj = jnp.where(jj >= tk, jj - tk, jj)
    suffix_total = jnp.where((ss >= tk) | (jj > ss), -1.0, 0.0).astype(BF16)

    def both_heads(x):
        zero = jnp.zeros_like(x)
        return jnp.concatenate([jnp.where(first, x, zero), jnp.where(first, zero, x)], axis=0)

    def strips(x, r0):
        return jnp.concatenate([x[s * ts + r0:(s + 1) * ts] for s in range(n_strips)], axis=0)

    def unstrips(old, part, r0):
        n = ts - r0
        pieces = []
        for s in range(n_strips):
            if r0:
                pieces.append(old[s * ts:s * ts + r0])
            pieces.append(part[s * n:(s + 1) * n])
        return jnp.concatenate(pieces, axis=0)

    def step(key_start, carry, diag_off):
        r0 = 0 if diag_off is None else diag_off
        n = ts - r0
        new = []
        for p in range(n_pairs):
            pair = slice(p * LANES, (p + 1) * LANES)
            acc_all, logsurv_all = carry[2 * p], carry[2 * p + 1]
            acc, logsurv = strips(acc_all, r0), strips(logsurv_all, r0)
            zs, v2 = [], []
            for s in range(n_strips):
                k0 = pl.multiple_of(key_start + s * ts, tk)
                k2 = both_heads(k_ref[0, pl.ds(k0, tk), pair])
                v2.append(both_heads(v_ref[0, pl.ds(k0, tk), pair]))
                zs.append(_mm(q_ref[0, s * ts + r0:(s + 1) * ts, pair], k2, _NT))
            z = jnp.concatenate(zs, axis=0)
            neg_abs = lax.bitcast_convert_type(
                lax.bitcast_convert_type(z, jnp.uint32) | jnp.uint32(0x80000000), F32)
            sp = jnp.maximum(z, 0.0) + _log2(1.0 + jnp.exp2(neg_abs))
            if diag_off is not None:
                causal = jnp.concatenate([(col + diag_off < row)[r0:]] * n_strips, axis=0)
                sp = jnp.where(causal, sp, 0.0)
            hi, lo = _split2(sp)
            st = [_mm(jnp.concatenate([hi[:, e * tk:(e + 1) * tk], lo[:, e * tk:(e + 1) * tk]],
                                      axis=1), suffix_total) for e in range(2)]
            suffix = jnp.concatenate([st[0][:, :tk], st[1][:, :tk]], axis=1)
            total = jnp.concatenate([st[0][:, tk:], st[1][:, tk:]], axis=1)
            w = jnp.exp2((z - sp) + logsurv + suffix)
            if diag_off is not None:
                w = jnp.where(causal, w, 0.0)
            w = w.astype(BF16)
            acc = acc + jnp.concatenate(
                [_mm(w[s * n:(s + 1) * n], v2[s]) for s in range(n_strips)], axis=0)
            new += [unstrips(acc_all, acc, r0), unstrips(logsurv_all, logsurv + total, r0)]
        return tuple(new)

    carry = (jnp.zeros((tq, LANES), F32), jnp.zeros((tq, 2 * tk), F32)) * n_pairs
    for j in reversed(range(ts // tk)):
        carry = step(q0 + pad + j * tk, carry, j * tk)

    def alive(carry):
        worst = carry[1]
        for p in range(1, n_pairs):
            worst = jnp.maximum(worst, carry[2 * p + 1])
        return (jnp.max(worst) > ATTN_DEAD_LOG2).astype(jnp.int32)

    n_off = (q0 + pad) // tk

    def cond(state):
        return (state[0] < n_off) & (state[1] > 0)

    def body(state):
        i = state[0]
        carry = step(q0 + pad - (i + 1) * tk, state[2:], None)
        return (i + 1, alive(carry)) + carry

    carry = lax.while_loop(cond, body, (jnp.int32(0), alive(carry)) + carry)[2:]
    for p in range(n_pairs):
        o_ref[0, :, p * LANES:(p + 1) * LANES] = carry[2 * p].astype(o_ref.dtype)


def _attn_call(q, k, v, *, tq, ts, n_pairs):
    bsz, slen, hd = q.shape
    width = n_pairs * LANES
    pad = tq - ts
    k, v = (jnp.pad(t, ((0, 0), (pad, 0), (0, 0))) for t in (k, v))
    qblk = pl.BlockSpec((1, tq, width), lambda b, p, i: (b, i, p))
    kvblk = pl.BlockSpec((1, slen + pad, width), lambda b, p, i: (b, 0, p))
    return pl.pallas_call(
        functools.partial(_attn_body, tq=tq, ts=ts, n_pairs=n_pairs),
        grid=(bsz, hd // width, slen // tq),
        in_specs=[qblk, kvblk, kvblk],
        out_specs=qblk,
        out_shape=jax.ShapeDtypeStruct(q.shape, BF16),
        compiler_params=_params("parallel", "parallel", "arbitrary"),
        name="stick_breaking_attn",
    )(q, k, v)


def _rwkv_prep_body(*refs, hd, has_vres):
    if has_vres:
        (p_ref, prev_ref, tail_ref, vf_ref, mu_ref, w0_ref, lora_ref, a0_ref, g2_ref, kk_ref,
         ka_ref, gsum_ref, v0_ref, v2_ref, r_o, lw_o, k_o, v_o, a_o, b_o, g_o) = refs
    else:
        (p_ref, prev_ref, mu_ref, w0_ref, lora_ref, a0_ref, g2_ref, kk_ref,
         ka_ref, gsum_ref, r_o, lw_o, k_o, v_o, a_o, b_o, g_o) = refs
    i = pl.program_id(1)
    p = p_ref[0]
    ts = p.shape[0]
    prev = prev_ref[0][7:8, :]
    prev = jnp.where(i == 0, jnp.zeros_like(prev), prev)
    shifted = pltpu.roll(p, 1, axis=0)
    rows = lax.broadcasted_iota(jnp.int32, (ts, 1), 0)
    shifted = jnp.where(rows == 0, prev, shifted)
    p = p + (shifted - p) * mu_ref[...]
    r = p[:, 0:hd]
    k = p[:, hd:2 * hd]
    v = p[:, 2 * hd:3 * hd]
    wa = p[:, 3 * hd:3 * hd + LANES]
    gl = p[:, 3 * hd + LANES:3 * hd + 2 * LANES]
    lane = lax.broadcasted_iota(jnp.int32, (1, LANES), 1)
    wa = jnp.where(lane < wa.shape[1] // 2, jnp.tanh(wa), wa)
    lora = _mm3(wa, lora_ref[...])
    w = -_softplus(-(w0_ref[...] + lora[:, 0:hd])) - 0.5
    lw_o[0] = -jnp.exp(w)
    a = jax.nn.sigmoid(a0_ref[...] + lora[:, hd:2 * hd])
    g_o[0] = _mm(jax.nn.sigmoid(gl).astype(BF16), g2_ref[...]).astype(g_o.dtype)
    if has_vres:
        mix = jax.nn.sigmoid(v0_ref[...] + _mm(tail_ref[0].astype(BF16), v2_ref[...]))
        v = v + (vf_ref[0].astype(F32) - v) * mix
    kk = k * kk_ref[...]
    sq_hi, sq_lo = _split2(kk * kk)
    ss = _groups(sq_hi, gsum_ref[...]) + _groups(sq_lo, gsum_ref[...])
    kk = kk / jnp.maximum(jnp.sqrt(ss), 1e-12)
    r_o[0] = r.astype(r_o.dtype)
    k_o[0] = (k * (1.0 + (a - 1.0) * ka_ref[...])).astype(k_o.dtype)
    v_o[0] = v.astype(v_o.dtype)
    a_o[0] = (-kk).astype(a_o.dtype)
    b_o[0] = (kk * a).astype(b_o.dtype)


def _rwkv_prep_call(p, tail, v_first, mu, w0, lora_w, a0, g2, k_k, k_a, gsum, v0, v2, *, hd, ts):
    bsz, slen, rw = p.shape
    has_vres = tail is not None

    def tok(width):
        return pl.BlockSpec((1, ts, width), lambda b, i: (b, i, 0))

    prev = pl.BlockSpec((1, 8, rw), lambda b, i: (b, jnp.maximum(i * (ts // 8) - 1, 0), 0))
    args = [p, p]
    specs = [tok(rw), prev]
    if has_vres:
        args += [tail, v_first]
        specs += [tok(LANES), tok(hd)]
    consts = [mu, w0, lora_w, a0, g2, k_k, k_a, gsum]
    if has_vres:
        consts += [v0, v2]
    args += consts
    specs += [_resident(c.shape) for c in consts]
    return pl.pallas_call(
        functools.partial(_rwkv_prep_body, hd=hd, has_vres=has_vres),
        grid=(bsz, slen // ts),
        in_specs=specs,
        out_specs=[tok(hd)] * 7,
        out_shape=[jax.ShapeDtypeStruct((bsz, slen, hd), F32 if i == 1 else BF16) for i in range(7)],
        compiler_params=_params("parallel", "parallel"),
        name="rwkv_prep",
    )(*args)


def _rwkv_chunk_body(r_ref, lw_ref, k_ref, v_ref, a_ref, b_ref, g_ref, tri_ref, gmean_ref,
                     rk_ref, lnw_ref, lnb_ref, o_ref, st_ref, *, n_sub):
    c = pl.program_id(1)

    @pl.when(c == 0)
    def _():
        st_ref[...] = jnp.zeros_like(st_ref)

    r, lw, k, v = r_ref[0].astype(F32), lw_ref[0], k_ref[0].astype(F32), v_ref[0].astype(F32)
    a, b = a_ref[0].astype(F32), b_ref[0].astype(F32)
    ch = RWKV_CHUNK
    n_pairs = r.shape[1] // LANES
    cum = _mm_lhs_exact(tri_ref[...], lw)
    cum_prev = cum - lw
    lasts = [cum[(s + 1) * ch - 1:(s + 1) * ch, :] for s in range(n_sub)]
    last = jnp.concatenate([jnp.broadcast_to(t, (ch, t.shape[1])) for t in lasts], axis=0)
    e_pos = jnp.exp(cum)
    e_neg = jnp.exp(-cum)
    e_end = jnp.exp(last - cum)
    at = a * jnp.exp(cum_prev)
    rt = r * e_pos
    bt = b * e_neg
    kt = k * e_neg
    bh = b * e_end
    kh = k * e_end
    p_end = [jnp.exp(t) for t in lasts]

    lane = lax.broadcasted_iota(jnp.int32, (1, LANES), 1)
    first = lane < HEAD_DIM

    def stacked(x, s, p):
        x = x[s * ch:(s + 1) * ch, p * LANES:(p + 1) * LANES]
        zero = jnp.zeros_like(x)
        return jnp.concatenate([jnp.where(first, x, zero), jnp.where(first, zero, x)], axis=0)

    row = lax.broadcasted_iota(jnp.int32, (2 * ch, 2 * ch), 0)
    col = lax.broadcasted_iota(jnp.int32, (2 * ch, 2 * ch), 1)
    strict = (row & (ch - 1)) > (col & (ch - 1))
    row2 = lax.broadcasted_iota(jnp.int32, (2 * ch, 4 * ch), 0)
    col2 = lax.broadcasted_iota(jnp.int32, (2 * ch, 4 * ch), 1)
    lower2 = (row2 & (ch - 1)) >= (col2 & (ch - 1))
    diag = row == col
    eye = jnp.where(diag, 1.0, 0.0)
    n_levels = int(math.log2(ch))

    def sub_diag(level):
        size, half = 1 << level, 1 << (level - 1)
        inside = (row >> level) == (col >> level)
        return inside & ((row & (size - 1)) >= half) & ((col & (size - 1)) < half)

    units = [(s, p) for s in range(n_sub) for p in range(n_pairs)]
    at_s, rt_s, bt_s, kt_s, bh_s, kh_s, v_s = (
        [stacked(t, s, p) for s, p in units] for t in (at, rt, bt, kt, bh, kh, v))
    idx = range(len(units))
    prod = [_mm(jnp.concatenate([at_s[u], rt_s[u]], axis=0).astype(BF16),
                jnp.concatenate([bt_s[u], kt_s[u]], axis=0).astype(BF16), _NT) for u in idx]
    a_ab = [jnp.where(strict, prod[u][:2 * ch, :2 * ch], 0.0) for u in idx]
    a_ak = [jnp.where(strict, prod[u][:2 * ch, 2 * ch:], 0.0).astype(BF16) for u in idx]
    a_rbk = [jnp.where(lower2, prod[u][2 * ch:, :], 0.0).astype(BF16) for u in idx]
    lvl = sub_diag(1)
    tinv = [eye + jnp.where(lvl, a_ab[u], 0.0) for u in idx]
    for level in range(2, n_levels + 1):
        lvl = sub_diag(level)
        tinv_b = [tinv[u].astype(BF16) for u in idx]
        right = [_mm(jnp.where(lvl, a_ab[u], 0.0).astype(BF16), tinv_b[u]) for u in idx]
        tinv = [tinv[u] + _mm(tinv_b[u], right[u].astype(BF16)) for u in idx]
    v_b = [v_s[u].astype(BF16) for u in idx]
    akv = [_mm(a_ak[u], v_b[u]) for u in idx]
    wx = [_mm(tinv[u].astype(BF16),
              jnp.concatenate([at_s[u], akv[u]], axis=1).astype(BF16)) for u in idx]
    zero_b = jnp.zeros((2 * ch, LANES), BF16)
    wxv = [jnp.concatenate([wx[u].astype(BF16), jnp.concatenate([zero_b, v_b[u]], axis=1)], axis=0)
           for u in idx]
    qy = [_mm(a_rbk[u], wxv[u]) for u in idx]
    mn = [_mm(jnp.concatenate([bh_s[u], kh_s[u]], axis=0).T.astype(BF16), wxv[u]) for u in idx]
    ys = []
    for s in range(n_sub):
        rows = []
        for p in range(n_pairs):
            u = s * n_pairs + p
            st = st_ref[p]
            y_st = _mm((rt_s[u] + qy[u][:, :LANES]).astype(BF16), st.astype(BF16)) + qy[u][:, LANES:]
            rows.append(y_st[:ch] + y_st[ch:])
            m_t = jnp.where(diag, p_end[s][:, p * LANES:(p + 1) * LANES], 0.0) + mn[u][:, :LANES]
            st_ref[p] = _mm3(m_t, st) + mn[u][:, LANES:]
        ys.append(jnp.concatenate(rows, axis=1))
    y = jnp.concatenate(ys, axis=0)

    gmean = gmean_ref[...]
    y_hi, y_lo = _split2(y)
    mean = _groups(y_hi, gmean) + _groups(y_lo, gmean)
    dev = y - mean
    var = _groups((dev * dev).astype(BF16), gmean)
    yn = dev * lax.rsqrt(var + GN_EPS) * lnw_ref[...] + lnb_ref[...]
    bonus = _groups((r * k * rk_ref[...]).astype(BF16), gmean) * float(HEAD_DIM) * v
    o_ref[0] = ((yn + bonus) * g_ref[0].astype(F32)).astype(o_ref.dtype)


def _rwkv_chunk_call(r, lw, k, v, a, b, g, gmean, rk, lnw, lnb, *, n_sub):
    bsz, slen, hd = r.shape
    rows = n_sub * RWKV_CHUNK
    idx = jnp.arange(rows)
    same_chunk = idx[:, None] // RWKV_CHUNK == idx[None, :] // RWKV_CHUNK
    tri = (same_chunk & (idx[:, None] >= idx[None, :])).astype(BF16)
    tok = pl.BlockSpec((1, rows, hd), lambda bi, ci: (bi, ci, 0))
    consts = [tri, gmean, rk, lnw, lnb]
    return pl.pallas_call(
        functools.partial(_rwkv_chunk_body, n_sub=n_sub),
        grid=(bsz, slen // rows),
        in_specs=[tok] * 7 + [_resident(c.shape) for c in consts],
        out_specs=tok,
        out_shape=jax.ShapeDtypeStruct((bsz, slen, hd), BF16),
        scratch_shapes=[pltpu.VMEM((hd // LANES, LANES, LANES), F32)],
        compiler_params=_params("parallel", "arbitrary"),
        name="rwkv_chunk_scan",
    )(r, lw, k, v, a, b, g, *consts)


def _merge_body(h_ref, ya_ref, yb_ref, gate_ref, mod_ref, wa_ref, wb_ref, wo_ref, o_ref):
    d = h_ref.shape[-1]
    ua = _mm(ya_ref[0], wa_ref[...])
    ub = _mm(yb_ref[0], wb_ref[...])
    merged = gate_ref[0, :, 0:d].astype(F32) * ua + gate_ref[0, :, d:2 * d].astype(F32) * ub
    out = _mm(merged.astype(BF16), wo_ref[...])
    o_ref[0] = h_ref[0] + mod_ref[0, 5:6, :] * out


def _merge_call(h, ya, yb, gates, mod, wa, wb, wo, *, tm):
    bsz, slen, d = h.shape

    def tok(width):
        return pl.BlockSpec((1, tm, width), lambda b, i: (b, i, 0))

    return pl.pallas_call(
        _merge_body,
        grid=(bsz, slen // tm),
        in_specs=[
            tok(d), tok(ya.shape[-1]), tok(yb.shape[-1]), tok(2 * d),
            pl.BlockSpec((1, N_MOD, d), lambda b, i: (b, 0, 0)),
            _resident(wa.shape), _resident(wb.shape), _resident(wo.shape),
        ],
        out_specs=tok(d),
        out_shape=jax.ShapeDtypeStruct(h.shape, F32),
        compiler_params=_params("parallel", "parallel"),
        name="merge_out",
    )(h, ya, yb, gates, mod, wa, wb, wo)


def _block_diag_const(width, group, value, dtype):
    idx = jnp.arange(width) // group
    return jnp.where(idx[:, None] == idx[None, :], value, 0.0).astype(dtype)


def kernel(x, c, mod_w, mod_b, norm_g, ffn1_wg, ffn1_wu, ffn1_wd, w_in_first, w_in_rest, shift_mu, decay_w0, decay_w2, aaa_a0, aaa_a2, gate_g2, k_k, k_a, r_k, lnx_w, lnx_b, vres_v0, vres_v2, q_norm_g, k_norm_g, w_up_a, w_up_b, w_out, ffn2_wg, ffn2_wu, ffn2_wd):
    bsz, slen, d = x.shape
    n_layers = mod_w.shape[0]
    hd = decay_w0.shape[-1]
    n_heads = hd // HEAD_DIM
    decay_rank, aaa_rank = decay_w2.shape[1], aaa_a2.shape[1]
    gate_rank, vres_rank = gate_g2.shape[1], vres_v2.shape[1]
    assert decay_rank + aaa_rank == LANES and gate_rank == LANES and vres_rank <= LANES
    assert w_up_b.shape[1] == hd and hd % LANES == 0
    rw = 3 * hd + decay_rank + aaa_rank + gate_rank
    cols_first = rw + 3 * hd + 2 * d
    assert w_in_first.shape[1] == cols_first

    tm = min(512, slen)
    tq_attn = min(1024, slen)
    ts_attn = tq_attn
    attn_pairs = 2

    gmean = _block_diag_const(MXU_WIDTH, HEAD_DIM, 1.0 / HEAD_DIM, BF16)
    gsum = _block_diag_const(MXU_WIDTH, HEAD_DIM, 1.0, BF16)

    mod_all = _mod_call(c, mod_w, mod_b).reshape(n_layers, bsz, N_MOD, d)

    h = x
    v_first = None
    for l in range(n_layers):
        mod = mod_all[l]
        h = _ffn_call(h, norm_g[l], mod, ffn1_wg[l].astype(BF16), ffn1_wu[l].astype(BF16),
                      ffn1_wd[l].astype(BF16), sub=0, tm=tm, tf=256)

        has_vres = l > 0
        if has_vres:
            w_in = jnp.pad(w_in_rest[l - 1], ((0, 0), (0, LANES - vres_rank)))
        else:
            w_in = w_in_first
        gq = jnp.tile(q_norm_g[l], n_heads)[None, :]
        gk = jnp.tile(k_norm_g[l], n_heads)[None, :]
        proj = _inproj_call(h, norm_g[l], mod, w_in.astype(BF16), gq, gk, gmean,
                            rw=rw, hd=hd, has_vres=has_vres, tm=tm)
        p_rwkv, q, k, v, gates = proj[:5]
        tail = proj[5] if has_vres else None

        lora_w = jnp.zeros((LANES, 2 * hd), F32)
        lora_w = lora_w.at[:decay_rank, :hd].set(decay_w2[l]).at[decay_rank:, hd:].set(aaa_a2[l])
        if has_vres:
            v0 = vres_v0[l - 1][None, :]
            v2 = jnp.pad(vres_v2[l - 1], ((0, LANES - vres_rank), (0, 0))).astype(BF16)
        else:
            v0 = v2 = None
        r_, lw_, k_, v_, a_, b_, g_ = _rwkv_prep_call(
            p_rwkv, tail, v_first, shift_mu[l][None, :], decay_w0[l][None, :], lora_w,
            aaa_a0[l][None, :], gate_g2[l].astype(BF16), k_k[l][None, :], k_a[l][None, :], gsum,
            v0, v2, hd=hd, ts=tm)
        if not has_vres:
            v_first = v_
        y_a = _rwkv_chunk_call(r_, lw_, k_, v_, a_, b_, g_, gmean, r_k[l].reshape(1, hd),
                               lnx_w[l][None, :], lnx_b[l][None, :], n_sub=RWKV_SUBCHUNKS)

        y_b = _attn_call(q, k, v, tq=tq_attn, ts=ts_attn, n_pairs=attn_pairs)

        h = _merge_call(h, y_a, y_b, gates, mod, w_up_a[l].astype(BF16), w_up_b[l].astype(BF16),
                        w_out[l].astype(BF16), tm=tm)

        h = _ffn_call(h, norm_g[l], mod, ffn2_wg[l].astype(BF16), ffn2_wu[l].astype(BF16),
                      ffn2_wd[l].astype(BF16), sub=2, tm=tm, tf=256)
    return h
```

```python
import functools
import math

import jax
import jax.numpy as jnp
from jax import lax
from jax.experimental import pallas as pl
from jax.experimental.pallas import tpu as pltpu

F32 = jnp.float32
BF16 = jnp.bfloat16

HEAD_DIM = 64
RMS_EPS = 1e-6
GN_EPS = 64e-5
N_MOD = 9
LANES = 128
MXU_WIDTH = 256
VMEM_LIMIT = 56 * 1024 * 1024
RWKV_CHUNK = 64
RWKV_SUBCHUNKS = 4

_NN = (((1,), (0,)), ((), ()))
_NT = (((1,), (1,)), ((), ()))


def _mm(a, b, dims=_NN):
    return lax.dot_general(a, b, dims, preferred_element_type=F32)


def _split2(x):
    hi = x.astype(BF16)
    lo = (x - hi.astype(F32)).astype(BF16)
    return hi, lo


def _split3(x):
    hi = x.astype(BF16)
    r1 = x - hi.astype(F32)
    mid = r1.astype(BF16)
    lo = (r1 - mid.astype(F32)).astype(BF16)
    return hi, mid, lo


def _mm3(a, b, dims=_NN):
    ah, al = _split2(a)
    bh, bl = _split2(b)
    return _mm(ah, bh, dims) + (_mm(ah, bl, dims) + _mm(al, bh, dims))


def _mm_lhs_exact(a_bf16, b):
    hi, mid, lo = _split3(b)
    return _mm(a_bf16, hi) + (_mm(a_bf16, mid) + _mm(a_bf16, lo))


def _groups(x_bf16, g):
    n = g.shape[0]
    slabs = [_mm(x_bf16[:, i * n:(i + 1) * n], g) for i in range(x_bf16.shape[1] // n)]
    return jnp.concatenate(slabs, axis=1)


def _params(*sem):
    return pltpu.CompilerParams(dimension_semantics=sem, vmem_limit_bytes=VMEM_LIMIT)


def _resident(shape):
    zeros = (0,) * len(shape)
    return pl.BlockSpec(shape, lambda *_: zeros, pipeline_mode=pl.Buffered(1))


def _norm_mod(x, g, sh, sc):
    ms = jnp.mean(x * x, axis=-1, keepdims=True)
    y = x * lax.rsqrt(ms + RMS_EPS) * g
    return y * (1.0 + sc) + sh


LOG2E = 1.4426950408889634
ATTN_DEAD_LOG2 = -160.0


def _log2(x):
    return jnp.log(x) * LOG2E


def _softplus(x):
    return jnp.maximum(x, 0.0) + jnp.log(1.0 + jnp.exp(-jnp.abs(x)))


def _mod_body(c_ref, w_ref, b_ref, o_ref):
    c = c_ref[...]
    ca = (c * jax.nn.sigmoid(c)).astype(BF16)
    o_ref[0] = _mm(ca, w_ref[0].astype(BF16)) + b_ref[0]


def _mod_call(c, mod_w, mod_b):
    n_layers, d, nd = mod_w.shape
    bsz = c.shape[0]
    tn = d
    return pl.pallas_call(
        _mod_body,
        grid=(n_layers, nd // tn),
        in_specs=[
            pl.BlockSpec((bsz, d), lambda l, j: (0, 0)),
            pl.BlockSpec((1, d, tn), lambda l, j: (l, 0, j)),
            pl.BlockSpec((1, 1, tn), lambda l, j: (l, 0, j)),
        ],
        out_specs=pl.BlockSpec((1, bsz, tn), lambda l, j: (l, 0, j)),
        out_shape=jax.ShapeDtypeStruct((n_layers, bsz, nd), F32),
        compiler_params=_params("parallel", "parallel"),
        name="adaln_mod",
    )(c, mod_w, mod_b.reshape(n_layers, 1, nd))


def _ffn_body(h_ref, g_ref, mod_ref, wg_ref, wu_ref, wd_ref, o_ref, *, sub, tf):
    x = h_ref[0]
    g = g_ref[sub:sub + 1, :]
    sh = mod_ref[0, 3 * sub:3 * sub + 1, :]
    sc = mod_ref[0, 3 * sub + 1:3 * sub + 2, :]
    gt = mod_ref[0, 3 * sub + 2:3 * sub + 3, :]
    n = _norm_mod(x, g, sh, sc).astype(BF16)
    d_ff = wg_ref.shape[1]
    acc = jnp.zeros_like(x)
    for j in range(d_ff // tf):
        gj = _mm(n, wg_ref[:, j * tf:(j + 1) * tf])
        uj = _mm(n, wu_ref[:, j * tf:(j + 1) * tf])
        a = (gj * jax.nn.sigmoid(gj) * uj).astype(BF16)
        acc = acc + _mm(a, wd_ref[j * tf:(j + 1) * tf, :])
    o_ref[0] = x + (0.5 * gt) * acc


def _ffn_call(h, norm_g, mod, wg, wu, wd, *, sub, tm, tf):
    bsz, slen, d = h.shape
    d_ff = wg.shape[1]
    tok = pl.BlockSpec((1, tm, d), lambda b, i: (b, i, 0))
    return pl.pallas_call(
        functools.partial(_ffn_body, sub=sub, tf=tf),
        grid=(bsz, slen // tm),
        in_specs=[
            tok,
            _resident(norm_g.shape),
            pl.BlockSpec((1, N_MOD, d), lambda b, i: (b, 0, 0)),
            _resident((d, d_ff)), _resident((d, d_ff)), _resident((d_ff, d)),
        ],
        out_specs=tok,
        out_shape=jax.ShapeDtypeStruct(h.shape, F32),
        compiler_params=_params("parallel", "parallel"),
        name="ffn",
    )(h, norm_g, mod, wg, wu, wd)


def _head_rms(t, g, gmean):
    ms = _groups((t * t).astype(BF16), gmean)
    return t * lax.rsqrt(ms + RMS_EPS) * g


def _rwkv_prep(p, prev, tail, v_first, c, *, hd, decay_rank):
    tm = p.shape[0]
    shifted = pltpu.roll(p, 1, axis=0)
    rows = lax.broadcasted_iota(jnp.int32, (tm, 1), 0)
    shifted = jnp.where(rows == 0, prev, shifted)
    p = p + (shifted - p) * c["mu"]
    r = p[:, 0:hd]
    k = p[:, hd:2 * hd]
    v = p[:, 2 * hd:3 * hd]
    wa = p[:, 3 * hd:3 * hd + LANES]
    gl = p[:, 3 * hd + LANES:3 * hd + 2 * LANES]
    lane = lax.broadcasted_iota(jnp.int32, (1, LANES), 1)
    wa = jnp.where(lane < decay_rank, jnp.tanh(wa), wa)
    lora = _mm3(wa, c["lora"])
    w = -_softplus(-(c["w0"] + lora[:, 0:hd])) - 0.5
    log_decay = -jnp.exp(w)
    a = jax.nn.sigmoid(c["a0"] + lora[:, hd:2 * hd])
    g = _mm(jax.nn.sigmoid(gl).astype(BF16), c["g2"])
    if tail is not None:
        mix = jax.nn.sigmoid(c["v0"] + _mm(tail.astype(BF16), c["v2"]))
        v = v + (v_first - v) * mix
    kk = k * c["k_k"]
    sq_hi, sq_lo = _split2(kk * kk)
    ss = _groups(sq_hi, c["gsum"]) + _groups(sq_lo, c["gsum"])
    kk = kk / jnp.maximum(jnp.sqrt(ss), 1e-12)
    return r, log_decay, k * (1.0 + (a - 1.0) * c["k_a"]), v, -kk, kk * a, g


_PREP_CONSTS = ("mu", "w0", "lora", "a0", "g2", "k_k", "k_a", "gsum")
_VRES_CONSTS = ("v0", "v2")


def _inproj_body(*refs, rw, hd, d, has_vres, decay_rank):
    names = ("h", "g", "mod", "w", "gq", "gk", "gmean") + (("vf",) if has_vres else ())
    names += _PREP_CONSTS + (_VRES_CONSTS if has_vres else ())
    ins = dict(zip(names, refs))
    q_ref, k_ref, v_ref, gate_ref = refs[len(names):len(names) + 4]
    rwkv_outs = refs[len(names) + 4:len(names) + 11]
    prev_ref = refs[len(names) + 11]
    i = pl.program_id(1)
    x = ins["h"][0]
    tm = x.shape[0]
    n = _norm_mod(x, ins["g"][1:2, :], ins["mod"][0, 3:4, :], ins["mod"][0, 4:5, :]).astype(BF16)
    w_ref = ins["w"]
    gmean = ins["gmean"][...]
    p = _mm(n, w_ref[:, 0:rw])
    og = rw + 3 * hd
    tail = _mm(n, w_ref[:, og + 2 * d:og + 2 * d + LANES]) if has_vres else None
    q = _mm(n, w_ref[:, rw:rw + hd])
    q_ref[0] = (_head_rms(q, ins["gq"][...], gmean) * (LOG2E / math.sqrt(HEAD_DIM))).astype(BF16)
    k = _mm(n, w_ref[:, rw + hd:rw + 2 * hd])
    k_ref[0] = _head_rms(k, ins["gk"][...], gmean).astype(BF16)
    v_ref[0] = _mm(n, w_ref[:, rw + 2 * hd:rw + 3 * hd]).astype(BF16)
    gate_ref[0] = jax.nn.sigmoid(_mm(n, w_ref[:, og:og + 2 * d])).astype(BF16)

    prev = prev_ref[7:8, :]
    prev = jnp.where(i == 0, jnp.zeros_like(prev), prev)
    prev_ref[...] = p[tm - 8:tm, :]
    consts = {name: ins[name][...] for name in _PREP_CONSTS + (_VRES_CONSTS if has_vres else ())}
    v_first = ins["vf"][0].astype(F32) if has_vres else None
    outs = _rwkv_prep(p, prev, tail, v_first, consts, hd=hd, decay_rank=decay_rank)
    for o_ref, val in zip(rwkv_outs, outs):
        o_ref[0] = val.astype(o_ref.dtype)


def _inproj_call(h, norm_g, mod, w, gq, gk, gmean, v_first, prep_consts, *, rw, hd, decay_rank, tm):
    bsz, slen, d = h.shape
    has_vres = v_first is not None

    def tok(width):
        return pl.BlockSpec((1, tm, width), lambda b, i: (b, i, 0))

    args = [h, norm_g, mod, w, gq, gk, gmean]
    specs = [tok(d), _resident(norm_g.shape), pl.BlockSpec((1, N_MOD, d), lambda b, i: (b, 0, 0)),
             _resident(w.shape), _resident(gq.shape), _resident(gk.shape), _resident(gmean.shape)]
    if has_vres:
        args.append(v_first)
        specs.append(tok(hd))
    names = _PREP_CONSTS + (_VRES_CONSTS if has_vres else ())
    args += [prep_consts[name] for name in names]
    specs += [_resident(prep_consts[name].shape) for name in names]
    outs = [(hd, BF16)] * 3 + [(2 * d, BF16)] + [(hd, F32 if j == 1 else BF16) for j in range(7)]
    return pl.pallas_call(
        functools.partial(_inproj_body, rw=rw, hd=hd, d=d, has_vres=has_vres, decay_rank=decay_rank),
        grid=(bsz, slen // tm),
        in_specs=specs,
        out_specs=[tok(wd) for wd, _ in outs],
        out_shape=[jax.ShapeDtypeStruct((bsz, slen, wd), dt) for wd, dt in outs],
        scratch_shapes=[pltpu.VMEM((8, rw), F32)],
        compiler_params=_params("parallel", "arbitrary"),
        name="in_proj",
    )(*args)


def _attn_body(q_ref, k_ref, v_ref, o_ref, *, tq, ts, n_pairs):
    tk = LANES
    n_strips = tq // ts
    pad = tq - ts
    qi = pl.program_id(2)
    q0 = qi * tq
    lane = lax.broadcasted_iota(jnp.int32, (1, LANES), 1)
    first = lane < HEAD_DIM
    row = lax.broadcasted_iota(jnp.int32, (ts, 2 * tk), 0)
    col = lax.broadcasted_iota(jnp.int32, (ts, 2 * tk), 1)
    col = jnp.where(col >= tk, col - tk, col)
    jj = lax.broadcasted_iota(jnp.int32, (2 * tk, 2 * tk), 0)
    ss = lax.broadcasted_iota(jnp.int32, (2 * tk, 2 * tk), 1)
    jj = jnp.where(jj >= tk, jj - tk, jj)
    suffix_total = jnp.where((ss >= tk) | (jj > ss), -1.0, 0.0).astype(BF16)

    def both_heads(x):
        zero = jnp.zeros_like(x)
        return jnp.concatenate([jnp.where(first, x, zero), jnp.where(first, zero, x)], axis=0)

    def strips(x, r0):
        return jnp.concatenate([x[s * ts + r0:(s + 1) * ts] for s in range(n_strips)], axis=0)

    def unstrips(old, part, r0):
        n = ts - r0
        pieces = []
        for s in range(n_strips):
            if r0:
                pieces.append(old[s * ts:s * ts + r0])
            pieces.append(part[s * n:(s + 1) * n])
        return jnp.concatenate(pieces, axis=0)

    def step(key_start, carry, diag_off):
        r0 = 0 if diag_off is None else diag_off
        n = ts - r0
        new = []
        for p in range(n_pairs):
            pair = slice(p * LANES, (p + 1) * LANES)
            acc_all, logsurv_all = carry[2 * p], carry[2 * p + 1]
            acc, logsurv = strips(acc_all, r0), strips(logsurv_all, r0)
            zs, v2 = [], []
            for s in range(n_strips):
                k0 = pl.multiple_of(key_start + s * ts, tk)
                k2 = both_heads(k_ref[0, pl.ds(k0, tk), pair])
                v2.append(both_heads(v_ref[0, pl.ds(k0, tk), pair]))
                zs.append(_mm(q_ref[0, s * ts + r0:(s + 1) * ts, pair], k2, _NT))
            z = jnp.concatenate(zs, axis=0)
            neg_abs = lax.bitcast_convert_type(
                lax.bitcast_convert_type(z, jnp.uint32) | jnp.uint32(0x80000000), F32)
            sp = jnp.maximum(z, 0.0) + _log2(1.0 + jnp.exp2(neg_abs))
            if diag_off is not None:
                causal = jnp.concatenate([(col + diag_off < row)[r0:]] * n_strips, axis=0)
                sp = jnp.where(causal, sp, 0.0)
            hi, lo = _split2(sp)
            st = [_mm(jnp.concatenate([hi[:, e * tk:(e + 1) * tk], lo[:, e * tk:(e + 1) * tk]],
                                      axis=1), suffix_total) for e in range(2)]
            suffix = jnp.concatenate([st[0][:, :tk], st[1][:, :tk]], axis=1)
            total = jnp.concatenate([st[0][:, tk:], st[1][:, tk:]], axis=1)
            w = jnp.exp2((z - sp) + logsurv + suffix)
            if diag_off is not None:
                w = jnp.where(causal, w, 0.0)
            w = w.astype(BF16)
            acc = acc + jnp.concatenate(
                [_mm(w[s * n:(s + 1) * n], v2[s]) for s in range(n_strips)], axis=0)
            new += [unstrips(acc_all, acc, r0), unstrips(logsurv_all, logsurv + total, r0)]
        return tuple(new)

    carry = (jnp.zeros((tq, LANES), F32), jnp.zeros((tq, 2 * tk), F32)) * n_pairs
    for j in reversed(range(ts // tk)):
        carry = step(q0 + pad + j * tk, carry, j * tk)

    def alive(carry):
        worst = carry[1]
        for p in range(1, n_pairs):
            worst = jnp.maximum(worst, carry[2 * p + 1])
        return (jnp.max(worst) > ATTN_DEAD_LOG2).astype(jnp.int32)

    n_off = (q0 + pad) // tk

    def cond(state):
        return (state[0] < n_off) & (state[1] > 0)

    def body(state):
        i = state[0]
        carry = step(q0 + pad - (i + 1) * tk, state[2:], None)
        return (i + 1, alive(carry)) + carry

    carry = lax.while_loop(cond, body, (jnp.int32(0), alive(carry)) + carry)[2:]
    for p in range(n_pairs):
        o_ref[0, :, p * LANES:(p + 1) * LANES] = carry[2 * p].astype(o_ref.dtype)


def _attn_call(q, k, v, *, tq, ts, n_pairs):
    bsz, slen, hd = q.shape
    width = n_pairs * LANES
    pad = tq - ts
    k, v = (jnp.pad(t, ((0, 0), (pad, 0), (0, 0))) for t in (k, v))
    qblk = pl.BlockSpec((1, tq, width), lambda b, p, i: (b, i, p))
    kvblk = pl.BlockSpec((1, slen + pad, width), lambda b, p, i: (b, 0, p))
    return pl.pallas_call(
        functools.partial(_attn_body, tq=tq, ts=ts, n_pairs=n_pairs),
        grid=(bsz, hd // width, slen // tq),
        in_specs=[qblk, kvblk, kvblk],
        out_specs=qblk,
        out_shape=jax.ShapeDtypeStruct(q.shape, BF16),
        compiler_params=_params("parallel", "parallel", "arbitrary"),
        name="stick_breaking_attn",
    )(q, k, v)


def _rwkv_chunk_body(r_ref, lw_ref, k_ref, v_ref, a_ref, b_ref, g_ref, tri_ref, gmean_ref,
                     rk_ref, lnw_ref, lnb_ref, o_ref, st_ref, *, n_sub):
    c = pl.program_id(1)

    @pl.when(c == 0)
    def _():
        st_ref[...] = jnp.zeros_like(st_ref)

    r, lw, k, v = r_ref[0].astype(F32), lw_ref[0], k_ref[0].astype(F32), v_ref[0].astype(F32)
    a, b = a_ref[0].astype(F32), b_ref[0].astype(F32)
    ch = RWKV_CHUNK
    n_pairs = r.shape[1] // LANES
    cum = _mm_lhs_exact(tri_ref[...], lw)
    cum_prev = cum - lw
    lasts = [cum[(s + 1) * ch - 1:(s + 1) * ch, :] for s in range(n_sub)]
    last = jnp.concatenate([jnp.broadcast_to(t, (ch, t.shape[1])) for t in lasts], axis=0)
    e_pos = jnp.exp(cum)
    e_neg = jnp.exp(-cum)
    e_end = jnp.exp(last - cum)
    at = a * jnp.exp(cum_prev)
    rt = r * e_pos
    bt = b * e_neg
    kt = k * e_neg
    bh = b * e_end
    kh = k * e_end
    p_end = [jnp.exp(t) for t in lasts]

    lane = lax.broadcasted_iota(jnp.int32, (1, LANES), 1)
    first = lane < HEAD_DIM

    def stacked(x, s, p):
        x = x[s * ch:(s + 1) * ch, p * LANES:(p + 1) * LANES]
        zero = jnp.zeros_like(x)
        return jnp.concatenate([jnp.where(first, x, zero), jnp.where(first, zero, x)], axis=0)

    row = lax.broadcasted_iota(jnp.int32, (2 * ch, 2 * ch), 0)
    col = lax.broadcasted_iota(jnp.int32, (2 * ch, 2 * ch), 1)
    strict = (row & (ch - 1)) > (col & (ch - 1))
    row2 = lax.broadcasted_iota(jnp.int32, (2 * ch, 4 * ch), 0)
    col2 = lax.broadcasted_iota(jnp.int32, (2 * ch, 4 * ch), 1)
    lower2 = (row2 & (ch - 1)) >= (col2 & (ch - 1))
    diag = row == col
    eye = jnp.where(diag, 1.0, 0.0)
    n_levels = int(math.log2(ch))

    def sub_diag(level):
        size, half = 1 << level, 1 << (level - 1)
        inside = (row >> level) == (col >> level)
        return inside & ((row & (size - 1)) >= half) & ((col & (size - 1)) < half)

    units = [(s, p) for s in range(n_sub) for p in range(n_pairs)]
    at_s, rt_s, bt_s, kt_s, bh_s, kh_s, v_s = (
        [stacked(t, s, p) for s, p in units] for t in (at, rt, bt, kt, bh, kh, v))
    idx = range(len(units))
    prod = [_mm(jnp.concatenate([at_s[u], rt_s[u]], axis=0).astype(BF16),
                jnp.concatenate([bt_s[u], kt_s[u]], axis=0).astype(BF16), _NT) for u in idx]
    a_ab = [jnp.where(strict, prod[u][:2 * ch, :2 * ch], 0.0) for u in idx]
    a_ak = [jnp.where(strict, prod[u][:2 * ch, 2 * ch:], 0.0).astype(BF16) for u in idx]
    a_rbk = [jnp.where(lower2, prod[u][2 * ch:, :], 0.0).astype(BF16) for u in idx]
    lvl = sub_diag(1)
    tinv = [eye + jnp.where(lvl, a_ab[u], 0.0) for u in idx]
    for level in range(2, n_levels + 1):
        lvl = sub_diag(level)
        tinv_b = [tinv[u].astype(BF16) for u in idx]
        right = [_mm(jnp.where(lvl, a_ab[u], 0.0).astype(BF16), tinv_b[u]) for u in idx]
        tinv = [tinv[u] + _mm(tinv_b[u], right[u].astype(BF16)) for u in idx]
    v_b = [v_s[u].astype(BF16) for u in idx]
    akv = [_mm(a_ak[u], v_b[u]) for u in idx]
    wx = [_mm(tinv[u].astype(BF16),
              jnp.concatenate([at_s[u], akv[u]], axis=1).astype(BF16)) for u in idx]
    zero_b = jnp.zeros((2 * ch, LANES), BF16)
    wxv = [jnp.concatenate([wx[u].astype(BF16), jnp.concatenate([zero_b, v_b[u]], axis=1)], axis=0)
           for u in idx]
    qy = [_mm(a_rbk[u], wxv[u]) for u in idx]
    mn = [_mm(jnp.concatenate([bh_s[u], kh_s[u]], axis=0).T.astype(BF16), wxv[u]) for u in idx]
    ys = []
    for s in range(n_sub):
        rows = []
        for p in range(n_pairs):
            u = s * n_pairs + p
            st = st_ref[p]
            y_st = _mm((rt_s[u] + qy[u][:, :LANES]).astype(BF16), st.astype(BF16)) + qy[u][:, LANES:]
            rows.append(y_st[:ch] + y_st[ch:])
            m_t = jnp.where(diag, p_end[s][:, p * LANES:(p + 1) * LANES], 0.0) + mn[u][:, :LANES]
            st_ref[p] = _mm3(m_t, st) + mn[u][:, LANES:]
        ys.append(jnp.concatenate(rows, axis=1))
    y = jnp.concatenate(ys, axis=0)

    gmean = gmean_ref[...]
    y_hi, y_lo = _split2(y)
    mean = _groups(y_hi, gmean) + _groups(y_lo, gmean)
    dev = y - mean
    var = _groups((dev * dev).astype(BF16), gmean)
    yn = dev * lax.rsqrt(var + GN_EPS) * lnw_ref[...] + lnb_ref[...]
    bonus = _groups((r * k * rk_ref[...]).astype(BF16), gmean) * float(HEAD_DIM) * v
    o_ref[0] = ((yn + bonus) * g_ref[0].astype(F32)).astype(o_ref.dtype)


def _rwkv_chunk_call(r, lw, k, v, a, b, g, gmean, rk, lnw, lnb, *, n_sub):
    bsz, slen, hd = r.shape
    rows = n_sub * RWKV_CHUNK
    idx = jnp.arange(rows)
    same_chunk = idx[:, None] // RWKV_CHUNK == idx[None, :] // RWKV_CHUNK
    tri = (same_chunk & (idx[:, None] >= idx[None, :])).astype(BF16)
    tok = pl.BlockSpec((1, rows, hd), lambda bi, ci: (bi, ci, 0))
    consts = [tri, gmean, rk, lnw, lnb]
    return pl.pallas_call(
        functools.partial(_rwkv_chunk_body, n_sub=n_sub),
        grid=(bsz, slen // rows),
        in_specs=[tok] * 7 + [_resident(c.shape) for c in consts],
        out_specs=tok,
        out_shape=jax.ShapeDtypeStruct((bsz, slen, hd), BF16),
        scratch_shapes=[pltpu.VMEM((hd // LANES, LANES, LANES), F32)],
        compiler_params=_params("parallel", "arbitrary"),
        name="rwkv_chunk_scan",
    )(r, lw, k, v, a, b, g, *consts)


def _merge_body(h_ref, ya_ref, yb_ref, gate_ref, mod_ref, wa_ref, wb_ref, wo_ref, o_ref):
    d = h_ref.shape[-1]
    ua = _mm(ya_ref[0], wa_ref[...])
    ub = _mm(yb_ref[0], wb_ref[...])
    merged = gate_ref[0, :, 0:d].astype(F32) * ua + gate_ref[0, :, d:2 * d].astype(F32) * ub
    out = _mm(merged.astype(BF16), wo_ref[...])
    o_ref[0] = h_ref[0] + mod_ref[0, 5:6, :] * out


def _merge_call(h, ya, yb, gates, mod, wa, wb, wo, *, tm):
    bsz, slen, d = h.shape

    def tok(width):
        return pl.BlockSpec((1, tm, width), lambda b, i: (b, i, 0))

    return pl.pallas_call(
        _merge_body,
        grid=(bsz, slen // tm),
        in_specs=[
            tok(d), tok(ya.shape[-1]), tok(yb.shape[-1]), tok(2 * d),
            pl.BlockSpec((1, N_MOD, d), lambda b, i: (b, 0, 0)),
            _resident(wa.shape), _resident(wb.shape), _resident(wo.shape),
        ],
        out_specs=tok(d),
        out_shape=jax.ShapeDtypeStruct(h.shape, F32),
        compiler_params=_params("parallel", "parallel"),
        name="merge_out",
    )(h, ya, yb, gates, mod, wa, wb, wo)


def _block_diag_const(width, group, value, dtype):
    idx = jnp.arange(width) // group
    return jnp.where(idx[:, None] == idx[None, :], value, 0.0).astype(dtype)


def kernel(x, c, mod_w, mod_b, norm_g, ffn1_wg, ffn1_wu, ffn1_wd, w_in_first, w_in_rest, shift_mu, decay_w0, decay_w2, aaa_a0, aaa_a2, gate_g2, k_k, k_a, r_k, lnx_w, lnx_b, vres_v0, vres_v2, q_norm_g, k_norm_g, w_up_a, w_up_b, w_out, ffn2_wg, ffn2_wu, ffn2_wd):
    bsz, slen, d = x.shape
    n_layers = mod_w.shape[0]
    hd = decay_w0.shape[-1]
    n_heads = hd // HEAD_DIM
    decay_rank, aaa_rank = decay_w2.shape[1], aaa_a2.shape[1]
    gate_rank, vres_rank = gate_g2.shape[1], vres_v2.shape[1]
    assert decay_rank + aaa_rank == LANES and gate_rank == LANES and vres_rank <= LANES
    assert w_up_b.shape[1] == hd and hd % LANES == 0
    rw = 3 * hd + decay_rank + aaa_rank + gate_rank
    cols_first = rw + 3 * hd + 2 * d
    assert w_in_first.shape[1] == cols_first

    tm = min(512, slen)
    tq_attn = min(512, slen)
    ts_attn = tq_attn
    attn_pairs = hd // LANES

    gmean = _block_diag_const(MXU_WIDTH, HEAD_DIM, 1.0 / HEAD_DIM, BF16)
    gsum = _block_diag_const(MXU_WIDTH, HEAD_DIM, 1.0, BF16)

    mod_all = _mod_call(c, mod_w, mod_b).reshape(n_layers, bsz, N_MOD, d)

    h = x
    v_first = None
    for l in range(n_layers):
        mod = mod_all[l]
        h = _ffn_call(h, norm_g[l], mod, ffn1_wg[l].astype(BF16), ffn1_wu[l].astype(BF16),
                      ffn1_wd[l].astype(BF16), sub=0, tm=tm, tf=256)

        has_vres = l > 0
        if has_vres:
            w_in = jnp.pad(w_in_rest[l - 1], ((0, 0), (0, LANES - vres_rank)))
        else:
            w_in = w_in_first
        gq = jnp.tile(q_norm_g[l], n_heads)[None, :]
        gk = jnp.tile(k_norm_g[l], n_heads)[None, :]
        lora_w = jnp.zeros((LANES, 2 * hd), F32)
        lora_w = lora_w.at[:decay_rank, :hd].set(decay_w2[l]).at[decay_rank:, hd:].set(aaa_a2[l])
        prep_consts = {
            "mu": shift_mu[l][None, :], "w0": decay_w0[l][None, :], "lora": lora_w,
            "a0": aaa_a0[l][None, :], "g2": gate_g2[l].astype(BF16), "k_k": k_k[l][None, :],
            "k_a": k_a[l][None, :], "gsum": gsum,
        }
        if has_vres:
            prep_consts["v0"] = vres_v0[l - 1][None, :]
            prep_consts["v2"] = jnp.pad(vres_v2[l - 1], ((0, LANES - vres_rank), (0, 0))).astype(BF16)
        q, k, v, gates, r_, lw_, k_, v_, a_, b_, g_ = _inproj_call(
            h, norm_g[l], mod, w_in.astype(BF16), gq, gk, gmean, v_first, prep_consts,
            rw=rw, hd=hd, decay_rank=decay_rank, tm=tm)
        if not has_vres:
            v_first = v_
        y_a = _rwkv_chunk_call(r_, lw_, k_, v_, a_, b_, g_, gmean, r_k[l].reshape(1, hd),
                               lnx_w[l][None, :], lnx_b[l][None, :], n_sub=RWKV_SUBCHUNKS)

        y_b = _attn_call(q, k, v, tq=tq_attn, ts=ts_attn, n_pairs=attn_pairs)

        h = _merge_call(h, y_a, y_b, gates, mod, w_up_a[l].astype(BF16), w_up_b[l].astype(BF16),
                        w_out[l].astype(BF16), tm=tm)

        h = _ffn_call(h, norm_g[l], mod, ffn2_wg[l].astype(BF16), ffn2_wu[l].astype(BF16),
                      ffn2_wd[l].astype(BF16), sub=2, tm=tm, tf=256)
    return h
```

```python
import functools
import math

import jax
import jax.numpy as jnp
from jax import lax
from jax.experimental import pallas as pl
from jax.experimental.pallas import tpu as pltpu

F32 = jnp.float32
BF16 = jnp.bfloat16

HEAD_DIM = 64
RMS_EPS = 1e-6
GN_EPS = 64e-5
N_MOD = 9
LANES = 128
MXU_WIDTH = 256
VMEM_LIMIT = 56 * 1024 * 1024
RWKV_CHUNK = 64
RWKV_SUBCHUNKS = 4

_NN = (((1,), (0,)), ((), ()))
_NT = (((1,), (1,)), ((), ()))


def _mm(a, b, dims=_NN):
    return lax.dot_general(a, b, dims, preferred_element_type=F32)


def _split2(x):
    hi = x.astype(BF16)
    lo = (x - hi.astype(F32)).astype(BF16)
    return hi, lo


def _split3(x):
    hi = x.astype(BF16)
    r1 = x - hi.astype(F32)
    mid = r1.astype(BF16)
    lo = (r1 - mid.astype(F32)).astype(BF16)
    return hi, mid, lo


def _mm3(a, b, dims=_NN):
    ah, al = _split2(a)
    bh, bl = _split2(b)
    return _mm(ah, bh, dims) + (_mm(ah, bl, dims) + _mm(al, bh, dims))


def _mm_lhs_exact(a_bf16, b):
    hi, mid, lo = _split3(b)
    return _mm(a_bf16, hi) + (_mm(a_bf16, mid) + _mm(a_bf16, lo))


def _groups(x_bf16, g):
    n = g.shape[0]
    slabs = [_mm(x_bf16[:, i * n:(i + 1) * n], g) for i in range(x_bf16.shape[1] // n)]
    return jnp.concatenate(slabs, axis=1)


def _params(*sem):
    return pltpu.CompilerParams(dimension_semantics=sem, vmem_limit_bytes=VMEM_LIMIT)


def _resident(shape):
    zeros = (0,) * len(shape)
    return pl.BlockSpec(shape, lambda *_: zeros, pipeline_mode=pl.Buffered(1))


def _norm_mod(x, g, sh, sc):
    ms = jnp.mean(x * x, axis=-1, keepdims=True)
    y = x * lax.rsqrt(ms + RMS_EPS) * g
    return y * (1.0 + sc) + sh


LOG2E = 1.4426950408889634
ATTN_DEAD_LOG2 = -160.0


def _log2(x):
    return jnp.log(x) * LOG2E


def _softplus(x):
    return jnp.maximum(x, 0.0) + jnp.log(1.0 + jnp.exp(-jnp.abs(x)))


def _mod_body(c_ref, w_ref, b_ref, o_ref):
    c = c_ref[...]
    ca = (c * jax.nn.sigmoid(c)).astype(BF16)
    o_ref[0] = _mm(ca, w_ref[0].astype(BF16)) + b_ref[0]


def _mod_call(c, mod_w, mod_b):
    n_layers, d, nd = mod_w.shape
    bsz = c.shape[0]
    tn = d
    return pl.pallas_call(
        _mod_body,
        grid=(n_layers, nd // tn),
        in_specs=[
            pl.BlockSpec((bsz, d), lambda l, j: (0, 0)),
            pl.BlockSpec((1, d, tn), lambda l, j: (l, 0, j)),
            pl.BlockSpec((1, 1, tn), lambda l, j: (l, 0, j)),
        ],
        out_specs=pl.BlockSpec((1, bsz, tn), lambda l, j: (l, 0, j)),
        out_shape=jax.ShapeDtypeStruct((n_layers, bsz, nd), F32),
        compiler_params=_params("parallel", "parallel"),
        name="adaln_mod",
    )(c, mod_w, mod_b.reshape(n_layers, 1, nd))


def _ffn_body(*refs, sub, tf, merge):
    if merge:
        h_ref, ya_ref, yb_ref, gate_ref, wa_ref, wb_ref, wo_ref = refs[:7]
        refs = refs[7:]
    else:
        h_ref = refs[0]
        refs = refs[1:]
    g_ref, mod_ref, wg_ref, wu_ref, wd_ref, o_ref = refs
    x = h_ref[0]
    if merge:
        d = x.shape[-1]
        ua = _mm(ya_ref[0], wa_ref[...])
        ub = _mm(yb_ref[0], wb_ref[...])
        merged = gate_ref[0, :, 0:d].astype(F32) * ua + gate_ref[0, :, d:2 * d].astype(F32) * ub
        x = x + mod_ref[0, 5:6, :] * _mm(merged.astype(BF16), wo_ref[...])
    g = g_ref[sub:sub + 1, :]
    sh = mod_ref[0, 3 * sub:3 * sub + 1, :]
    sc = mod_ref[0, 3 * sub + 1:3 * sub + 2, :]
    gt = mod_ref[0, 3 * sub + 2:3 * sub + 3, :]
    n = _norm_mod(x, g, sh, sc).astype(BF16)
    d_ff = wg_ref.shape[1]
    acc = jnp.zeros_like(x)
    for j in range(d_ff // tf):
        gj = _mm(n, wg_ref[:, j * tf:(j + 1) * tf])
        uj = _mm(n, wu_ref[:, j * tf:(j + 1) * tf])
        a = (gj * jax.nn.sigmoid(gj) * uj).astype(BF16)
        acc = acc + _mm(a, wd_ref[j * tf:(j + 1) * tf, :])
    o_ref[0] = x + (0.5 * gt) * acc


def _ffn_call(h, norm_g, mod, wg, wu, wd, *, sub, tm, tf, branches=None):
    bsz, slen, d = h.shape
    d_ff = wg.shape[1]

    def tok(width):
        return pl.BlockSpec((1, tm, width), lambda b, i: (b, i, 0))

    args, specs = [h], [tok(d)]
    if branches is not None:
        ya, yb, gates, wa, wb, wo = branches
        args += [ya, yb, gates, wa, wb, wo]
        specs += [tok(ya.shape[-1]), tok(yb.shape[-1]), tok(2 * d),
                  _resident(wa.shape), _resident(wb.shape), _resident(wo.shape)]
    args += [norm_g, mod, wg, wu, wd]
    specs += [_resident(norm_g.shape), pl.BlockSpec((1, N_MOD, d), lambda b, i: (b, 0, 0)),
              _resident((d, d_ff)), _resident((d, d_ff)), _resident((d_ff, d))]
    return pl.pallas_call(
        functools.partial(_ffn_body, sub=sub, tf=tf, merge=branches is not None),
        grid=(bsz, slen // tm),
        in_specs=specs,
        out_specs=tok(d),
        out_shape=jax.ShapeDtypeStruct(h.shape, F32),
        compiler_params=_params("parallel", "parallel"),
        name="merge_ffn" if branches is not None else "ffn",
    )(*args)


def _head_rms(t, g, gmean):
    ms = _groups((t * t).astype(BF16), gmean)
    return t * lax.rsqrt(ms + RMS_EPS) * g


def _rwkv_prep(p, prev, tail, v_first, c, *, hd, decay_rank):
    tm = p.shape[0]
    shifted = pltpu.roll(p, 1, axis=0)
    rows = lax.broadcasted_iota(jnp.int32, (tm, 1), 0)
    shifted = jnp.where(rows == 0, prev, shifted)
    p = p + (shifted - p) * c["mu"]
    r = p[:, 0:hd]
    k = p[:, hd:2 * hd]
    v = p[:, 2 * hd:3 * hd]
    wa = p[:, 3 * hd:3 * hd + LANES]
    gl = p[:, 3 * hd + LANES:3 * hd + 2 * LANES]
    lane = lax.broadcasted_iota(jnp.int32, (1, LANES), 1)
    wa = jnp.where(lane < decay_rank, jnp.tanh(wa), wa)
    lora = _mm3(wa, c["lora"])
    w = -_softplus(-(c["w0"] + lora[:, 0:hd])) - 0.5
    log_decay = -jnp.exp(w)
    a = jax.nn.sigmoid(c["a0"] + lora[:, hd:2 * hd])
    g = _mm(jax.nn.sigmoid(gl).astype(BF16), c["g2"])
    if tail is not None:
        mix = jax.nn.sigmoid(c["v0"] + _mm(tail.astype(BF16), c["v2"]))
        v = v + (v_first - v) * mix
    kk = k * c["k_k"]
    sq_hi, sq_lo = _split2(kk * kk)
    ss = _groups(sq_hi, c["gsum"]) + _groups(sq_lo, c["gsum"])
    kk = kk / jnp.maximum(jnp.sqrt(ss), 1e-12)
    return r, log_decay, k * (1.0 + (a - 1.0) * c["k_a"]), v, -kk, kk * a, g


_PREP_CONSTS = ("mu", "w0", "lora", "a0", "g2", "k_k", "k_a", "gsum")
_VRES_CONSTS = ("v0", "v2")


def _inproj_body(*refs, rw, hd, d, has_vres, decay_rank):
    names = ("h", "g", "mod", "w", "gq", "gk", "gmean") + (("vf",) if has_vres else ())
    names += _PREP_CONSTS + (_VRES_CONSTS if has_vres else ())
    ins = dict(zip(names, refs))
    q_ref, k_ref, v_ref, gate_ref = refs[len(names):len(names) + 4]
    rwkv_outs = refs[len(names) + 4:len(names) + 11]
    prev_ref = refs[len(names) + 11]
    i = pl.program_id(1)
    x = ins["h"][0]
    tm = x.shape[0]
    n = _norm_mod(x, ins["g"][1:2, :], ins["mod"][0, 3:4, :], ins["mod"][0, 4:5, :]).astype(BF16)
    w_ref = ins["w"]
    gmean = ins["gmean"][...]
    p = _mm(n, w_ref[:, 0:rw])
    og = rw + 3 * hd
    tail = _mm(n, w_ref[:, og + 2 * d:og + 2 * d + LANES]) if has_vres else None
    q = _mm(n, w_ref[:, rw:rw + hd])
    q_ref[0] = (_head_rms(q, ins["gq"][...], gmean) * (LOG2E / math.sqrt(HEAD_DIM))).astype(BF16)
    k = _mm(n, w_ref[:, rw + hd:rw + 2 * hd])
    k_ref[0] = _head_rms(k, ins["gk"][...], gmean).astype(BF16)
    v_ref[0] = _mm(n, w_ref[:, rw + 2 * hd:rw + 3 * hd]).astype(BF16)
    gate_ref[0] = jax.nn.sigmoid(_mm(n, w_ref[:, og:og + 2 * d])).astype(BF16)

    prev = prev_ref[7:8, :]
    prev = jnp.where(i == 0, jnp.zeros_like(prev), prev)
    prev_ref[...] = p[tm - 8:tm, :]
    consts = {name: ins[name][...] for name in _PREP_CONSTS + (_VRES_CONSTS if has_vres else ())}
    v_first = ins["vf"][0].astype(F32) if has_vres else None
    outs = _rwkv_prep(p, prev, tail, v_first, consts, hd=hd, decay_rank=decay_rank)
    for o_ref, val in zip(rwkv_outs, outs):
        o_ref[0] = val.astype(o_ref.dtype)


def _inproj_call(h, norm_g, mod, w, gq, gk, gmean, v_first, prep_consts, *, rw, hd, decay_rank, tm):
    bsz, slen, d = h.shape
    has_vres = v_first is not None

    def tok(width):
        return pl.BlockSpec((1, tm, width), lambda b, i: (b, i, 0))

    args = [h, norm_g, mod, w, gq, gk, gmean]
    specs = [tok(d), _resident(norm_g.shape), pl.BlockSpec((1, N_MOD, d), lambda b, i: (b, 0, 0)),
             _resident(w.shape), _resident(gq.shape), _resident(gk.shape), _resident(gmean.shape)]
    if has_vres:
        args.append(v_first)
        specs.append(tok(hd))
    names = _PREP_CONSTS + (_VRES_CONSTS if has_vres else ())
    args += [prep_consts[name] for name in names]
    specs += [_resident(prep_consts[name].shape) for name in names]
    outs = [(hd, BF16)] * 3 + [(2 * d, BF16)] + [(hd, F32 if j == 1 else BF16) for j in range(7)]
    return pl.pallas_call(
        functools.partial(_inproj_body, rw=rw, hd=hd, d=d, has_vres=has_vres, decay_rank=decay_rank),
        grid=(bsz, slen // tm),
        in_specs=specs,
        out_specs=[tok(wd) for wd, _ in outs],
        out_shape=[jax.ShapeDtypeStruct((bsz, slen, wd), dt) for wd, dt in outs],
        scratch_shapes=[pltpu.VMEM((8, rw), F32)],
        compiler_params=_params("parallel", "arbitrary"),
        name="in_proj",
    )(*args)


def _attn_body(q_ref, k_ref, v_ref, o_ref, *, tq, ts, n_pairs):
    tk = LANES
    n_strips = tq // ts
    pad = tq - ts
    qi = pl.program_id(2)
    q0 = qi * tq
    lane = lax.broadcasted_iota(jnp.int32, (1, LANES), 1)
    first = lane < HEAD_DIM
    row = lax.broadcasted_iota(jnp.int32, (ts, 2 * tk), 0)
    col = lax.broadcasted_iota(jnp.int32, (ts, 2 * tk), 1)
    col = jnp.where(col >= tk, col - tk, col)
    jj = lax.broadcasted_iota(jnp.int32, (2 * tk, 2 * tk), 0)
    ss = lax.broadcasted_iota(jnp.int32, (2 * tk, 2 * tk), 1)
    jj = jnp.where(jj >= tk, jj - tk, jj)
    suffix_total = jnp.where((ss >= tk) | (jj > ss), -1.0, 0.0).astype(BF16)

    def both_heads(x):
        zero = jnp.zeros_like(x)
        return jnp.concatenate([jnp.where(first, x, zero), jnp.where(first, zero, x)], axis=0)

    def strips(x, r0):
        return jnp.concatenate([x[s * ts + r0:(s + 1) * ts] for s in range(n_strips)], axis=0)

    def unstrips(old, part, r0):
        n = ts - r0
        pieces = []
        for s in range(n_strips):
            if r0:
                pieces.append(old[s * ts:s * ts + r0])
            pieces.append(part[s * n:(s + 1) * n])
        return jnp.concatenate(pieces, axis=0)

    def step(key_start, carry, diag_off):
        r0 = 0 if diag_off is None else diag_off
        n = ts - r0
        new = []
        for p in range(n_pairs):
            pair = slice(p * LANES, (p + 1) * LANES)
            acc_all, logsurv_all = carry[2 * p], carry[2 * p + 1]
            acc, logsurv = strips(acc_all, r0), strips(logsurv_all, r0)
            zs, v2 = [], []
            for s in range(n_strips):
                k0 = pl.multiple_of(key_start + s * ts, tk)
                k2 = both_heads(k_ref[0, pl.ds(k0, tk), pair])
                v2.append(both_heads(v_ref[0, pl.ds(k0, tk), pair]))
                zs.append(_mm(q_ref[0, s * ts + r0:(s + 1) * ts, pair], k2, _NT))
            z = jnp.concatenate(zs, axis=0)
            neg_abs = lax.bitcast_convert_type(
                lax.bitcast_convert_type(z, jnp.uint32) | jnp.uint32(0x80000000), F32)
            sp = jnp.maximum(z, 0.0) + _log2(1.0 + jnp.exp2(neg_abs))
            if diag_off is not None:
                causal = jnp.concatenate([(col + diag_off < row)[r0:]] * n_strips, axis=0)
                sp = jnp.where(causal, sp, 0.0)
            hi, lo = _split2(sp)
            st = [_mm(jnp.concatenate([hi[:, e * tk:(e + 1) * tk], lo[:, e * tk:(e + 1) * tk]],
                                      axis=1), suffix_total) for e in range(2)]
            suffix = jnp.concatenate([st[0][:, :tk], st[1][:, :tk]], axis=1)
            total = jnp.concatenate([st[0][:, tk:], st[1][:, tk:]], axis=1)
            w = jnp.exp2((z - sp) + logsurv + suffix)
            if diag_off is not None:
                w = jnp.where(causal, w, 0.0)
            w = w.astype(BF16)
            acc = acc + jnp.concatenate(
                [_mm(w[s * n:(s + 1) * n], v2[s]) for s in range(n_strips)], axis=0)
            new += [unstrips(acc_all, acc, r0), unstrips(logsurv_all, logsurv + total, r0)]
        return tuple(new)

    carry = (jnp.zeros((tq, LANES), F32), jnp.zeros((tq, 2 * tk), F32)) * n_pairs
    for j in reversed(range(ts // tk)):
        carry = step(q0 + pad + j * tk, carry, j * tk)

    def alive(carry):
        worst = carry[1]
        for p in range(1, n_pairs):
            worst = jnp.maximum(worst, carry[2 * p + 1])
        return (jnp.max(worst) > ATTN_DEAD_LOG2).astype(jnp.int32)

    n_off = (q0 + pad) // tk

    def cond(state):
        return (state[0] < n_off) & (state[1] > 0)

    def body(state):
        i = state[0]
        carry = step(q0 + pad - (i + 1) * tk, state[2:], None)
        return (i + 1, alive(carry)) + carry

    carry = lax.while_loop(cond, body, (jnp.int32(0), alive(carry)) + carry)[2:]
    for p in range(n_pairs):
        o_ref[0, :, p * LANES:(p + 1) * LANES] = carry[2 * p].astype(o_ref.dtype)


def _attn_call(q, k, v, *, tq, ts, n_pairs):
    bsz, slen, hd = q.shape
    width = n_pairs * LANES
    pad = tq - ts
    k, v = (jnp.pad(t, ((0, 0), (pad, 0), (0, 0))) for t in (k, v))
    qblk = pl.BlockSpec((1, tq, width), lambda b, p, i: (b, i, p))
    kvblk = pl.BlockSpec((1, slen + pad, width), lambda b, p, i: (b, 0, p))
    return pl.pallas_call(
        functools.partial(_attn_body, tq=tq, ts=ts, n_pairs=n_pairs),
        grid=(bsz, hd // width, slen // tq),
        in_specs=[qblk, kvblk, kvblk],
        out_specs=qblk,
        out_shape=jax.ShapeDtypeStruct(q.shape, BF16),
        compiler_params=_params("parallel", "parallel", "arbitrary"),
        name="stick_breaking_attn",
    )(q, k, v)


def _rwkv_chunk_body(r_ref, lw_ref, k_ref, v_ref, a_ref, b_ref, g_ref, tri_ref, gmean_ref,
                     rk_ref, lnw_ref, lnb_ref, o_ref, st_ref, *, n_sub):
    c = pl.program_id(1)

    @pl.when(c == 0)
    def _():
        st_ref[...] = jnp.zeros_like(st_ref)

    r, lw, k, v = r_ref[0].astype(F32), lw_ref[0], k_ref[0].astype(F32), v_ref[0].astype(F32)
    a, b = a_ref[0].astype(F32), b_ref[0].astype(F32)
    ch = RWKV_CHUNK
    n_pairs = r.shape[1] // LANES
    cum = _mm_lhs_exact(tri_ref[...], lw)
    cum_prev = cum - lw
    lasts = [cum[(s + 1) * ch - 1:(s + 1) * ch, :] for s in range(n_sub)]
    last = jnp.concatenate([jnp.broadcast_to(t, (ch, t.shape[1])) for t in lasts], axis=0)
    e_pos = jnp.exp(cum)
    e_neg = jnp.exp(-cum)
    e_end = jnp.exp(last - cum)
    at = a * jnp.exp(cum_prev)
    rt = r * e_pos
    bt = b * e_neg
    kt = k * e_neg
    bh = b * e_end
    kh = k * e_end
    p_end = [jnp.exp(t) for t in lasts]

    lane = lax.broadcasted_iota(jnp.int32, (1, LANES), 1)
    first = lane < HEAD_DIM

    def stacked(x, s, p):
        x = x[s * ch:(s + 1) * ch, p * LANES:(p + 1) * LANES]
        zero = jnp.zeros_like(x)
        return jnp.concatenate([jnp.where(first, x, zero), jnp.where(first, zero, x)], axis=0)

    row = lax.broadcasted_iota(jnp.int32, (2 * ch, 2 * ch), 0)
    col = lax.broadcasted_iota(jnp.int32, (2 * ch, 2 * ch), 1)
    strict = (row & (ch - 1)) > (col & (ch - 1))
    row2 = lax.broadcasted_iota(jnp.int32, (2 * ch, 4 * ch), 0)
    col2 = lax.broadcasted_iota(jnp.int32, (2 * ch, 4 * ch), 1)
    lower2 = (row2 & (ch - 1)) >= (col2 & (ch - 1))
    diag = row == col
    eye = jnp.where(diag, 1.0, 0.0)
    n_levels = int(math.log2(ch))

    def sub_diag(level):
        size, half = 1 << level, 1 << (level - 1)
        inside = (row >> level) == (col >> level)
        return inside & ((row & (size - 1)) >= half) & ((col & (size - 1)) < half)

    units = [(s, p) for s in range(n_sub) for p in range(n_pairs)]
    at_s, rt_s, bt_s, kt_s, bh_s, kh_s, v_s = (
        [stacked(t, s, p) for s, p in units] for t in (at, rt, bt, kt, bh, kh, v))
    idx = range(len(units))
    prod = [_mm(jnp.concatenate([at_s[u], rt_s[u]], axis=0).astype(BF16),
                jnp.concatenate([bt_s[u], kt_s[u]], axis=0).astype(BF16), _NT) for u in idx]
    a_ab = [jnp.where(strict, prod[u][:2 * ch, :2 * ch], 0.0) for u in idx]
    a_ak = [jnp.where(strict, prod[u][:2 * ch, 2 * ch:], 0.0).astype(BF16) for u in idx]
    a_rbk = [jnp.where(lower2, prod[u][2 * ch:, :], 0.0).astype(BF16) for u in idx]
    lvl = sub_diag(1)
    tinv = [eye + jnp.where(lvl, a_ab[u], 0.0) for u in idx]
    for level in range(2, n_levels + 1):
        lvl = sub_diag(level)
        tinv_b = [tinv[u].astype(BF16) for u in idx]
        right = [_mm(jnp.where(lvl, a_ab[u], 0.0).astype(BF16), tinv_b[u]) for u in idx]
        tinv = [tinv[u] + _mm(tinv_b[u], right[u].astype(BF16)) for u in idx]
    v_b = [v_s[u].astype(BF16) for u in idx]
    akv = [_mm(a_ak[u], v_b[u]) for u in idx]
    wx = [_mm(tinv[u].astype(BF16),
              jnp.concatenate([at_s[u], akv[u]], axis=1).astype(BF16)) for u in idx]
    zero_b = jnp.zeros((2 * ch, LANES), BF16)
    wxv = [jnp.concatenate([wx[u].astype(BF16), jnp.concatenate([zero_b, v_b[u]], axis=1)], axis=0)
           for u in idx]
    qy = [_mm(a_rbk[u], wxv[u]) for u in idx]
    mn = [_mm(jnp.concatenate([bh_s[u], kh_s[u]], axis=0).T.astype(BF16), wxv[u]) for u in idx]
    ys = []
    for s in range(n_sub):
        rows = []
        for p in range(n_pairs):
            u = s * n_pairs + p
            st = st_ref[p]
            y_st = _mm((rt_s[u] + qy[u][:, :LANES]).astype(BF16), st.astype(BF16)) + qy[u][:, LANES:]
            rows.append(y_st[:ch] + y_st[ch:])
            m_t = jnp.where(diag, p_end[s][:, p * LANES:(p + 1) * LANES], 0.0) + mn[u][:, :LANES]
            st_ref[p] = _mm3(m_t, st) + mn[u][:, LANES:]
        ys.append(jnp.concatenate(rows, axis=1))
    y = jnp.concatenate(ys, axis=0)

    gmean = gmean_ref[...]
    y_hi, y_lo = _split2(y)
    mean = _groups(y_hi, gmean) + _groups(y_lo, gmean)
    dev = y - mean
    var = _groups((dev * dev).astype(BF16), gmean)
    yn = dev * lax.rsqrt(var + GN_EPS) * lnw_ref[...] + lnb_ref[...]
    bonus = _groups((r * k * rk_ref[...]).astype(BF16), gmean) * float(HEAD_DIM) * v
    o_ref[0] = ((yn + bonus) * g_ref[0].astype(F32)).astype(o_ref.dtype)


def _rwkv_chunk_call(r, lw, k, v, a, b, g, gmean, rk, lnw, lnb, *, n_sub):
    bsz, slen, hd = r.shape
    rows = n_sub * RWKV_CHUNK
    idx = jnp.arange(rows)
    same_chunk = idx[:, None] // RWKV_CHUNK == idx[None, :] // RWKV_CHUNK
    tri = (same_chunk & (idx[:, None] >= idx[None, :])).astype(BF16)
    tok = pl.BlockSpec((1, rows, hd), lambda bi, ci: (bi, ci, 0))
    consts = [tri, gmean, rk, lnw, lnb]
    return pl.pallas_call(
        functools.partial(_rwkv_chunk_body, n_sub=n_sub),
        grid=(bsz, slen // rows),
        in_specs=[tok] * 7 + [_resident(c.shape) for c in consts],
        out_specs=tok,
        out_shape=jax.ShapeDtypeStruct((bsz, slen, hd), BF16),
        scratch_shapes=[pltpu.VMEM((hd // LANES, LANES, LANES), F32)],
        compiler_params=_params("parallel", "arbitrary"),
        name="rwkv_chunk_scan",
    )(r, lw, k, v, a, b, g, *consts)


def _block_diag_const(width, group, value, dtype):
    idx = jnp.arange(width) // group
    return jnp.where(idx[:, None] == idx[None, :], value, 0.0).astype(dtype)


def kernel(x, c, mod_w, mod_b, norm_g, ffn1_wg, ffn1_wu, ffn1_wd, w_in_first, w_in_rest, shift_mu, decay_w0, decay_w2, aaa_a0, aaa_a2, gate_g2, k_k, k_a, r_k, lnx_w, lnx_b, vres_v0, vres_v2, q_norm_g, k_norm_g, w_up_a, w_up_b, w_out, ffn2_wg, ffn2_wu, ffn2_wd):
    bsz, slen, d = x.shape
    n_layers = mod_w.shape[0]
    hd = decay_w0.shape[-1]
    n_heads = hd // HEAD_DIM
    decay_rank, aaa_rank = decay_w2.shape[1], aaa_a2.shape[1]
    gate_rank, vres_rank = gate_g2.shape[1], vres_v2.shape[1]
    assert decay_rank + aaa_rank == LANES and gate_rank == LANES and vres_rank <= LANES
    assert w_up_b.shape[1] == hd and hd % LANES == 0
    rw = 3 * hd + decay_rank + aaa_rank + gate_rank
    cols_first = rw + 3 * hd + 2 * d
    assert w_in_first.shape[1] == cols_first

    tm = min(512, slen)
    tq_attn = min(512, slen)
    ts_attn = tq_attn
    attn_pairs = hd // LANES

    gmean = _block_diag_const(MXU_WIDTH, HEAD_DIM, 1.0 / HEAD_DIM, BF16)
    gsum = _block_diag_const(MXU_WIDTH, HEAD_DIM, 1.0, BF16)

    mod_all = _mod_call(c, mod_w, mod_b).reshape(n_layers, bsz, N_MOD, d)

    h = x
    v_first = None
    for l in range(n_layers):
        mod = mod_all[l]
        h = _ffn_call(h, norm_g[l], mod, ffn1_wg[l].astype(BF16), ffn1_wu[l].astype(BF16),
                      ffn1_wd[l].astype(BF16), sub=0, tm=tm, tf=256)

        has_vres = l > 0
        if has_vres:
            w_in = jnp.pad(w_in_rest[l - 1], ((0, 0), (0, LANES - vres_rank)))
        else:
            w_in = w_in_first
        gq = jnp.tile(q_norm_g[l], n_heads)[None, :]
        gk = jnp.tile(k_norm_g[l], n_heads)[None, :]
        lora_w = jnp.zeros((LANES, 2 * hd), F32)
        lora_w = lora_w.at[:decay_rank, :hd].set(decay_w2[l]).at[decay_rank:, hd:].set(aaa_a2[l])
        prep_consts = {
            "mu": shift_mu[l][None, :], "w0": decay_w0[l][None, :], "lora": lora_w,
            "a0": aaa_a0[l][None, :], "g2": gate_g2[l].astype(BF16), "k_k": k_k[l][None, :],
            "k_a": k_a[l][None, :], "gsum": gsum,
        }
        if has_vres:
            prep_consts["v0"] = vres_v0[l - 1][None, :]
            prep_consts["v2"] = jnp.pad(vres_v2[l - 1], ((0, LANES - vres_rank), (0, 0))).astype(BF16)
        q, k, v, gates, r_, lw_, k_, v_, a_, b_, g_ = _inproj_call(
            h, norm_g[l], mod, w_in.astype(BF16), gq, gk, gmean, v_first, prep_consts,
            rw=rw, hd=hd, decay_rank=decay_rank, tm=tm)
        if not has_vres:
            v_first = v_
        y_a = _rwkv_chunk_call(r_, lw_, k_, v_, a_, b_, g_, gmean, r_k[l].reshape(1, hd),
                               lnx_w[l][None, :], lnx_b[l][None, :], n_sub=RWKV_SUBCHUNKS)

        y_b = _attn_call(q, k, v, tq=tq_attn, ts=ts_attn, n_pairs=attn_pairs)

        branches = (y_a, y_b, gates, w_up_a[l].astype(BF16), w_up_b[l].astype(BF16), w_out[l].astype(BF16))
        h = _ffn_call(h, norm_g[l], mod, ffn2_wg[l].astype(BF16), ffn2_wu[l].astype(BF16),
                      ffn2_wd[l].astype(BF16), sub=2, tm=tm, tf=256, branches=branches)
    return h
```

```python
import functools
import math

import jax
import jax.numpy as jnp
from jax import lax
from jax.experimental import pallas as pl
from jax.experimental.pallas import tpu as pltpu

F32 = jnp.float32
BF16 = jnp.bfloat16

HEAD_DIM = 64
RMS_EPS = 1e-6
GN_EPS = 64e-5
N_MOD = 9
LANES = 128
MXU_WIDTH = 256
VMEM_LIMIT = 56 * 1024 * 1024
RWKV_CHUNK = 64
RWKV_SUBCHUNKS = 4

_NN = (((1,), (0,)), ((), ()))
_NT = (((1,), (1,)), ((), ()))


def _mm(a, b, dims=_NN):
    return lax.dot_general(a, b, dims, preferred_element_type=F32)


def _split2(x):
    hi = x.astype(BF16)
    lo = (x - hi.astype(F32)).astype(BF16)
    return hi, lo


def _split3(x):
    hi = x.astype(BF16)
    r1 = x - hi.astype(F32)
    mid = r1.astype(BF16)
    lo = (r1 - mid.astype(F32)).astype(BF16)
    return hi, mid, lo


def _mm3(a, b, dims=_NN):
    ah, al = _split2(a)
    bh, bl = _split2(b)
    return _mm(ah, bh, dims) + (_mm(ah, bl, dims) + _mm(al, bh, dims))


def _mm_lhs_exact(a_bf16, b):
    hi, mid, lo = _split3(b)
    return _mm(a_bf16, hi) + (_mm(a_bf16, mid) + _mm(a_bf16, lo))


def _groups(x_bf16, g):
    n = g.shape[0]
    slabs = [_mm(x_bf16[:, i * n:(i + 1) * n], g) for i in range(x_bf16.shape[1] // n)]
    return jnp.concatenate(slabs, axis=1)


def _params(*sem):
    return pltpu.CompilerParams(dimension_semantics=sem, vmem_limit_bytes=VMEM_LIMIT)


def _resident(shape):
    zeros = (0,) * len(shape)
    return pl.BlockSpec(shape, lambda *_: zeros, pipeline_mode=pl.Buffered(1))


def _norm_mod(x, g, sh, sc):
    ms = jnp.mean(x * x, axis=-1, keepdims=True)
    y = x * lax.rsqrt(ms + RMS_EPS) * g
    return y * (1.0 + sc) + sh


LOG2E = 1.4426950408889634
ATTN_DEAD_LOG2 = -160.0


def _log2(x):
    return jnp.log(x) * LOG2E


def _softplus(x):
    return jnp.maximum(x, 0.0) + jnp.log(1.0 + jnp.exp(-jnp.abs(x)))


def _mod_body(c_ref, w_ref, b_ref, o_ref):
    c = c_ref[...]
    ca = (c * jax.nn.sigmoid(c)).astype(BF16)
    o_ref[0] = _mm(ca, w_ref[0].astype(BF16)) + b_ref[0]


def _mod_call(c, mod_w, mod_b):
    n_layers, d, nd = mod_w.shape
    bsz = c.shape[0]
    tn = d
    return pl.pallas_call(
        _mod_body,
        grid=(n_layers, nd // tn),
        in_specs=[
            pl.BlockSpec((bsz, d), lambda l, j: (0, 0)),
            pl.BlockSpec((1, d, tn), lambda l, j: (l, 0, j)),
            pl.BlockSpec((1, 1, tn), lambda l, j: (l, 0, j)),
        ],
        out_specs=pl.BlockSpec((1, bsz, tn), lambda l, j: (l, 0, j)),
        out_shape=jax.ShapeDtypeStruct((n_layers, bsz, nd), F32),
        compiler_params=_params("parallel", "parallel"),
        name="adaln_mod",
    )(c, mod_w, mod_b.reshape(n_layers, 1, nd))


def _ffn_body(*refs, sub, tf, merge):
    if merge:
        h_ref, ya_ref, yb_ref, gate_ref, wa_ref, wb_ref, wo_ref = refs[:7]
        refs = refs[7:]
    else:
        h_ref = refs[0]
        refs = refs[1:]
    g_ref, mod_ref, wg_ref, wu_ref, wd_ref, o_ref = refs
    x = h_ref[0]
    if merge:
        d = x.shape[-1]
        ua = _mm(ya_ref[0], wa_ref[...])
        ub = _mm(yb_ref[0], wb_ref[...])
        merged = gate_ref[0, :, 0:d].astype(F32) * ua + gate_ref[0, :, d:2 * d].astype(F32) * ub
        x = x + mod_ref[0, 5:6, :] * _mm(merged.astype(BF16), wo_ref[...])
    g = g_ref[sub:sub + 1, :]
    sh = mod_ref[0, 3 * sub:3 * sub + 1, :]
    sc = mod_ref[0, 3 * sub + 1:3 * sub + 2, :]
    gt = mod_ref[0, 3 * sub + 2:3 * sub + 3, :]
    n = _norm_mod(x, g, sh, sc).astype(BF16)
    d_ff = wg_ref.shape[1]
    acc = jnp.zeros_like(x)
    for j in range(d_ff // tf):
        gj = _mm(n, wg_ref[:, j * tf:(j + 1) * tf])
        uj = _mm(n, wu_ref[:, j * tf:(j + 1) * tf])
        a = (gj * jax.nn.sigmoid(gj) * uj).astype(BF16)
        acc = acc + _mm(a, wd_ref[j * tf:(j + 1) * tf, :])
    o_ref[0] = x + (0.5 * gt) * acc


def _ffn_call(h, norm_g, mod, wg, wu, wd, *, sub, tm, tf, branches=None):
    bsz, slen, d = h.shape
    d_ff = wg.shape[1]

    def tok(width):
        return pl.BlockSpec((1, tm, width), lambda b, i: (b, i, 0))

    args, specs = [h], [tok(d)]
    if branches is not None:
        ya, yb, gates, wa, wb, wo = branches
        args += [ya, yb, gates, wa, wb, wo]
        specs += [tok(ya.shape[-1]), tok(yb.shape[-1]), tok(2 * d),
                  _resident(wa.shape), _resident(wb.shape), _resident(wo.shape)]
    args += [norm_g, mod, wg, wu, wd]
    specs += [_resident(norm_g.shape), pl.BlockSpec((1, N_MOD, d), lambda b, i: (b, 0, 0)),
              _resident((d, d_ff)), _resident((d, d_ff)), _resident((d_ff, d))]
    return pl.pallas_call(
        functools.partial(_ffn_body, sub=sub, tf=tf, merge=branches is not None),
        grid=(bsz, slen // tm),
        in_specs=specs,
        out_specs=tok(d),
        out_shape=jax.ShapeDtypeStruct(h.shape, F32),
        compiler_params=_params("parallel", "parallel"),
        name="merge_ffn" if branches is not None else "ffn",
    )(*args)


def _head_rms(t, g, gmean):
    ms = _groups((t * t).astype(BF16), gmean)
    return t * lax.rsqrt(ms + RMS_EPS) * g


def _rwkv_prep(p, prev, tail, v_first, c, *, hd, decay_rank):
    tm = p.shape[0]
    shifted = pltpu.roll(p, 1, axis=0)
    rows = lax.broadcasted_iota(jnp.int32, (tm, 1), 0)
    shifted = jnp.where(rows == 0, prev, shifted)
    p = p + (shifted - p) * c["mu"]
    r = p[:, 0:hd]
    k = p[:, hd:2 * hd]
    v = p[:, 2 * hd:3 * hd]
    wa = p[:, 3 * hd:3 * hd + LANES]
    gl = p[:, 3 * hd + LANES:3 * hd + 2 * LANES]
    lane = lax.broadcasted_iota(jnp.int32, (1, LANES), 1)
    wa = jnp.where(lane < decay_rank, jnp.tanh(wa), wa)
    lora = _mm3(wa, c["lora"])
    w = -_softplus(-(c["w0"] + lora[:, 0:hd])) - 0.5
    log_decay = -jnp.exp(w)
    a = jax.nn.sigmoid(c["a0"] + lora[:, hd:2 * hd])
    g = _mm(jax.nn.sigmoid(gl).astype(BF16), c["g2"])
    if tail is not None:
        mix = jax.nn.sigmoid(c["v0"] + _mm(tail.astype(BF16), c["v2"]))
        v = v + (v_first - v) * mix
    kk = k * c["k_k"]
    sq_hi, sq_lo = _split2(kk * kk)
    ss = _groups(sq_hi, c["gsum"]) + _groups(sq_lo, c["gsum"])
    kk = kk / jnp.maximum(jnp.sqrt(ss), 1e-12)
    return r, log_decay, k * (1.0 + (a - 1.0) * c["k_a"]), v, -kk, kk * a, g


_PREP_CONSTS = ("mu", "w0", "lora", "a0", "g2", "k_k", "k_a", "gsum")
_VRES_CONSTS = ("v0", "v2")


def _inproj_body(*refs, rw, hd, d, has_vres, decay_rank):
    names = ("h", "g", "mod", "w", "gq", "gk", "gmean") + (("vf",) if has_vres else ())
    names += _PREP_CONSTS + (_VRES_CONSTS if has_vres else ())
    ins = dict(zip(names, refs))
    q_ref, k_ref, v_ref, gate_ref = refs[len(names):len(names) + 4]
    rwkv_outs = refs[len(names) + 4:len(names) + 11]
    prev_ref = refs[len(names) + 11]
    i = pl.program_id(1)
    x = ins["h"][0]
    tm = x.shape[0]
    n = _norm_mod(x, ins["g"][1:2, :], ins["mod"][0, 3:4, :], ins["mod"][0, 4:5, :]).astype(BF16)
    w_ref = ins["w"]
    gmean = ins["gmean"][...]
    p = _mm(n, w_ref[:, 0:rw])
    og = rw + 3 * hd
    tail = _mm(n, w_ref[:, og + 2 * d:og + 2 * d + LANES]) if has_vres else None
    q = _mm(n, w_ref[:, rw:rw + hd])
    q_ref[0] = (_head_rms(q, ins["gq"][...], gmean) * (LOG2E / math.sqrt(HEAD_DIM))).astype(BF16)
    k = _mm(n, w_ref[:, rw + hd:rw + 2 * hd])
    k_ref[0] = _head_rms(k, ins["gk"][...], gmean).astype(BF16)
    v_ref[0] = _mm(n, w_ref[:, rw + 2 * hd:rw + 3 * hd]).astype(BF16)
    gate_ref[0] = jax.nn.sigmoid(_mm(n, w_ref[:, og:og + 2 * d])).astype(BF16)

    prev = prev_ref[7:8, :]
    prev = jnp.where(i == 0, jnp.zeros_like(prev), prev)
    prev_ref[...] = p[tm - 8:tm, :]
    consts = {name: ins[name][...] for name in _PREP_CONSTS + (_VRES_CONSTS if has_vres else ())}
    v_first = ins["vf"][0].astype(F32) if has_vres else None
    outs = _rwkv_prep(p, prev, tail, v_first, consts, hd=hd, decay_rank=decay_rank)
    for o_ref, val in zip(rwkv_outs, outs):
        o_ref[0] = val.astype(o_ref.dtype)


def _inproj_call(h, norm_g, mod, w, gq, gk, gmean, v_first, prep_consts, *, rw, hd, decay_rank, tm):
    bsz, slen, d = h.shape
    has_vres = v_first is not None

    def tok(width):
        return pl.BlockSpec((1, tm, width), lambda b, i: (b, i, 0))

    args = [h, norm_g, mod, w, gq, gk, gmean]
    specs = [tok(d), _resident(norm_g.shape), pl.BlockSpec((1, N_MOD, d), lambda b, i: (b, 0, 0)),
             _resident(w.shape), _resident(gq.shape), _resident(gk.shape), _resident(gmean.shape)]
    if has_vres:
        args.append(v_first)
        specs.append(tok(hd))
    names = _PREP_CONSTS + (_VRES_CONSTS if has_vres else ())
    args += [prep_consts[name] for name in names]
    specs += [_resident(prep_consts[name].shape) for name in names]
    outs = [(hd, BF16)] * 3 + [(2 * d, BF16)] + [(hd, F32 if j == 1 else BF16) for j in range(7)]
    return pl.pallas_call(
        functools.partial(_inproj_body, rw=rw, hd=hd, d=d, has_vres=has_vres, decay_rank=decay_rank),
        grid=(bsz, slen // tm),
        in_specs=specs,
        out_specs=[tok(wd) for wd, _ in outs],
        out_shape=[jax.ShapeDtypeStruct((bsz, slen, wd), dt) for wd, dt in outs],
        scratch_shapes=[pltpu.VMEM((8, rw), F32)],
        compiler_params=_params("parallel", "arbitrary"),
        name="in_proj",
    )(*args)


def _attn_body(q_ref, k_ref, v_ref, o_ref, *, tq, n_pairs):
    tk = LANES
    qi = pl.program_id(2)
    q0 = qi * tq
    pairs = range(n_pairs)
    lanes = [slice(p * LANES, (p + 1) * LANES) for p in pairs]
    lane = lax.broadcasted_iota(jnp.int32, (1, LANES), 1)
    first = lane < HEAD_DIM
    row = lax.broadcasted_iota(jnp.int32, (tq, 2 * tk), 0)
    col = lax.broadcasted_iota(jnp.int32, (tq, 2 * tk), 1)
    col = jnp.where(col >= tk, col - tk, col)
    jj = lax.broadcasted_iota(jnp.int32, (2 * tk, 2 * tk), 0)
    ss = lax.broadcasted_iota(jnp.int32, (2 * tk, 2 * tk), 1)
    jj = jnp.where(jj >= tk, jj - tk, jj)
    suffix_total = jnp.where((ss >= tk) | (jj > ss), -1.0, 0.0).astype(BF16)
    halves = [slice(e * tk, (e + 1) * tk) for e in range(2)]

    def both_heads(x):
        zero = jnp.zeros_like(x)
        return jnp.concatenate([jnp.where(first, x, zero), jnp.where(first, zero, x)], axis=0)

    def block(k0, carry, diag_off):
        r0 = 0 if diag_off is None else diag_off
        new = []
        for p in pairs:
            acc, logsurv = carry[2 * p][r0:], carry[2 * p + 1][r0:]
            k2 = both_heads(k_ref[0, pl.ds(k0, tk), lanes[p]])
            v2 = both_heads(v_ref[0, pl.ds(k0, tk), lanes[p]])
            z = _mm(q_ref[0, r0:, lanes[p]], k2, _NT)
            neg_abs = lax.bitcast_convert_type(
                lax.bitcast_convert_type(z, jnp.uint32) | jnp.uint32(0x80000000), F32)
            sp = jnp.maximum(z, 0.0) + _log2(1.0 + jnp.exp2(neg_abs))
            if diag_off is not None:
                causal = (col + diag_off < row)[r0:]
                sp = jnp.where(causal, sp, 0.0)
            hi, lo = _split2(sp)
            st = [_mm(jnp.concatenate([hi[:, h], lo[:, h]], axis=1), suffix_total) for h in halves]
            suffix = jnp.concatenate([st[0][:, :tk], st[1][:, :tk]], axis=1)
            total = jnp.concatenate([st[0][:, tk:], st[1][:, tk:]], axis=1)
            w = jnp.exp2((z - sp) + logsurv + suffix)
            if diag_off is not None:
                w = jnp.where(causal, w, 0.0)
            acc = acc + _mm(w.astype(BF16), v2)
            logsurv = logsurv + total
            if r0:
                acc = jnp.concatenate([carry[2 * p][:r0], acc], axis=0)
                logsurv = jnp.concatenate([carry[2 * p + 1][:r0], logsurv], axis=0)
            new += [acc, logsurv]
        return tuple(new)

    carry = (jnp.zeros((tq, LANES), F32), jnp.zeros((tq, 2 * tk), F32)) * n_pairs
    n_diag = tq // tk
    for j in reversed(range(n_diag)):
        carry = block(pl.multiple_of(q0 + j * tk, tk), carry, j * tk)

    def alive(carry):
        worst = carry[1]
        for p in range(1, n_pairs):
            worst = jnp.maximum(worst, carry[2 * p + 1])
        return (jnp.max(worst) > ATTN_DEAD_LOG2).astype(jnp.int32)

    n_off = qi * n_diag

    def cond(state):
        return (state[0] < n_off) & (state[1] > 0)

    def body(state):
        i = state[0]
        carry = block(pl.multiple_of(q0 - (i + 1) * tk, tk), state[2:], None)
        return (i + 1, alive(carry)) + carry

    carry = lax.while_loop(cond, body, (jnp.int32(0), alive(carry)) + carry)[2:]
    for p in pairs:
        o_ref[0, :, lanes[p]] = carry[2 * p].astype(o_ref.dtype)


def _attn_call(q, k, v, *, tq, n_pairs):
    bsz, slen, hd = q.shape
    width = n_pairs * LANES
    qblk = pl.BlockSpec((1, tq, width), lambda b, p, i: (b, i, p))
    kvblk = pl.BlockSpec((1, slen, width), lambda b, p, i: (b, 0, p))
    return pl.pallas_call(
        functools.partial(_attn_body, tq=tq, n_pairs=n_pairs),
        grid=(bsz, hd // width, slen // tq),
        in_specs=[qblk, kvblk, kvblk],
        out_specs=qblk,
        out_shape=jax.ShapeDtypeStruct(q.shape, BF16),
        compiler_params=_params("parallel", "parallel", "arbitrary"),
        name="stick_breaking_attn",
    )(q, k, v)


def _rwkv_chunk_body(r_ref, lw_ref, k_ref, v_ref, a_ref, b_ref, g_ref, tri_ref, gmean_ref,
                     rk_ref, lnw_ref, lnb_ref, o_ref, st_ref, *, n_sub):
    c = pl.program_id(1)

    @pl.when(c == 0)
    def _():
        st_ref[...] = jnp.zeros_like(st_ref)

    r, lw, k, v = r_ref[0].astype(F32), lw_ref[0], k_ref[0].astype(F32), v_ref[0].astype(F32)
    a, b = a_ref[0].astype(F32), b_ref[0].astype(F32)
    ch = RWKV_CHUNK
    n_pairs = r.shape[1] // LANES
    cum = _mm_lhs_exact(tri_ref[...], lw)
    cum_prev = cum - lw
    lasts = [cum[(s + 1) * ch - 1:(s + 1) * ch, :] for s in range(n_sub)]
    last = jnp.concatenate([jnp.broadcast_to(t, (ch, t.shape[1])) for t in lasts], axis=0)
    e_pos = jnp.exp(cum)
    e_neg = jnp.exp(-cum)
    e_end = jnp.exp(last - cum)
    at = a * jnp.exp(cum_prev)
    rt = r * e_pos
    bt = b * e_neg
    kt = k * e_neg
    bh = b * e_end
    kh = k * e_end
    p_end = [jnp.exp(t) for t in lasts]

    lane = lax.broadcasted_iota(jnp.int32, (1, LANES), 1)
    first = lane < HEAD_DIM

    def stacked(x, s, p):
        x = x[s * ch:(s + 1) * ch, p * LANES:(p + 1) * LANES]
        zero = jnp.zeros_like(x)
        return jnp.concatenate([jnp.where(first, x, zero), jnp.where(first, zero, x)], axis=0)

    row = lax.broadcasted_iota(jnp.int32, (2 * ch, 2 * ch), 0)
    col = lax.broadcasted_iota(jnp.int32, (2 * ch, 2 * ch), 1)
    strict = (row & (ch - 1)) > (col & (ch - 1))
    row2 = lax.broadcasted_iota(jnp.int32, (2 * ch, 4 * ch), 0)
    col2 = lax.broadcasted_iota(jnp.int32, (2 * ch, 4 * ch), 1)
    lower2 = (row2 & (ch - 1)) >= (col2 & (ch - 1))
    diag = row == col
    eye = jnp.where(diag, 1.0, 0.0)
    n_levels = int(math.log2(ch))

    def sub_diag(level):
        size, half = 1 << level, 1 << (level - 1)
        inside = (row >> level) == (col >> level)
        return inside & ((row & (size - 1)) >= half) & ((col & (size - 1)) < half)

    units = [(s, p) for s in range(n_sub) for p in range(n_pairs)]
    at_s, rt_s, bt_s, kt_s, bh_s, kh_s, v_s = (
        [stacked(t, s, p) for s, p in units] for t in (at, rt, bt, kt, bh, kh, v))
    idx = range(len(units))
    prod = [_mm(jnp.concatenate([at_s[u], rt_s[u]], axis=0).astype(BF16),
                jnp.concatenate([bt_s[u], kt_s[u]], axis=0).astype(BF16), _NT) for u in idx]
    a_ab = [jnp.where(strict, prod[u][:2 * ch, :2 * ch], 0.0) for u in idx]
    a_ak = [jnp.where(strict, prod[u][:2 * ch, 2 * ch:], 0.0).astype(BF16) for u in idx]
    a_rbk = [jnp.where(lower2, prod[u][2 * ch:, :], 0.0).astype(BF16) for u in idx]
    lvl = sub_diag(1)
    tinv = [eye + jnp.where(lvl, a_ab[u], 0.0) for u in idx]
    for level in range(2, n_levels + 1):
        lvl = sub_diag(level)
        tinv_b = [tinv[u].astype(BF16) for u in idx]
        right = [_mm(jnp.where(lvl, a_ab[u], 0.0).astype(BF16), tinv_b[u]) for u in idx]
        tinv = [tinv[u] + _mm(tinv_b[u], right[u].astype(BF16)) for u in idx]
    v_b = [v_s[u].astype(BF16) for u in idx]
    akv = [_mm(a_ak[u], v_b[u]) for u in idx]
    wx = [_mm(tinv[u].astype(BF16),
              jnp.concatenate([at_s[u], akv[u]], axis=1).astype(BF16)) for u in idx]
    zero_b = jnp.zeros((2 * ch, LANES), BF16)
    wxv = [jnp.concatenate([wx[u].astype(BF16), jnp.concatenate([zero_b, v_b[u]], axis=1)], axis=0)
           for u in idx]
    qy = [_mm(a_rbk[u], wxv[u]) for u in idx]
    mn = [_mm(jnp.concatenate([bh_s[u], kh_s[u]], axis=0).T.astype(BF16), wxv[u]) for u in idx]
    ys = []
    for s in range(n_sub):
        rows = []
        for p in range(n_pairs):
            u = s * n_pairs + p
            st = st_ref[p]
            y_st = _mm((rt_s[u] + qy[u][:, :LANES]).astype(BF16), st.astype(BF16)) + qy[u][:, LANES:]
            rows.append(y_st[:ch] + y_st[ch:])
            m_t = jnp.where(diag, p_end[s][:, p * LANES:(p + 1) * LANES], 0.0) + mn[u][:, :LANES]
            st_ref[p] = _mm3(m_t, st) + mn[u][:, LANES:]
        ys.append(jnp.concatenate(rows, axis=1))
    y = jnp.concatenate(ys, axis=0)

    gmean = gmean_ref[...]
    y_hi, y_lo = _split2(y)
    mean = _groups(y_hi, gmean) + _groups(y_lo, gmean)
    dev = y - mean
    var = _groups((dev * dev).astype(BF16), gmean)
    yn = dev * lax.rsqrt(var + GN_EPS) * lnw_ref[...] + lnb_ref[...]
    bonus = _groups((r * k * rk_ref[...]).astype(BF16), gmean) * float(HEAD_DIM) * v
    o_ref[0] = ((yn + bonus) * g_ref[0].astype(F32)).astype(o_ref.dtype)


def _rwkv_chunk_call(r, lw, k, v, a, b, g, gmean, rk, lnw, lnb, *, n_sub):
    bsz, slen, hd = r.shape
    rows = n_sub * RWKV_CHUNK
    idx = jnp.arange(rows)
    same_chunk = idx[:, None] // RWKV_CHUNK == idx[None, :] // RWKV_CHUNK
    tri = (same_chunk & (idx[:, None] >= idx[None, :])).astype(BF16)
    tok = pl.BlockSpec((1, rows, hd), lambda bi, ci: (bi, ci, 0))
    consts = [tri, gmean, rk, lnw, lnb]
    return pl.pallas_call(
        functools.partial(_rwkv_chunk_body, n_sub=n_sub),
        grid=(bsz, slen // rows),
        in_specs=[tok] * 7 + [_resident(c.shape) for c in consts],
        out_specs=tok,
        out_shape=jax.ShapeDtypeStruct((bsz, slen, hd), BF16),
        scratch_shapes=[pltpu.VMEM((hd // LANES, LANES, LANES), F32)],
        compiler_params=_params("parallel", "arbitrary"),
        name="rwkv_chunk_scan",
    )(r, lw, k, v, a, b, g, *consts)


def _block_diag_const(width, group, value, dtype):
    idx = jnp.arange(width) // group
    return jnp.where(idx[:, None] == idx[None, :], value, 0.0).astype(dtype)


def kernel(x, c, mod_w, mod_b, norm_g, ffn1_wg, ffn1_wu, ffn1_wd, w_in_first, w_in_rest, shift_mu, decay_w0, decay_w2, aaa_a0, aaa_a2, gate_g2, k_k, k_a, r_k, lnx_w, lnx_b, vres_v0, vres_v2, q_norm_g, k_norm_g, w_up_a, w_up_b, w_out, ffn2_wg, ffn2_wu, ffn2_wd):
    bsz, slen, d = x.shape
    n_layers = mod_w.shape[0]
    hd = decay_w0.shape[-1]
    n_heads = hd // HEAD_DIM
    decay_rank, aaa_rank = decay_w2.shape[1], aaa_a2.shape[1]
    gate_rank, vres_rank = gate_g2.shape[1], vres_v2.shape[1]
    assert decay_rank + aaa_rank == LANES and gate_rank == LANES and vres_rank <= LANES
    assert w_up_b.shape[1] == hd and hd % LANES == 0
    rw = 3 * hd + decay_rank + aaa_rank + gate_rank
    cols_first = rw + 3 * hd + 2 * d
    assert w_in_first.shape[1] == cols_first

    tm = min(512, slen)
    tq_attn = min(512, slen)

    gmean = _block_diag_const(MXU_WIDTH, HEAD_DIM, 1.0 / HEAD_DIM, BF16)
    gsum = _block_diag_const(MXU_WIDTH, HEAD_DIM, 1.0, BF16)

    mod_all = _mod_call(c, mod_w, mod_b).reshape(n_layers, bsz, N_MOD, d)

    h = x
    v_first = None
    for l in range(n_layers):
        mod = mod_all[l]
        h = _ffn_call(h, norm_g[l], mod, ffn1_wg[l].astype(BF16), ffn1_wu[l].astype(BF16),
                      ffn1_wd[l].astype(BF16), sub=0, tm=tm, tf=256)

        has_vres = l > 0
        if has_vres:
            w_in = jnp.pad(w_in_rest[l - 1], ((0, 0), (0, LANES - vres_rank)))
        else:
            w_in = w_in_first
        gq = jnp.tile(q_norm_g[l], n_heads)[None, :]
        gk = jnp.tile(k_norm_g[l], n_heads)[None, :]
        lora_w = jnp.zeros((LANES, 2 * hd), F32)
        lora_w = lora_w.at[:decay_rank, :hd].set(decay_w2[l]).at[decay_rank:, hd:].set(aaa_a2[l])
        prep_consts = {
            "mu": shift_mu[l][None, :], "w0": decay_w0[l][None, :], "lora": lora_w,
            "a0": aaa_a0[l][None, :], "g2": gate_g2[l].astype(BF16), "k_k": k_k[l][None, :],
            "k_a": k_a[l][None, :], "gsum": gsum,
        }
        if has_vres:
            prep_consts["v0"] = vres_v0[l - 1][None, :]
            prep_consts["v2"] = jnp.pad(vres_v2[l - 1], ((0, LANES - vres_rank), (0, 0))).astype(BF16)
        q, k, v, gates, r_, lw_, k_, v_, a_, b_, g_ = _inproj_call(
            h, norm_g[l], mod, w_in.astype(BF16), gq, gk, gmean, v_first, prep_consts,
            rw=rw, hd=hd, decay_rank=decay_rank, tm=tm)
        if not has_vres:
            v_first = v_
        y_a = _rwkv_chunk_call(r_, lw_, k_, v_, a_, b_, g_, gmean, r_k[l].reshape(1, hd),
                               lnx_w[l][None, :], lnx_b[l][None, :], n_sub=RWKV_SUBCHUNKS)

        y_b = _attn_call(q, k, v, tq=tq_attn, n_pairs=hd // LANES)

        branches = (y_a, y_b, gates, w_up_a[l].astype(BF16), w_up_b[l].astype(BF16), w_out[l].astype(BF16))
        h = _ffn_call(h, norm_g[l], mod, ffn2_wg[l].astype(BF16), ffn2_wu[l].astype(BF16),
                      ffn2_wd[l].astype(BF16), sub=2, tm=tm, tf=256, branches=branches)
    return h
```

```python
import functools
import math

import jax
import jax.numpy as jnp
from jax import lax
from jax.experimental import pallas as pl
from jax.experimental.pallas import tpu as pltpu

F32 = jnp.float32
BF16 = jnp.bfloat16

HEAD_DIM = 64
RMS_EPS = 1e-6
GN_EPS = 64e-5
N_MOD = 9
LANES = 128
MXU_WIDTH = 256
VMEM_LIMIT = 56 * 1024 * 1024
RWKV_CHUNK = 64
RWKV_SUBCHUNKS = 4

_NN = (((1,), (0,)), ((), ()))
_NT = (((1,), (1,)), ((), ()))


def _mm(a, b, dims=_NN):
    return lax.dot_general(a, b, dims, preferred_element_type=F32)


def _split2(x):
    hi = x.astype(BF16)
    lo = (x - hi.astype(F32)).astype(BF16)
    return hi, lo


def _split3(x):
    hi = x.astype(BF16)
    r1 = x - hi.astype(F32)
    mid = r1.astype(BF16)
    lo = (r1 - mid.astype(F32)).astype(BF16)
    return hi, mid, lo


def _mm3(a, b, dims=_NN):
    ah, al = _split2(a)
    bh, bl = _split2(b)
    return _mm(ah, bh, dims) + (_mm(ah, bl, dims) + _mm(al, bh, dims))


def _mm_lhs_exact(a_bf16, b):
    hi, mid, lo = _split3(b)
    return _mm(a_bf16, hi) + (_mm(a_bf16, mid) + _mm(a_bf16, lo))


def _groups(x_bf16, g):
    n = g.shape[0]
    slabs = [_mm(x_bf16[:, i * n:(i + 1) * n], g) for i in range(x_bf16.shape[1] // n)]
    return jnp.concatenate(slabs, axis=1)


def _params(*sem):
    return pltpu.CompilerParams(dimension_semantics=sem, vmem_limit_bytes=VMEM_LIMIT)


def _resident(shape):
    zeros = (0,) * len(shape)
    return pl.BlockSpec(shape, lambda *_: zeros, pipeline_mode=pl.Buffered(1))


def _norm_mod(x, g, sh, sc):
    ms = jnp.mean(x * x, axis=-1, keepdims=True)
    y = x * lax.rsqrt(ms + RMS_EPS) * g
    return y * (1.0 + sc) + sh


LOG2E = 1.4426950408889634


def _log2(x):
    return jnp.log(x) * LOG2E


def _softplus(x):
    return jnp.maximum(x, 0.0) + jnp.log(1.0 + jnp.exp(-jnp.abs(x)))


def _mod_body(c_ref, w_ref, b_ref, o_ref):
    c = c_ref[...]
    ca = (c * jax.nn.sigmoid(c)).astype(BF16)
    o_ref[0] = _mm(ca, w_ref[0].astype(BF16)) + b_ref[0]


def _mod_call(c, mod_w, mod_b):
    n_layers, d, nd = mod_w.shape
    bsz = c.shape[0]
    tn = d
    return pl.pallas_call(
        _mod_body,
        grid=(n_layers, nd // tn),
        in_specs=[
            pl.BlockSpec((bsz, d), lambda l, j: (0, 0)),
            pl.BlockSpec((1, d, tn), lambda l, j: (l, 0, j)),
            pl.BlockSpec((1, 1, tn), lambda l, j: (l, 0, j)),
        ],
        out_specs=pl.BlockSpec((1, bsz, tn), lambda l, j: (l, 0, j)),
        out_shape=jax.ShapeDtypeStruct((n_layers, bsz, nd), F32),
        compiler_params=_params("parallel", "parallel"),
        name="adaln_mod",
    )(c, mod_w, mod_b.reshape(n_layers, 1, nd))


def _ffn_body(*refs, sub, tf, merge):
    if merge:
        h_ref, ya_ref, yb_ref, gate_ref, wa_ref, wb_ref, wo_ref = refs[:7]
        refs = refs[7:]
    else:
        h_ref = refs[0]
        refs = refs[1:]
    g_ref, mod_ref, wg_ref, wu_ref, wd_ref, o_ref = refs
    x = h_ref[0]
    if merge:
        d = x.shape[-1]
        ua = _mm(ya_ref[0], wa_ref[...])
        ub = _mm(yb_ref[0], wb_ref[...])
        merged = gate_ref[0, :, 0:d].astype(F32) * ua + gate_ref[0, :, d:2 * d].astype(F32) * ub
        x = x + mod_ref[0, 5:6, :] * _mm(merged.astype(BF16), wo_ref[...])
    g = g_ref[sub:sub + 1, :]
    sh = mod_ref[0, 3 * sub:3 * sub + 1, :]
    sc = mod_ref[0, 3 * sub + 1:3 * sub + 2, :]
    gt = mod_ref[0, 3 * sub + 2:3 * sub + 3, :]
    n = _norm_mod(x, g, sh, sc).astype(BF16)
    d_ff = wg_ref.shape[1]
    acc = jnp.zeros_like(x)
    for j in range(d_ff // tf):
        gj = _mm(n, wg_ref[:, j * tf:(j + 1) * tf])
        uj = _mm(n, wu_ref[:, j * tf:(j + 1) * tf])
        a = (gj * jax.nn.sigmoid(gj) * uj).astype(BF16)
        acc = acc + _mm(a, wd_ref[j * tf:(j + 1) * tf, :])
    o_ref[0] = x + (0.5 * gt) * acc


def _ffn_call(h, norm_g, mod, wg, wu, wd, *, sub, tm, tf, branches=None):
    bsz, slen, d = h.shape
    d_ff = wg.shape[1]

    def tok(width):
        return pl.BlockSpec((1, tm, width), lambda b, i: (b, i, 0))

    args, specs = [h], [tok(d)]
    if branches is not None:
        ya, yb, gates, wa, wb, wo = branches
        args += [ya, yb, gates, wa, wb, wo]
        specs += [tok(ya.shape[-1]), tok(yb.shape[-1]), tok(2 * d),
                  _resident(wa.shape), _resident(wb.shape), _resident(wo.shape)]
    args += [norm_g, mod, wg, wu, wd]
    specs += [_resident(norm_g.shape), pl.BlockSpec((1, N_MOD, d), lambda b, i: (b, 0, 0)),
              _resident((d, d_ff)), _resident((d, d_ff)), _resident((d_ff, d))]
    return pl.pallas_call(
        functools.partial(_ffn_body, sub=sub, tf=tf, merge=branches is not None),
        grid=(bsz, slen // tm),
        in_specs=specs,
        out_specs=tok(d),
        out_shape=jax.ShapeDtypeStruct(h.shape, F32),
        compiler_params=_params("parallel", "parallel"),
        name="merge_ffn" if branches is not None else "ffn",
    )(*args)


def _head_rms(t, g, gmean):
    ms = _groups((t * t).astype(BF16), gmean)
    return t * lax.rsqrt(ms + RMS_EPS) * g


def _rwkv_prep(p, prev, tail, v_first, c, *, hd, decay_rank):
    tm = p.shape[0]
    shifted = pltpu.roll(p, 1, axis=0)
    rows = lax.broadcasted_iota(jnp.int32, (tm, 1), 0)
    shifted = jnp.where(rows == 0, prev, shifted)
    p = p + (shifted - p) * c["mu"]
    r = p[:, 0:hd]
    k = p[:, hd:2 * hd]
    v = p[:, 2 * hd:3 * hd]
    wa = p[:, 3 * hd:3 * hd + LANES]
    gl = p[:, 3 * hd + LANES:3 * hd + 2 * LANES]
    lane = lax.broadcasted_iota(jnp.int32, (1, LANES), 1)
    wa = jnp.where(lane < decay_rank, jnp.tanh(wa), wa)
    lora = _mm3(wa, c["lora"])
    w = -_softplus(-(c["w0"] + lora[:, 0:hd])) - 0.5
    log_decay = -jnp.exp(w)
    a = jax.nn.sigmoid(c["a0"] + lora[:, hd:2 * hd])
    g = _mm(jax.nn.sigmoid(gl).astype(BF16), c["g2"])
    if tail is not None:
        mix = jax.nn.sigmoid(c["v0"] + _mm(tail.astype(BF16), c["v2"]))
        v = v + (v_first - v) * mix
    kk = k * c["k_k"]
    sq_hi, sq_lo = _split2(kk * kk)
    ss = _groups(sq_hi, c["gsum"]) + _groups(sq_lo, c["gsum"])
    kk = kk / jnp.maximum(jnp.sqrt(ss), 1e-12)
    return r, log_decay, k * (1.0 + (a - 1.0) * c["k_a"]), v, -kk, kk * a, g


_PREP_CONSTS = ("mu", "w0", "lora", "a0", "g2", "k_k", "k_a", "gsum")
_VRES_CONSTS = ("v0", "v2")


def _inproj_body(*refs, rw, hd, d, has_vres, decay_rank):
    names = ("h", "g", "mod", "w", "gq", "gk", "gmean") + (("vf",) if has_vres else ())
    names += _PREP_CONSTS + (_VRES_CONSTS if has_vres else ())
    ins = dict(zip(names, refs))
    q_ref, k_ref, v_ref, gate_ref = refs[len(names):len(names) + 4]
    rwkv_outs = refs[len(names) + 4:len(names) + 11]
    prev_ref = refs[len(names) + 11]
    i = pl.program_id(1)
    x = ins["h"][0]
    tm = x.shape[0]
    n = _norm_mod(x, ins["g"][1:2, :], ins["mod"][0, 3:4, :], ins["mod"][0, 4:5, :]).astype(BF16)
    w_ref = ins["w"]
    gmean = ins["gmean"][...]
    p = _mm(n, w_ref[:, 0:rw])
    og = rw + 3 * hd
    tail = _mm(n, w_ref[:, og + 2 * d:og + 2 * d + LANES]) if has_vres else None
    q = _mm(n, w_ref[:, rw:rw + hd])
    q_ref[0] = (_head_rms(q, ins["gq"][...], gmean) * (LOG2E / math.sqrt(HEAD_DIM))).astype(BF16)
    k = _mm(n, w_ref[:, rw + hd:rw + 2 * hd])
    k_ref[0] = _head_rms(k, ins["gk"][...], gmean).astype(BF16)
    v_ref[0] = _mm(n, w_ref[:, rw + 2 * hd:rw + 3 * hd]).astype(BF16)
    gate_ref[0] = jax.nn.sigmoid(_mm(n, w_ref[:, og:og + 2 * d])).astype(BF16)

    prev = prev_ref[7:8, :]
    prev = jnp.where(i == 0, jnp.zeros_like(prev), prev)
    prev_ref[...] = p[tm - 8:tm, :]
    consts = {name: ins[name][...] for name in _PREP_CONSTS + (_VRES_CONSTS if has_vres else ())}
    v_first = ins["vf"][0].astype(F32) if has_vres else None
    outs = _rwkv_prep(p, prev, tail, v_first, consts, hd=hd, decay_rank=decay_rank)
    for o_ref, val in zip(rwkv_outs, outs):
        o_ref[0] = val.astype(o_ref.dtype)


def _inproj_call(h, norm_g, mod, w, gq, gk, gmean, v_first, prep_consts, *, rw, hd, decay_rank, tm):
    bsz, slen, d = h.shape
    has_vres = v_first is not None

    def tok(width):
        return pl.BlockSpec((1, tm, width), lambda b, i: (b, i, 0))

    args = [h, norm_g, mod, w, gq, gk, gmean]
    specs = [tok(d), _resident(norm_g.shape), pl.BlockSpec((1, N_MOD, d), lambda b, i: (b, 0, 0)),
             _resident(w.shape), _resident(gq.shape), _resident(gk.shape), _resident(gmean.shape)]
    if has_vres:
        args.append(v_first)
        specs.append(tok(hd))
    names = _PREP_CONSTS + (_VRES_CONSTS if has_vres else ())
    args += [prep_consts[name] for name in names]
    specs += [_resident(prep_consts[name].shape) for name in names]
    outs = [(hd, BF16)] * 3 + [(2 * d, BF16)] + [(hd, F32 if j == 1 else BF16) for j in range(7)]
    return pl.pallas_call(
        functools.partial(_inproj_body, rw=rw, hd=hd, d=d, has_vres=has_vres, decay_rank=decay_rank),
        grid=(bsz, slen // tm),
        in_specs=specs,
        out_specs=[tok(wd) for wd, _ in outs],
        out_shape=[jax.ShapeDtypeStruct((bsz, slen, wd), dt) for wd, dt in outs],
        scratch_shapes=[pltpu.VMEM((8, rw), F32)],
        compiler_params=_params("parallel", "arbitrary"),
        name="in_proj",
    )(*args)


def _attn_body(q_ref, k_ref, v_ref, o_ref, *, tq, n_pairs):
    tk = LANES
    qi = pl.program_id(2)
    q0 = qi * tq
    pairs = range(n_pairs)
    lanes = [slice(p * LANES, (p + 1) * LANES) for p in pairs]
    lane = lax.broadcasted_iota(jnp.int32, (1, LANES), 1)
    first = lane < HEAD_DIM
    row = lax.broadcasted_iota(jnp.int32, (tq, 2 * tk), 0)
    col = lax.broadcasted_iota(jnp.int32, (tq, 2 * tk), 1)
    col = jnp.where(col >= tk, col - tk, col)
    jj = lax.broadcasted_iota(jnp.int32, (2 * tk, 2 * tk), 0)
    ss = lax.broadcasted_iota(jnp.int32, (2 * tk, 2 * tk), 1)
    jj = jnp.where(jj >= tk, jj - tk, jj)
    suffix_total = jnp.where((ss >= tk) | (jj > ss), -1.0, 0.0).astype(BF16)
    halves = [slice(e * tk, (e + 1) * tk) for e in range(2)]

    def both_heads(x):
        zero = jnp.zeros_like(x)
        return jnp.concatenate([jnp.where(first, x, zero), jnp.where(first, zero, x)], axis=0)

    def block(k0, carry, diag_off):
        r0 = 0 if diag_off is None else diag_off
        new = []
        for p in pairs:
            acc, logsurv = carry[2 * p][r0:], carry[2 * p + 1][r0:]
            k2 = both_heads(k_ref[0, pl.ds(k0, tk), lanes[p]])
            v2 = both_heads(v_ref[0, pl.ds(k0, tk), lanes[p]])
            z = _mm(q_ref[0, r0:, lanes[p]], k2, _NT)
            neg_abs = lax.bitcast_convert_type(
                lax.bitcast_convert_type(z, jnp.uint32) | jnp.uint32(0x80000000), F32)
            sp = jnp.maximum(z, 0.0) + _log2(1.0 + jnp.exp2(neg_abs))
            if diag_off is not None:
                causal = (col + diag_off < row)[r0:]
                sp = jnp.where(causal, sp, 0.0)
            hi, lo = _split2(sp)
            st = [_mm(jnp.concatenate([hi[:, h], lo[:, h]], axis=1), suffix_total) for h in halves]
            suffix = jnp.concatenate([st[0][:, :tk], st[1][:, :tk]], axis=1)
            total = jnp.concatenate([st[0][:, tk:], st[1][:, tk:]], axis=1)
            w = jnp.exp2((z - sp) + logsurv + suffix)
            if diag_off is not None:
                w = jnp.where(causal, w, 0.0)
            acc = acc + _mm(w.astype(BF16), v2)
            logsurv = logsurv + total
            if r0:
                acc = jnp.concatenate([carry[2 * p][:r0], acc], axis=0)
                logsurv = jnp.concatenate([carry[2 * p + 1][:r0], logsurv], axis=0)
            new += [acc, logsurv]
        return tuple(new)

    carry = (jnp.zeros((tq, LANES), F32), jnp.zeros((tq, 2 * tk), F32)) * n_pairs
    n_diag = tq // tk
    for j in reversed(range(n_diag)):
        carry = block(pl.multiple_of(q0 + j * tk, tk), carry, j * tk)

    def body(i, carry):
        return block(pl.multiple_of(q0 - (i + 1) * tk, tk), carry, None)

    carry = lax.fori_loop(0, qi * n_diag, body, carry)
    for p in pairs:
        o_ref[0, :, lanes[p]] = carry[2 * p].astype(o_ref.dtype)


def _attn_call(q, k, v, *, tq, n_pairs):
    bsz, slen, hd = q.shape
    width = n_pairs * LANES
    qblk = pl.BlockSpec((1, tq, width), lambda b, p, i: (b, i, p))
    kvblk = pl.BlockSpec((1, slen, width), lambda b, p, i: (b, 0, p))
    return pl.pallas_call(
        functools.partial(_attn_body, tq=tq, n_pairs=n_pairs),
        grid=(bsz, hd // width, slen // tq),
        in_specs=[qblk, kvblk, kvblk],
        out_specs=qblk,
        out_shape=jax.ShapeDtypeStruct(q.shape, BF16),
        compiler_params=_params("parallel", "parallel", "arbitrary"),
        name="stick_breaking_attn",
    )(q, k, v)


def _rwkv_chunk_body(r_ref, lw_ref, k_ref, v_ref, a_ref, b_ref, g_ref, tri_ref, gmean_ref,
                     rk_ref, lnw_ref, lnb_ref, o_ref, st_ref, *, n_sub):
    c = pl.program_id(1)

    @pl.when(c == 0)
    def _():
        st_ref[...] = jnp.zeros_like(st_ref)

    r, lw, k, v = r_ref[0].astype(F32), lw_ref[0], k_ref[0].astype(F32), v_ref[0].astype(F32)
    a, b = a_ref[0].astype(F32), b_ref[0].astype(F32)
    ch = RWKV_CHUNK
    n_pairs = r.shape[1] // LANES
    cum = _mm_lhs_exact(tri_ref[...], lw)
    cum_prev = cum - lw
    lasts = [cum[(s + 1) * ch - 1:(s + 1) * ch, :] for s in range(n_sub)]
    last = jnp.concatenate([jnp.broadcast_to(t, (ch, t.shape[1])) for t in lasts], axis=0)
    e_pos = jnp.exp(cum)
    e_neg = jnp.exp(-cum)
    e_end = jnp.exp(last - cum)
    at = a * jnp.exp(cum_prev)
    rt = r * e_pos
    bt = b * e_neg
    kt = k * e_neg
    bh = b * e_end
    kh = k * e_end
    p_end = [jnp.exp(t) for t in lasts]

    lane = lax.broadcasted_iota(jnp.int32, (1, LANES), 1)
    first = lane < HEAD_DIM

    def stacked(x, s, p):
        x = x[s * ch:(s + 1) * ch, p * LANES:(p + 1) * LANES]
        zero = jnp.zeros_like(x)
        return jnp.concatenate([jnp.where(first, x, zero), jnp.where(first, zero, x)], axis=0)

    row = lax.broadcasted_iota(jnp.int32, (2 * ch, 2 * ch), 0)
    col = lax.broadcasted_iota(jnp.int32, (2 * ch, 2 * ch), 1)
    strict = (row & (ch - 1)) > (col & (ch - 1))
    row2 = lax.broadcasted_iota(jnp.int32, (2 * ch, 4 * ch), 0)
    col2 = lax.broadcasted_iota(jnp.int32, (2 * ch, 4 * ch), 1)
    lower2 = (row2 & (ch - 1)) >= (col2 & (ch - 1))
    diag = row == col
    eye = jnp.where(diag, 1.0, 0.0)
    n_levels = int(math.log2(ch))

    def sub_diag(level):
        size, half = 1 << level, 1 << (level - 1)
        inside = (row >> level) == (col >> level)
        return inside & ((row & (size - 1)) >= half) & ((col & (size - 1)) < half)

    units = [(s, p) for s in range(n_sub) for p in range(n_pairs)]
    at_s, rt_s, bt_s, kt_s, bh_s, kh_s, v_s = (
        [stacked(t, s, p) for s, p in units] for t in (at, rt, bt, kt, bh, kh, v))
    idx = range(len(units))
    prod = [_mm(jnp.concatenate([at_s[u], rt_s[u]], axis=0).astype(BF16),
                jnp.concatenate([bt_s[u], kt_s[u]], axis=0).astype(BF16), _NT) for u in idx]
    a_ab = [jnp.where(strict, prod[u][:2 * ch, :2 * ch], 0.0) for u in idx]
    a_ak = [jnp.where(strict, prod[u][:2 * ch, 2 * ch:], 0.0).astype(BF16) for u in idx]
    a_rbk = [jnp.where(lower2, prod[u][2 * ch:, :], 0.0).astype(BF16) for u in idx]
    lvl = sub_diag(1)
    tinv = [eye + jnp.where(lvl, a_ab[u], 0.0) for u in idx]
    for level in range(2, n_levels + 1):
        lvl = sub_diag(level)
        tinv_b = [tinv[u].astype(BF16) for u in idx]
        right = [_mm(jnp.where(lvl, a_ab[u], 0.0).astype(BF16), tinv_b[u]) for u in idx]
        tinv = [tinv[u] + _mm(tinv_b[u], right[u].astype(BF16)) for u in idx]
    v_b = [v_s[u].astype(BF16) for u in idx]
    akv = [_mm(a_ak[u], v_b[u]) for u in idx]
    wx = [_mm(tinv[u].astype(BF16),
              jnp.concatenate([at_s[u], akv[u]], axis=1).astype(BF16)) for u in idx]
    zero_b = jnp.zeros((2 * ch, LANES), BF16)
    wxv = [jnp.concatenate([wx[u].astype(BF16), jnp.concatenate([zero_b, v_b[u]], axis=1)], axis=0)
           for u in idx]
    qy = [_mm(a_rbk[u], wxv[u]) for u in idx]
    mn = [_mm(jnp.concatenate([bh_s[u], kh_s[u]], axis=0).T.astype(BF16), wxv[u]) for u in idx]
    ys = []
    for s in range(n_sub):
        rows = []
        for p in range(n_pairs):
            u = s * n_pairs + p
            st = st_ref[p]
            y_st = _mm((rt_s[u] + qy[u][:, :LANES]).astype(BF16), st.astype(BF16)) + qy[u][:, LANES:]
            rows.append(y_st[:ch] + y_st[ch:])
            m_t = jnp.where(diag, p_end[s][:, p * LANES:(p + 1) * LANES], 0.0) + mn[u][:, :LANES]
            st_ref[p] = _mm3(m_t, st) + mn[u][:, LANES:]
        ys.append(jnp.concatenate(rows, axis=1))
    y = jnp.concatenate(ys, axis=0)

    gmean = gmean_ref[...]
    y_hi, y_lo = _split2(y)
    mean = _groups(y_hi, gmean) + _groups(y_lo, gmean)
    dev = y - mean
    var = _groups((dev * dev).astype(BF16), gmean)
    yn = dev * lax.rsqrt(var + GN_EPS) * lnw_ref[...] + lnb_ref[...]
    bonus = _groups((r * k * rk_ref[...]).astype(BF16), gmean) * float(HEAD_DIM) * v
    o_ref[0] = ((yn + bonus) * g_ref[0].astype(F32)).astype(o_ref.dtype)


def _rwkv_chunk_call(r, lw, k, v, a, b, g, gmean, rk, lnw, lnb, *, n_sub):
    bsz, slen, hd = r.shape
    rows = n_sub * RWKV_CHUNK
    idx = jnp.arange(rows)
    same_chunk = idx[:, None] // RWKV_CHUNK == idx[None, :] // RWKV_CHUNK
    tri = (same_chunk & (idx[:, None] >= idx[None, :])).astype(BF16)
    tok = pl.BlockSpec((1, rows, hd), lambda bi, ci: (bi, ci, 0))
    consts = [tri, gmean, rk, lnw, lnb]
    return pl.pallas_call(
        functools.partial(_rwkv_chunk_body, n_sub=n_sub),
        grid=(bsz, slen // rows),
        in_specs=[tok] * 7 + [_resident(c.shape) for c in consts],
        out_specs=tok,
        out_shape=jax.ShapeDtypeStruct((bsz, slen, hd), BF16),
        scratch_shapes=[pltpu.VMEM((hd // LANES, LANES, LANES), F32)],
        compiler_params=_params("parallel", "arbitrary"),
        name="rwkv_chunk_scan",
    )(r, lw, k, v, a, b, g, *consts)


def _block_diag_const(width, group, value, dtype):
    idx = jnp.arange(width) // group
    return jnp.where(idx[:, None] == idx[None, :], value, 0.0).astype(dtype)


def kernel(x, c, mod_w, mod_b, norm_g, ffn1_wg, ffn1_wu, ffn1_wd, w_in_first, w_in_rest, shift_mu, decay_w0, decay_w2, aaa_a0, aaa_a2, gate_g2, k_k, k_a, r_k, lnx_w, lnx_b, vres_v0, vres_v2, q_norm_g, k_norm_g, w_up_a, w_up_b, w_out, ffn2_wg, ffn2_wu, ffn2_wd):
    bsz, slen, d = x.shape
    n_layers = mod_w.shape[0]
    hd = decay_w0.shape[-1]
    n_heads = hd // HEAD_DIM
    decay_rank, aaa_rank = decay_w2.shape[1], aaa_a2.shape[1]
    gate_rank, vres_rank = gate_g2.shape[1], vres_v2.shape[1]
    assert decay_rank + aaa_rank == LANES and gate_rank == LANES and vres_rank <= LANES
    assert w_up_b.shape[1] == hd and hd % LANES == 0
    rw = 3 * hd + decay_rank + aaa_rank + gate_rank
    cols_first = rw + 3 * hd + 2 * d
    assert w_in_first.shape[1] == cols_first

    tm = min(512, slen)
    tq_attn = min(512, slen)

    gmean = _block_diag_const(MXU_WIDTH, HEAD_DIM, 1.0 / HEAD_DIM, BF16)
    gsum = _block_diag_const(MXU_WIDTH, HEAD_DIM, 1.0, BF16)

    mod_all = _mod_call(c, mod_w, mod_b).reshape(n_layers, bsz, N_MOD, d)

    h = x
    v_first = None
    for l in range(n_layers):
        mod = mod_all[l]
        h = _ffn_call(h, norm_g[l], mod, ffn1_wg[l].astype(BF16), ffn1_wu[l].astype(BF16),
                      ffn1_wd[l].astype(BF16), sub=0, tm=tm, tf=256)

        has_vres = l > 0
        if has_vres:
            w_in = jnp.pad(w_in_rest[l - 1], ((0, 0), (0, LANES - vres_rank)))
        else:
            w_in = w_in_first
        gq = jnp.tile(q_norm_g[l], n_heads)[None, :]
        gk = jnp.tile(k_norm_g[l], n_heads)[None, :]
        lora_w = jnp.zeros((LANES, 2 * hd), F32)
        lora_w = lora_w.at[:decay_rank, :hd].set(decay_w2[l]).at[decay_rank:, hd:].set(aaa_a2[l])
        prep_consts = {
            "mu": shift_mu[l][None, :], "w0": decay_w0[l][None, :], "lora": lora_w,
            "a0": aaa_a0[l][None, :], "g2": gate_g2[l].astype(BF16), "k_k": k_k[l][None, :],
            "k_a": k_a[l][None, :], "gsum": gsum,
        }
        if has_vres:
            prep_consts["v0"] = vres_v0[l - 1][None, :]
            prep_consts["v2"] = jnp.pad(vres_v2[l - 1], ((0, LANES - vres_rank), (0, 0))).astype(BF16)
        q, k, v, gates, r_, lw_, k_, v_, a_, b_, g_ = _inproj_call(
            h, norm_g[l], mod, w_in.astype(BF16), gq, gk, gmean, v_first, prep_consts,
            rw=rw, hd=hd, decay_rank=decay_rank, tm=tm)
        if not has_vres:
            v_first = v_
        y_a = _rwkv_chunk_call(r_, lw_, k_, v_, a_, b_, g_, gmean, r_k[l].reshape(1, hd),
                               lnx_w[l][None, :], lnx_b[l][None, :], n_sub=RWKV_SUBCHUNKS)

        y_b = _attn_call(q, k, v, tq=tq_attn, n_pairs=hd // LANES)

        branches = (y_a, y_b, gates, w_up_a[l].astype(BF16), w_up_b[l].astype(BF16), w_out[l].astype(BF16))
        h = _ffn_call(h, norm_g[l], mod, ffn2_wg[l].astype(BF16), ffn2_wu[l].astype(BF16),
                      ffn2_wd[l].astype(BF16), sub=2, tm=tm, tf=256, branches=branches)
    return h
```

```python
import functools
import math

import jax
import jax.numpy as jnp
from jax import lax
from jax.experimental import pallas as pl
from jax.experimental.pallas import tpu as pltpu

F32 = jnp.float32
BF16 = jnp.bfloat16

HEAD_DIM = 64
RMS_EPS = 1e-6
GN_EPS = 64e-5
N_MOD = 9
LANES = 128
MXU_WIDTH = 256
VMEM_LIMIT = 56 * 1024 * 1024
RWKV_CHUNK = 64
RWKV_SUBCHUNKS = 4

_NN = (((1,), (0,)), ((), ()))
_NT = (((1,), (1,)), ((), ()))


def _mm(a, b, dims=_NN):
    return lax.dot_general(a, b, dims, preferred_element_type=F32)


def _split2(x):
    hi = x.astype(BF16)
    lo = (x - hi.astype(F32)).astype(BF16)
    return hi, lo


def _split3(x):
    hi = x.astype(BF16)
    r1 = x - hi.astype(F32)
    mid = r1.astype(BF16)
    lo = (r1 - mid.astype(F32)).astype(BF16)
    return hi, mid, lo


def _mm3(a, b, dims=_NN):
    ah, al = _split2(a)
    bh, bl = _split2(b)
    return _mm(ah, bh, dims) + (_mm(ah, bl, dims) + _mm(al, bh, dims))


def _mm_lhs_exact(a_bf16, b):
    hi, mid, lo = _split3(b)
    return _mm(a_bf16, hi) + (_mm(a_bf16, mid) + _mm(a_bf16, lo))


def _groups(x_bf16, g):
    n = g.shape[0]
    slabs = [_mm(x_bf16[:, i * n:(i + 1) * n], g) for i in range(x_bf16.shape[1] // n)]
    return jnp.concatenate(slabs, axis=1)


def _params(*sem):
    return pltpu.CompilerParams(dimension_semantics=sem, vmem_limit_bytes=VMEM_LIMIT)


def _resident(shape):
    zeros = (0,) * len(shape)
    return pl.BlockSpec(shape, lambda *_: zeros, pipeline_mode=pl.Buffered(1))


def _layer(arr, l):
    zeros = (0,) * (arr.ndim - 1)
    return pl.BlockSpec((1,) + arr.shape[1:], lambda *_: (l,) + zeros, pipeline_mode=pl.Buffered(1))


def _norm_mod(x, g, sh, sc):
    ms = jnp.mean(x * x, axis=-1, keepdims=True)
    y = x * lax.rsqrt(ms + RMS_EPS) * g
    return y * (1.0 + sc) + sh


LOG2E = 1.4426950408889634
ATTN_DEAD_LOG2 = -160.0


def _log2(x):
    return jnp.log(x) * LOG2E


def _softplus(x):
    return jnp.maximum(x, 0.0) + jnp.log(1.0 + jnp.exp(-jnp.abs(x)))


def _mod_body(c_ref, w_ref, b_ref, o_ref):
    c = c_ref[...]
    ca = (c * jax.nn.sigmoid(c)).astype(BF16)
    o_ref[0] = _mm(ca, w_ref[0].astype(BF16)) + b_ref[0]


def _mod_call(c, mod_w, mod_b):
    n_layers, d, nd = mod_w.shape
    bsz = c.shape[0]
    tn = d
    return pl.pallas_call(
        _mod_body,
        grid=(n_layers, nd // tn),
        in_specs=[
            pl.BlockSpec((bsz, d), lambda l, j: (0, 0)),
            pl.BlockSpec((1, d, tn), lambda l, j: (l, 0, j)),
            pl.BlockSpec((1, 1, tn), lambda l, j: (l, 0, j)),
        ],
        out_specs=pl.BlockSpec((1, bsz, tn), lambda l, j: (l, 0, j)),
        out_shape=jax.ShapeDtypeStruct((n_layers, bsz, nd), F32),
        compiler_params=_params("parallel", "parallel"),
        name="adaln_mod",
    )(c, mod_w, mod_b.reshape(n_layers, 1, nd))


def _ffn_body(*refs, sub, tf, merge):
    if merge:
        h_ref, ya_ref, yb_ref, gate_ref, wa_ref, wb_ref, wo_ref = refs[:7]
        refs = refs[7:]
    else:
        h_ref = refs[0]
        refs = refs[1:]
    g_ref, mod_ref, wg_ref, wu_ref, wd_ref, o_ref = refs
    x = h_ref[0]
    mod = mod_ref[0, 0]
    if merge:
        d = x.shape[-1]
        ua = _mm(ya_ref[0], wa_ref[0])
        ub = _mm(yb_ref[0], wb_ref[0])
        merged = gate_ref[0, :, 0:d].astype(F32) * ua + gate_ref[0, :, d:2 * d].astype(F32) * ub
        x = x + mod[5:6, :] * _mm(merged.astype(BF16), wo_ref[0])
    g = g_ref[0, sub:sub + 1, :]
    sh = mod[3 * sub:3 * sub + 1, :]
    sc = mod[3 * sub + 1:3 * sub + 2, :]
    gt = mod[3 * sub + 2:3 * sub + 3, :]
    n = _norm_mod(x, g, sh, sc).astype(BF16)
    d_ff = wg_ref.shape[2]
    acc = jnp.zeros_like(x)
    for j in range(d_ff // tf):
        gj = _mm(n, wg_ref[0, :, j * tf:(j + 1) * tf])
        uj = _mm(n, wu_ref[0, :, j * tf:(j + 1) * tf])
        a = (gj * jax.nn.sigmoid(gj) * uj).astype(BF16)
        acc = acc + _mm(a, wd_ref[0, j * tf:(j + 1) * tf, :])
    o_ref[0] = x + (0.5 * gt) * acc


def _ffn_call(h, norm_g, mod, wg, wu, wd, *, layer, sub, tm, tf, branches=None):
    bsz, slen, d = h.shape

    def tok(width):
        return pl.BlockSpec((1, tm, width), lambda b, i: (b, i, 0))

    args, specs = [h], [tok(d)]
    if branches is not None:
        ya, yb, gates, wa, wb, wo = branches
        args += [ya, yb, gates, wa, wb, wo]
        specs += [tok(ya.shape[-1]), tok(yb.shape[-1]), tok(2 * d),
                  _layer(wa, layer), _layer(wb, layer), _layer(wo, layer)]
    args += [norm_g, mod, wg, wu, wd]
    specs += [_layer(norm_g, layer), pl.BlockSpec((1, 1, N_MOD, d), lambda b, i: (layer, b, 0, 0)),
              _layer(wg, layer), _layer(wu, layer), _layer(wd, layer)]
    return pl.pallas_call(
        functools.partial(_ffn_body, sub=sub, tf=tf, merge=branches is not None),
        grid=(bsz, slen // tm),
        in_specs=specs,
        out_specs=tok(d),
        out_shape=jax.ShapeDtypeStruct(h.shape, F32),
        compiler_params=_params("parallel", "parallel"),
        name="merge_ffn" if branches is not None else "ffn",
    )(*args)


def _head_rms(t, g, gmean):
    ms = _groups((t * t).astype(BF16), gmean)
    return t * lax.rsqrt(ms + RMS_EPS) * g


def _rwkv_prep(p, prev, tail, v_first, c, *, hd, decay_rank):
    tm = p.shape[0]
    shifted = pltpu.roll(p, 1, axis=0)
    rows = lax.broadcasted_iota(jnp.int32, (tm, 1), 0)
    shifted = jnp.where(rows == 0, prev, shifted)
    p = p + (shifted - p) * c["mu"]
    r = p[:, 0:hd]
    k = p[:, hd:2 * hd]
    v = p[:, 2 * hd:3 * hd]
    wa = p[:, 3 * hd:3 * hd + LANES]
    gl = p[:, 3 * hd + LANES:3 * hd + 2 * LANES]
    lane = lax.broadcasted_iota(jnp.int32, (1, LANES), 1)
    wa = jnp.where(lane < decay_rank, jnp.tanh(wa), wa)
    lora = _mm3(wa, c["lora"])
    w = -_softplus(-(c["w0"] + lora[:, 0:hd])) - 0.5
    log_decay = -jnp.exp(w)
    a = jax.nn.sigmoid(c["a0"] + lora[:, hd:2 * hd])
    g = _mm(jax.nn.sigmoid(gl).astype(BF16), c["g2"])
    if tail is not None:
        mix = jax.nn.sigmoid(c["v0"] + _mm(tail.astype(BF16), c["v2"]))
        v = v + (v_first - v) * mix
    kk = k * c["k_k"]
    sq_hi, sq_lo = _split2(kk * kk)
    ss = _groups(sq_hi, c["gsum"]) + _groups(sq_lo, c["gsum"])
    kk = kk / jnp.maximum(jnp.sqrt(ss), 1e-12)
    return r, log_decay, k * (1.0 + (a - 1.0) * c["k_a"]), v, -kk, kk * a, g


_PREP_CONSTS = ("mu", "w0", "lora", "a0", "g2", "k_k", "k_a", "gsum")
_VRES_CONSTS = ("v0", "v2")


def _inproj_body(*refs, rw, hd, d, has_vres, decay_rank):
    names = ("h", "g", "mod", "w", "gq", "gk", "gmean") + (("vf",) if has_vres else ())
    names += _PREP_CONSTS + (_VRES_CONSTS if has_vres else ())
    ins = dict(zip(names, refs))
    q_ref, k_ref, v_ref, gate_ref = refs[len(names):len(names) + 4]
    rwkv_outs = refs[len(names) + 4:len(names) + 11]
    prev_ref = refs[len(names) + 11]
    i = pl.program_id(1)
    x = ins["h"][0]
    tm = x.shape[0]
    mod = ins["mod"][0, 0]
    n = _norm_mod(x, ins["g"][0, 1:2, :], mod[3:4, :], mod[4:5, :]).astype(BF16)
    w_ref = ins["w"].at[0]
    gmean = ins["gmean"][...]
    p = _mm(n, w_ref[:, 0:rw])
    og = rw + 3 * hd
    tail = _mm(n, w_ref[:, og + 2 * d:og + 2 * d + LANES]) if has_vres else None
    q = _mm(n, w_ref[:, rw:rw + hd])
    q_ref[0] = (_head_rms(q, ins["gq"][...], gmean) * (LOG2E / math.sqrt(HEAD_DIM))).astype(BF16)
    k = _mm(n, w_ref[:, rw + hd:rw + 2 * hd])
    k_ref[0] = _head_rms(k, ins["gk"][...], gmean).astype(BF16)
    v_ref[0] = _mm(n, w_ref[:, rw + 2 * hd:rw + 3 * hd]).astype(BF16)
    gate_ref[0] = jax.nn.sigmoid(_mm(n, w_ref[:, og:og + 2 * d])).astype(BF16)

    prev = prev_ref[7:8, :]
    prev = jnp.where(i == 0, jnp.zeros_like(prev), prev)
    prev_ref[...] = p[tm - 8:tm, :]
    consts = {name: ins[name][...] for name in _PREP_CONSTS + (_VRES_CONSTS if has_vres else ())}
    v_first = ins["vf"][0].astype(F32) if has_vres else None
    outs = _rwkv_prep(p, prev, tail, v_first, consts, hd=hd, decay_rank=decay_rank)
    for o_ref, val in zip(rwkv_outs, outs):
        o_ref[0] = val.astype(o_ref.dtype)


def _inproj_call(h, norm_g, mod, w, gq, gk, gmean, v_first, prep_consts, *, layer, rw, hd,
                 decay_rank, tm):
    bsz, slen, d = h.shape
    has_vres = v_first is not None

    def tok(width):
        return pl.BlockSpec((1, tm, width), lambda b, i: (b, i, 0))

    args = [h, norm_g, mod, w, gq, gk, gmean]
    specs = [tok(d), _layer(norm_g, layer), pl.BlockSpec((1, 1, N_MOD, d), lambda b, i: (layer, b, 0, 0)),
             _layer(w, layer), _resident(gq.shape), _resident(gk.shape), _resident(gmean.shape)]
    if has_vres:
        args.append(v_first)
        specs.append(tok(hd))
    names = _PREP_CONSTS + (_VRES_CONSTS if has_vres else ())
    args += [prep_consts[name] for name in names]
    specs += [_resident(prep_consts[name].shape) for name in names]
    outs = [(hd, BF16)] * 3 + [(2 * d, BF16)] + [(hd, F32 if j == 1 else BF16) for j in range(7)]
    return pl.pallas_call(
        functools.partial(_inproj_body, rw=rw, hd=hd, d=d, has_vres=has_vres, decay_rank=decay_rank),
        grid=(bsz, slen // tm),
        in_specs=specs,
        out_specs=[tok(wd) for wd, _ in outs],
        out_shape=[jax.ShapeDtypeStruct((bsz, slen, wd), dt) for wd, dt in outs],
        scratch_shapes=[pltpu.VMEM((8, rw), F32)],
        compiler_params=_params("parallel", "arbitrary"),
        name="in_proj",
    )(*args)


def _attn_body(q_ref, k_ref, v_ref, o_ref, *, tq, n_pairs):
    tk = LANES
    qi = pl.program_id(2)
    q0 = qi * tq
    pairs = range(n_pairs)
    lanes = [slice(p * LANES, (p + 1) * LANES) for p in pairs]
    lane = lax.broadcasted_iota(jnp.int32, (1, LANES), 1)
    first = lane < HEAD_DIM
    row = lax.broadcasted_iota(jnp.int32, (tq, 2 * tk), 0)
    col = lax.broadcasted_iota(jnp.int32, (tq, 2 * tk), 1)
    col = jnp.where(col >= tk, col - tk, col)
    jj = lax.broadcasted_iota(jnp.int32, (2 * tk, 2 * tk), 0)
    ss = lax.broadcasted_iota(jnp.int32, (2 * tk, 2 * tk), 1)
    jj = jnp.where(jj >= tk, jj - tk, jj)
    suffix_total = jnp.where((ss >= tk) | (jj > ss), -1.0, 0.0).astype(BF16)
    halves = [slice(e * tk, (e + 1) * tk) for e in range(2)]

    def both_heads(x):
        zero = jnp.zeros_like(x)
        return jnp.concatenate([jnp.where(first, x, zero), jnp.where(first, zero, x)], axis=0)

    def block(k0, carry, diag_off):
        r0 = 0 if diag_off is None else diag_off
        new = []
        for p in pairs:
            acc, logsurv = carry[2 * p][r0:], carry[2 * p + 1][r0:]
            k2 = both_heads(k_ref[0, pl.ds(k0, tk), lanes[p]])
            v2 = both_heads(v_ref[0, pl.ds(k0, tk), lanes[p]])
            z = _mm(q_ref[0, r0:, lanes[p]], k2, _NT)
            neg_abs = lax.bitcast_convert_type(
                lax.bitcast_convert_type(z, jnp.uint32) | jnp.uint32(0x80000000), F32)
            sp = jnp.maximum(z, 0.0) + _log2(1.0 + jnp.exp2(neg_abs))
            if diag_off is not None:
                causal = (col + diag_off < row)[r0:]
                sp = jnp.where(causal, sp, 0.0)
            hi, lo = _split2(sp)
            st = [_mm(jnp.concatenate([hi[:, h], lo[:, h]], axis=1), suffix_total) for h in halves]
            suffix = jnp.concatenate([st[0][:, :tk], st[1][:, :tk]], axis=1)
            total = jnp.concatenate([st[0][:, tk:], st[1][:, tk:]], axis=1)
            w = jnp.exp2((z - sp) + logsurv + suffix)
            if diag_off is not None:
                w = jnp.where(causal, w, 0.0)
            acc = acc + _mm(w.astype(BF16), v2)
            logsurv = logsurv + total
            if r0:
                acc = jnp.concatenate([carry[2 * p][:r0], acc], axis=0)
                logsurv = jnp.concatenate([carry[2 * p + 1][:r0], logsurv], axis=0)
            new += [acc, logsurv]
        return tuple(new)

    carry = (jnp.zeros((tq, LANES), F32), jnp.zeros((tq, 2 * tk), F32)) * n_pairs
    n_diag = tq // tk
    for j in reversed(range(n_diag)):
        carry = block(pl.multiple_of(q0 + j * tk, tk), carry, j * tk)

    def alive(carry):
        worst = carry[1]
        for p in range(1, n_pairs):
            worst = jnp.maximum(worst, carry[2 * p + 1])
        return (jnp.max(worst) > ATTN_DEAD_LOG2).astype(jnp.int32)

    n_off = qi * n_diag

    def cond(state):
        return (state[0] < n_off) & (state[1] > 0)

    def body(state):
        i = state[0]
        carry = block(pl.multiple_of(q0 - (i + 1) * tk, tk), state[2:], None)
        return (i + 1, alive(carry)) + carry

    carry = lax.while_loop(cond, body, (jnp.int32(0), alive(carry)) + carry)[2:]
    for p in pairs:
        o_ref[0, :, lanes[p]] = carry[2 * p].astype(o_ref.dtype)


def _attn_call(q, k, v, *, tq, n_pairs):
    bsz, slen, hd = q.shape
    width = n_pairs * LANES
    qblk = pl.BlockSpec((1, tq, width), lambda b, p, i: (b, i, p))
    kvblk = pl.BlockSpec((1, slen, width), lambda b, p, i: (b, 0, p))
    return pl.pallas_call(
        functools.partial(_attn_body, tq=tq, n_pairs=n_pairs),
        grid=(bsz, hd // width, slen // tq),
        in_specs=[qblk, kvblk, kvblk],
        out_specs=qblk,
        out_shape=jax.ShapeDtypeStruct(q.shape, BF16),
        compiler_params=_params("parallel", "parallel", "arbitrary"),
        name="stick_breaking_attn",
    )(q, k, v)


def _rwkv_chunk_body(r_ref, lw_ref, k_ref, v_ref, a_ref, b_ref, g_ref, tri_ref, gmean_ref,
                     rk_ref, lnw_ref, lnb_ref, o_ref, st_ref, *, n_sub):
    c = pl.program_id(1)

    @pl.when(c == 0)
    def _():
        st_ref[...] = jnp.zeros_like(st_ref)

    r, lw, k, v = r_ref[0].astype(F32), lw_ref[0], k_ref[0].astype(F32), v_ref[0].astype(F32)
    a, b = a_ref[0].astype(F32), b_ref[0].astype(F32)
    ch = RWKV_CHUNK
    n_pairs = r.shape[1] // LANES
    cum = _mm_lhs_exact(tri_ref[...], lw)
    cum_prev = cum - lw
    lasts = [cum[(s + 1) * ch - 1:(s + 1) * ch, :] for s in range(n_sub)]
    last = jnp.concatenate([jnp.broadcast_to(t, (ch, t.shape[1])) for t in lasts], axis=0)
    e_pos = jnp.exp(cum)
    e_neg = jnp.exp(-cum)
    e_end = jnp.exp(last - cum)
    at = a * jnp.exp(cum_prev)
    rt = r * e_pos
    bt = b * e_neg
    kt = k * e_neg
    bh = b * e_end
    kh = k * e_end
    p_end = [jnp.exp(t) for t in lasts]

    lane = lax.broadcasted_iota(jnp.int32, (1, LANES), 1)
    first = lane < HEAD_DIM

    def stacked(x, s, p):
        x = x[s * ch:(s + 1) * ch, p * LANES:(p + 1) * LANES]
        zero = jnp.zeros_like(x)
        return jnp.concatenate([jnp.where(first, x, zero), jnp.where(first, zero, x)], axis=0)

    row = lax.broadcasted_iota(jnp.int32, (2 * ch, 2 * ch), 0)
    col = lax.broadcasted_iota(jnp.int32, (2 * ch, 2 * ch), 1)
    strict = (row & (ch - 1)) > (col & (ch - 1))
    row2 = lax.broadcasted_iota(jnp.int32, (2 * ch, 4 * ch), 0)
    col2 = lax.broadcasted_iota(jnp.int32, (2 * ch, 4 * ch), 1)
    lower2 = (row2 & (ch - 1)) >= (col2 & (ch - 1))
    diag = row == col
    eye = jnp.where(diag, 1.0, 0.0)
    n_levels = int(math.log2(ch))

    def sub_diag(level):
        size, half = 1 << level, 1 << (level - 1)
        inside = (row >> level) == (col >> level)
        return inside & ((row & (size - 1)) >= half) & ((col & (size - 1)) < half)

    units = [(s, p) for s in range(n_sub) for p in range(n_pairs)]
    at_s, rt_s, bt_s, kt_s, bh_s, kh_s, v_s = (
        [stacked(t, s, p) for s, p in units] for t in (at, rt, bt, kt, bh, kh, v))
    idx = range(len(units))
    prod = [_mm(jnp.concatenate([at_s[u], rt_s[u]], axis=0).astype(BF16),
                jnp.concatenate([bt_s[u], kt_s[u]], axis=0).astype(BF16), _NT) for u in idx]
    a_ab = [jnp.where(strict, prod[u][:2 * ch, :2 * ch], 0.0) for u in idx]
    a_ak = [jnp.where(strict, prod[u][:2 * ch, 2 * ch:], 0.0).astype(BF16) for u in idx]
    a_rbk = [jnp.where(lower2, prod[u][2 * ch:, :], 0.0).astype(BF16) for u in idx]
    lvl = sub_diag(1)
    tinv = [eye + jnp.where(lvl, a_ab[u], 0.0) for u in idx]
    for level in range(2, n_levels + 1):
        lvl = sub_diag(level)
        tinv_b = [tinv[u].astype(BF16) for u in idx]
        right = [_mm(jnp.where(lvl, a_ab[u], 0.0).astype(BF16), tinv_b[u]) for u in idx]
        tinv = [tinv[u] + _mm(tinv_b[u], right[u].astype(BF16)) for u in idx]
    v_b = [v_s[u].astype(BF16) for u in idx]
    akv = [_mm(a_ak[u], v_b[u]) for u in idx]
    wx = [_mm(tinv[u].astype(BF16),
              jnp.concatenate([at_s[u], akv[u]], axis=1).astype(BF16)) for u in idx]
    zero_b = jnp.zeros((2 * ch, LANES), BF16)
    wxv = [jnp.concatenate([wx[u].astype(BF16), jnp.concatenate([zero_b, v_b[u]], axis=1)], axis=0)
           for u in idx]
    qy = [_mm(a_rbk[u], wxv[u]) for u in idx]
    mn = [_mm(jnp.concatenate([bh_s[u], kh_s[u]], axis=0).T.astype(BF16), wxv[u]) for u in idx]
    ys = []
    for s in range(n_sub):
        rows = []
        for p in range(n_pairs):
            u = s * n_pairs + p
            st = st_ref[p]
            y_st = _mm((rt_s[u] + qy[u][:, :LANES]).astype(BF16), st.astype(BF16)) + qy[u][:, LANES:]
            rows.append(y_st[:ch] + y_st[ch:])
            m_t = jnp.where(diag, p_end[s][:, p * LANES:(p + 1) * LANES], 0.0) + mn[u][:, :LANES]
            st_ref[p] = _mm3(m_t, st) + mn[u][:, LANES:]
        ys.append(jnp.concatenate(rows, axis=1))
    y = jnp.concatenate(ys, axis=0)

    gmean = gmean_ref[...]
    y_hi, y_lo = _split2(y)
    mean = _groups(y_hi, gmean) + _groups(y_lo, gmean)
    dev = y - mean
    var = _groups((dev * dev).astype(BF16), gmean)
    yn = dev * lax.rsqrt(var + GN_EPS) * lnw_ref[...] + lnb_ref[...]
    bonus = _groups((r * k * rk_ref[...]).astype(BF16), gmean) * float(HEAD_DIM) * v
    o_ref[0] = ((yn + bonus) * g_ref[0].astype(F32)).astype(o_ref.dtype)


def _rwkv_chunk_call(r, lw, k, v, a, b, g, gmean, rk, lnw, lnb, *, n_sub):
    bsz, slen, hd = r.shape
    rows = n_sub * RWKV_CHUNK
    idx = jnp.arange(rows)
    same_chunk = idx[:, None] // RWKV_CHUNK == idx[None, :] // RWKV_CHUNK
    tri = (same_chunk & (idx[:, None] >= idx[None, :])).astype(BF16)
    tok = pl.BlockSpec((1, rows, hd), lambda bi, ci: (bi, ci, 0))
    consts = [tri, gmean, rk, lnw, lnb]
    return pl.pallas_call(
        functools.partial(_rwkv_chunk_body, n_sub=n_sub),
        grid=(bsz, slen // rows),
        in_specs=[tok] * 7 + [_resident(c.shape) for c in consts],
        out_specs=tok,
        out_shape=jax.ShapeDtypeStruct((bsz, slen, hd), BF16),
        scratch_shapes=[pltpu.VMEM((hd // LANES, LANES, LANES), F32)],
        compiler_params=_params("parallel", "arbitrary"),
        name="rwkv_chunk_scan",
    )(r, lw, k, v, a, b, g, *consts)


def _block_diag_const(width, group, value, dtype):
    idx = jnp.arange(width) // group
    return jnp.where(idx[:, None] == idx[None, :], value, 0.0).astype(dtype)


def kernel(x, c, mod_w, mod_b, norm_g, ffn1_wg, ffn1_wu, ffn1_wd, w_in_first, w_in_rest, shift_mu, decay_w0, decay_w2, aaa_a0, aaa_a2, gate_g2, k_k, k_a, r_k, lnx_w, lnx_b, vres_v0, vres_v2, q_norm_g, k_norm_g, w_up_a, w_up_b, w_out, ffn2_wg, ffn2_wu, ffn2_wd):
    bsz, slen, d = x.shape
    n_layers = mod_w.shape[0]
    hd = decay_w0.shape[-1]
    n_heads = hd // HEAD_DIM
    decay_rank, aaa_rank = decay_w2.shape[1], aaa_a2.shape[1]
    gate_rank, vres_rank = gate_g2.shape[1], vres_v2.shape[1]
    assert decay_rank + aaa_rank == LANES and gate_rank == LANES and vres_rank <= LANES
    assert w_up_b.shape[1] == hd and hd % LANES == 0
    rw = 3 * hd + decay_rank + aaa_rank + gate_rank
    cols_first = rw + 3 * hd + 2 * d
    assert w_in_first.shape[1] == cols_first

    tm = min(512, slen)
    tq_attn = min(512, slen)

    gmean = _block_diag_const(MXU_WIDTH, HEAD_DIM, 1.0 / HEAD_DIM, BF16)
    gsum = _block_diag_const(MXU_WIDTH, HEAD_DIM, 1.0, BF16)

    mod_all = _mod_call(c, mod_w, mod_b).reshape(n_layers, bsz, N_MOD, d)

    ffn1 = tuple(t.astype(BF16) for t in (ffn1_wg, ffn1_wu, ffn1_wd))
    ffn2 = tuple(t.astype(BF16) for t in (ffn2_wg, ffn2_wu, ffn2_wd))
    ups = tuple(t.astype(BF16) for t in (w_up_a, w_up_b, w_out))
    w_in = jnp.concatenate([
        jnp.pad(w_in_first, ((0, 0), (0, LANES)))[None],
        jnp.pad(w_in_rest, ((0, 0), (0, 0), (0, LANES - vres_rank)))], axis=0).astype(BF16)

    h = x
    v_first = None
    for l in range(n_layers):
        h = _ffn_call(h, norm_g, mod_all, *ffn1, layer=l, sub=0, tm=tm, tf=256)

        has_vres = l > 0
        gq = jnp.tile(q_norm_g[l], n_heads)[None, :]
        gk = jnp.tile(k_norm_g[l], n_heads)[None, :]
        lora_w = jnp.zeros((LANES, 2 * hd), F32)
        lora_w = lora_w.at[:decay_rank, :hd].set(decay_w2[l]).at[decay_rank:, hd:].set(aaa_a2[l])
        prep_consts = {
            "mu": shift_mu[l][None, :], "w0": decay_w0[l][None, :], "lora": lora_w,
            "a0": aaa_a0[l][None, :], "g2": gate_g2[l].astype(BF16), "k_k": k_k[l][None, :],
            "k_a": k_a[l][None, :], "gsum": gsum,
        }
        if has_vres:
            prep_consts["v0"] = vres_v0[l - 1][None, :]
            prep_consts["v2"] = jnp.pad(vres_v2[l - 1], ((0, LANES - vres_rank), (0, 0))).astype(BF16)
        q, k, v, gates, r_, lw_, k_, v_, a_, b_, g_ = _inproj_call(
            h, norm_g, mod_all, w_in, gq, gk, gmean, v_first, prep_consts,
            layer=l, rw=rw, hd=hd, decay_rank=decay_rank, tm=tm)
        if not has_vres:
            v_first = v_
        y_a = _rwkv_chunk_call(r_, lw_, k_, v_, a_, b_, g_, gmean, r_k[l].reshape(1, hd),
                               lnx_w[l][None, :], lnx_b[l][None, :], n_sub=RWKV_SUBCHUNKS)

        y_b = _attn_call(q, k, v, tq=tq_attn, n_pairs=hd // LANES)

        h = _ffn_call(h, norm_g, mod_all, *ffn2, layer=l, sub=2, tm=tm, tf=256,
                      branches=(y_a, y_b, gates, *ups))
    return h
```

```python
import functools
import math

import jax
import jax.numpy as jnp
from jax import lax
from jax.experimental import pallas as pl
from jax.experimental.pallas import tpu as pltpu

F32 = jnp.float32
BF16 = jnp.bfloat16

HEAD_DIM = 64
RMS_EPS = 1e-6
GN_EPS = 64e-5
N_MOD = 9
LANES = 128
MXU_WIDTH = 256
VMEM_LIMIT = 56 * 1024 * 1024
RWKV_CHUNK = 64
RWKV_SUBCHUNKS = 4

_NN = (((1,), (0,)), ((), ()))
_NT = (((1,), (1,)), ((), ()))


def _mm(a, b, dims=_NN):
    return lax.dot_general(a, b, dims, preferred_element_type=F32)


def _split2(x):
    hi = x.astype(BF16)
    lo = (x - hi.astype(F32)).astype(BF16)
    return hi, lo


def _split3(x):
    hi = x.astype(BF16)
    r1 = x - hi.astype(F32)
    mid = r1.astype(BF16)
    lo = (r1 - mid.astype(F32)).astype(BF16)
    return hi, mid, lo


def _mm3(a, b, dims=_NN):
    ah, al = _split2(a)
    bh, bl = _split2(b)
    return _mm(ah, bh, dims) + (_mm(ah, bl, dims) + _mm(al, bh, dims))


def _mm_lhs_exact(a_bf16, b):
    hi, mid, lo = _split3(b)
    return _mm(a_bf16, hi) + (_mm(a_bf16, mid) + _mm(a_bf16, lo))


def _groups(x_bf16, g):
    n = g.shape[0]
    slabs = [_mm(x_bf16[:, i * n:(i + 1) * n], g) for i in range(x_bf16.shape[1] // n)]
    return jnp.concatenate(slabs, axis=1)


def _params(*sem):
    return pltpu.CompilerParams(dimension_semantics=sem, vmem_limit_bytes=VMEM_LIMIT)


def _resident(shape):
    zeros = (0,) * len(shape)
    return pl.BlockSpec(shape, lambda *_: zeros, pipeline_mode=pl.Buffered(1))


def _layer(arr, l):
    zeros = (0,) * (arr.ndim - 1)
    return pl.BlockSpec((1,) + arr.shape[1:], lambda *_: (l,) + zeros, pipeline_mode=pl.Buffered(1))


def _norm_mod(x, g, sh, sc):
    ms = jnp.mean(x * x, axis=-1, keepdims=True)
    return (x * lax.rsqrt(ms + RMS_EPS)) * (g * (1.0 + sc)) + sh


LOG2E = 1.4426950408889634
ATTN_DEAD_LOG2 = -160.0


def _log2(x):
    return jnp.log(x) * LOG2E


def _softplus(x):
    return jnp.maximum(x, 0.0) + jnp.log(1.0 + jnp.exp(-jnp.abs(x)))


def _mod_body(c_ref, w_ref, b_ref, o_ref):
    c = c_ref[...]
    ca = (c * jax.nn.sigmoid(c)).astype(BF16)
    o_ref[0] = _mm(ca, w_ref[0].astype(BF16)) + b_ref[0]


def _mod_call(c, mod_w, mod_b):
    n_layers, d, nd = mod_w.shape
    bsz = c.shape[0]
    tn = d
    return pl.pallas_call(
        _mod_body,
        grid=(n_layers, nd // tn),
        in_specs=[
            pl.BlockSpec((bsz, d), lambda l, j: (0, 0)),
            pl.BlockSpec((1, d, tn), lambda l, j: (l, 0, j)),
            pl.BlockSpec((1, 1, tn), lambda l, j: (l, 0, j)),
        ],
        out_specs=pl.BlockSpec((1, bsz, tn), lambda l, j: (l, 0, j)),
        out_shape=jax.ShapeDtypeStruct((n_layers, bsz, nd), F32),
        compiler_params=_params("parallel", "parallel"),
        name="adaln_mod",
    )(c, mod_w, mod_b.reshape(n_layers, 1, nd))


def _ffn_body(*refs, sub, tf, merge):
    if merge:
        h_ref, ya_ref, yb_ref, gate_ref, wa_ref, wb_ref, wo_ref = refs[:7]
        refs = refs[7:]
    else:
        h_ref = refs[0]
        refs = refs[1:]
    g_ref, mod_ref, wg_ref, wu_ref, wd_ref, o_ref = refs
    x = h_ref[0]
    mod = mod_ref[0, 0]
    if merge:
        d = x.shape[-1]
        ua = _mm(ya_ref[0], wa_ref[0])
        ub = _mm(yb_ref[0], wb_ref[0])
        gate_a = jax.nn.sigmoid(gate_ref[0, :, 0:d].astype(F32))
        gate_b = jax.nn.sigmoid(gate_ref[0, :, d:2 * d].astype(F32))
        merged = gate_a * ua + gate_b * ub
        x = x + mod[5:6, :] * _mm(merged.astype(BF16), wo_ref[0])
    g = g_ref[0, sub:sub + 1, :]
    sh = mod[3 * sub:3 * sub + 1, :]
    sc = mod[3 * sub + 1:3 * sub + 2, :]
    gt = mod[3 * sub + 2:3 * sub + 3, :]
    n = _norm_mod(x, g, sh, sc).astype(BF16)
    d_ff = wg_ref.shape[2]
    acc = jnp.zeros_like(x)
    for j in range(d_ff // tf):
        gj = _mm(n, wg_ref[0, :, j * tf:(j + 1) * tf])
        uj = _mm(n, wu_ref[0, :, j * tf:(j + 1) * tf])
        a = (gj * jax.nn.sigmoid(gj) * uj).astype(BF16)
        acc = acc + _mm(a, wd_ref[0, j * tf:(j + 1) * tf, :])
    o_ref[0] = x + (0.5 * gt) * acc


def _ffn_call(h, norm_g, mod, wg, wu, wd, *, layer, sub, tm, tf, branches=None):
    bsz, slen, d = h.shape

    def tok(width):
        return pl.BlockSpec((1, tm, width), lambda b, i: (b, i, 0))

    args, specs = [h], [tok(d)]
    if branches is not None:
        ya, yb, gates, wa, wb, wo = branches
        args += [ya, yb, gates, wa, wb, wo]
        specs += [tok(ya.shape[-1]), tok(yb.shape[-1]), tok(2 * d),
                  _layer(wa, layer), _layer(wb, layer), _layer(wo, layer)]
    args += [norm_g, mod, wg, wu, wd]
    specs += [_layer(norm_g, layer), pl.BlockSpec((1, 1, N_MOD, d), lambda b, i: (layer, b, 0, 0)),
              _layer(wg, layer), _layer(wu, layer), _layer(wd, layer)]
    return pl.pallas_call(
        functools.partial(_ffn_body, sub=sub, tf=tf, merge=branches is not None),
        grid=(bsz, slen // tm),
        in_specs=specs,
        out_specs=tok(d),
        out_shape=jax.ShapeDtypeStruct(h.shape, F32),
        compiler_params=_params("parallel", "parallel"),
        name="merge_ffn" if branches is not None else "ffn",
    )(*args)


def _head_rms(t, g, gmean):
    ms = _groups((t * t).astype(BF16), gmean)
    return t * lax.rsqrt(ms + RMS_EPS) * g


def _rwkv_prep(p, prev, tail, v_first, c, *, hd, decay_rank):
    tm = p.shape[0]
    shifted = pltpu.roll(p, 1, axis=0)
    rows = lax.broadcasted_iota(jnp.int32, (tm, 1), 0)
    shifted = jnp.where(rows == 0, prev, shifted)
    p = p + (shifted - p) * c["mu"]
    r = p[:, 0:hd]
    k = p[:, hd:2 * hd]
    v = p[:, 2 * hd:3 * hd]
    wa = p[:, 3 * hd:3 * hd + LANES]
    gl = p[:, 3 * hd + LANES:3 * hd + 2 * LANES]
    lane = lax.broadcasted_iota(jnp.int32, (1, LANES), 1)
    wa = jnp.where(lane < decay_rank, jnp.tanh(wa), wa)
    lora = _mm3(wa, c["lora"])
    w = -_softplus(-(c["w0"] + lora[:, 0:hd])) - 0.5
    log_decay = -jnp.exp(w)
    a = jax.nn.sigmoid(c["a0"] + lora[:, hd:2 * hd])
    g = _mm(jax.nn.sigmoid(gl).astype(BF16), c["g2"])
    if tail is not None:
        mix = jax.nn.sigmoid(c["v0"] + _mm(tail.astype(BF16), c["v2"]))
        v = v + (v_first - v) * mix
    kk = k * c["k_k"]
    sq_hi, sq_lo = _split2(kk * kk)
    ss = _groups(sq_hi, c["gsum"]) + _groups(sq_lo, c["gsum"])
    kk = kk / jnp.maximum(jnp.sqrt(ss), 1e-12)
    return r, log_decay, k * (1.0 + (a - 1.0) * c["k_a"]), v, -kk, kk * a, g


_PREP_CONSTS = ("mu", "w0", "lora", "a0", "g2", "k_k", "k_a", "gsum")
_VRES_CONSTS = ("v0", "v2")


def _inproj_body(*refs, rw, hd, d, has_vres, decay_rank):
    names = ("h", "g", "mod", "w", "gq", "gk", "gmean") + (("vf",) if has_vres else ())
    names += _PREP_CONSTS + (_VRES_CONSTS if has_vres else ())
    ins = dict(zip(names, refs))
    q_ref, k_ref, v_ref, gate_ref = refs[len(names):len(names) + 4]
    rwkv_outs = refs[len(names) + 4:len(names) + 11]
    prev_ref = refs[len(names) + 11]
    i = pl.program_id(1)
    x = ins["h"][0]
    tm = x.shape[0]
    mod = ins["mod"][0, 0]
    n = _norm_mod(x, ins["g"][0, 1:2, :], mod[3:4, :], mod[4:5, :]).astype(BF16)
    w_ref = ins["w"].at[0]
    gmean = ins["gmean"][...]
    p = _mm(n, w_ref[:, 0:rw])
    og = rw + 3 * hd
    tail = _mm(n, w_ref[:, og + 2 * d:og + 2 * d + LANES]) if has_vres else None
    q = _mm(n, w_ref[:, rw:rw + hd])
    q_ref[0] = _head_rms(q, ins["gq"][...], gmean).astype(BF16)
    k = _mm(n, w_ref[:, rw + hd:rw + 2 * hd])
    k_ref[0] = _head_rms(k, ins["gk"][...], gmean).astype(BF16)
    v_ref[0] = _mm(n, w_ref[:, rw + 2 * hd:rw + 3 * hd]).astype(BF16)
    gate_ref[0] = _mm(n, w_ref[:, og:og + 2 * d]).astype(BF16)

    prev = prev_ref[7:8, :]
    prev = jnp.where(i == 0, jnp.zeros_like(prev), prev)
    prev_ref[...] = p[tm - 8:tm, :]
    consts = {name: ins[name][...] for name in _PREP_CONSTS + (_VRES_CONSTS if has_vres else ())}
    v_first = ins["vf"][0].astype(F32) if has_vres else None
    outs = _rwkv_prep(p, prev, tail, v_first, consts, hd=hd, decay_rank=decay_rank)
    for o_ref, val in zip(rwkv_outs, outs):
        o_ref[0] = val.astype(o_ref.dtype)


def _inproj_call(h, norm_g, mod, w, gq, gk, gmean, v_first, prep_consts, *, layer, rw, hd,
                 decay_rank, tm):
    bsz, slen, d = h.shape
    has_vres = v_first is not None

    def tok(width):
        return pl.BlockSpec((1, tm, width), lambda b, i: (b, i, 0))

    args = [h, norm_g, mod, w, gq, gk, gmean]
    specs = [tok(d), _layer(norm_g, layer), pl.BlockSpec((1, 1, N_MOD, d), lambda b, i: (layer, b, 0, 0)),
             _layer(w, layer), _resident(gq.shape), _resident(gk.shape), _resident(gmean.shape)]
    if has_vres:
        args.append(v_first)
        specs.append(tok(hd))
    names = _PREP_CONSTS + (_VRES_CONSTS if has_vres else ())
    args += [prep_consts[name] for name in names]
    specs += [_resident(prep_consts[name].shape) for name in names]
    outs = [(hd, BF16)] * 3 + [(2 * d, BF16)] + [(hd, F32 if j == 1 else BF16) for j in range(7)]
    return pl.pallas_call(
        functools.partial(_inproj_body, rw=rw, hd=hd, d=d, has_vres=has_vres, decay_rank=decay_rank),
        grid=(bsz, slen // tm),
        in_specs=specs,
        out_specs=[tok(wd) for wd, _ in outs],
        out_shape=[jax.ShapeDtypeStruct((bsz, slen, wd), dt) for wd, dt in outs],
        scratch_shapes=[pltpu.VMEM((8, rw), F32)],
        compiler_params=_params("parallel", "arbitrary"),
        name="in_proj",
    )(*args)


def _attn_body(q_ref, k_ref, v_ref, o_ref, *, tq, n_pairs):
    tk = LANES
    qi = pl.program_id(2)
    q0 = qi * tq
    pairs = range(n_pairs)
    lanes = [slice(p * LANES, (p + 1) * LANES) for p in pairs]
    lane = lax.broadcasted_iota(jnp.int32, (1, LANES), 1)
    first = lane < HEAD_DIM
    row = lax.broadcasted_iota(jnp.int32, (tq, 2 * tk), 0)
    col = lax.broadcasted_iota(jnp.int32, (tq, 2 * tk), 1)
    col = jnp.where(col >= tk, col - tk, col)
    jj = lax.broadcasted_iota(jnp.int32, (2 * tk, 2 * tk), 0)
    ss = lax.broadcasted_iota(jnp.int32, (2 * tk, 2 * tk), 1)
    jj = jnp.where(jj >= tk, jj - tk, jj)
    suffix_total = jnp.where((ss >= tk) | (jj > ss), -1.0, 0.0).astype(BF16)
    halves = [slice(e * tk, (e + 1) * tk) for e in range(2)]

    def both_heads(x):
        zero = jnp.zeros_like(x)
        return jnp.concatenate([jnp.where(first, x, zero), jnp.where(first, zero, x)], axis=0)

    def block(k0, carry, diag_off):
        r0 = 0 if diag_off is None else diag_off
        new = []
        for p in pairs:
            acc, logsurv = carry[2 * p][r0:], carry[2 * p + 1][r0:]
            k2 = both_heads(k_ref[0, pl.ds(k0, tk), lanes[p]])
            v2 = both_heads(v_ref[0, pl.ds(k0, tk), lanes[p]])
            z = _mm(q_ref[0, r0:, lanes[p]], k2, _NT)
            neg_abs = lax.bitcast_convert_type(
                lax.bitcast_convert_type(z, jnp.uint32) | jnp.uint32(0x80000000), F32)
            sp = jnp.maximum(z, 0.0) + _log2(1.0 + jnp.exp2(neg_abs))
            if diag_off is not None:
                causal = (col + diag_off < row)[r0:]
                sp = jnp.where(causal, sp, 0.0)
            hi, lo = _split2(sp)
            st = [_mm(jnp.concatenate([hi[:, h], lo[:, h]], axis=1), suffix_total) for h in halves]
            suffix = jnp.concatenate([st[0][:, :tk], st[1][:, :tk]], axis=1)
            total = jnp.concatenate([st[0][:, tk:], st[1][:, tk:]], axis=1)
            w = jnp.exp2((z - sp) + logsurv + suffix)
            if diag_off is not None:
                w = jnp.where(causal, w, 0.0)
            acc = acc + _mm(w.astype(BF16), v2)
            logsurv = logsurv + total
            if r0:
                acc = jnp.concatenate([carry[2 * p][:r0], acc], axis=0)
                logsurv = jnp.concatenate([carry[2 * p + 1][:r0], logsurv], axis=0)
            new += [acc, logsurv]
        return tuple(new)

    carry = (jnp.zeros((tq, LANES), F32), jnp.zeros((tq, 2 * tk), F32)) * n_pairs
    n_diag = tq // tk
    for j in reversed(range(n_diag)):
        carry = block(pl.multiple_of(q0 + j * tk, tk), carry, j * tk)

    def alive(carry):
        worst = carry[1]
        for p in range(1, n_pairs):
            worst = jnp.maximum(worst, carry[2 * p + 1])
        return (jnp.max(worst) > ATTN_DEAD_LOG2).astype(jnp.int32)

    n_off = qi * n_diag

    def cond(state):
        return (state[0] < n_off) & (state[1] > 0)

    def body(state):
        i = state[0]
        carry = block(pl.multiple_of(q0 - (i + 1) * tk, tk), state[2:], None)
        return (i + 1, alive(carry)) + carry

    carry = lax.while_loop(cond, body, (jnp.int32(0), alive(carry)) + carry)[2:]
    for p in pairs:
        o_ref[0, :, lanes[p]] = carry[2 * p].astype(o_ref.dtype)


def _attn_call(q, k, v, *, tq, n_pairs):
    bsz, slen, hd = q.shape
    width = n_pairs * LANES
    qblk = pl.BlockSpec((1, tq, width), lambda b, p, i: (b, i, p))
    kvblk = pl.BlockSpec((1, slen, width), lambda b, p, i: (b, 0, p))
    return pl.pallas_call(
        functools.partial(_attn_body, tq=tq, n_pairs=n_pairs),
        grid=(bsz, hd // width, slen // tq),
        in_specs=[qblk, kvblk, kvblk],
        out_specs=qblk,
        out_shape=jax.ShapeDtypeStruct(q.shape, BF16),
        compiler_params=_params("parallel", "parallel", "arbitrary"),
        name="stick_breaking_attn",
    )(q, k, v)


def _rwkv_chunk_body(r_ref, lw_ref, k_ref, v_ref, a_ref, b_ref, g_ref, tri_ref, gmean_ref,
                     rk_ref, lnw_ref, lnb_ref, o_ref, st_ref, *, n_sub):
    c = pl.program_id(1)

    @pl.when(c == 0)
    def _():
        st_ref[...] = jnp.zeros_like(st_ref)

    r, lw, k, v = r_ref[0].astype(F32), lw_ref[0], k_ref[0].astype(F32), v_ref[0].astype(F32)
    a, b = a_ref[0].astype(F32), b_ref[0].astype(F32)
    ch = RWKV_CHUNK
    n_pairs = r.shape[1] // LANES
    cum = _mm_lhs_exact(tri_ref[...], lw)
    cum_prev = cum - lw
    lasts = [cum[(s + 1) * ch - 1:(s + 1) * ch, :] for s in range(n_sub)]
    last = jnp.concatenate([jnp.broadcast_to(t, (ch, t.shape[1])) for t in lasts], axis=0)
    e_pos = jnp.exp(cum)
    e_neg = jnp.exp(-cum)
    e_end = jnp.exp(last - cum)
    at = a * jnp.exp(cum_prev)
    rt = r * e_pos
    bt = b * e_neg
    kt = k * e_neg
    bh = b * e_end
    kh = k * e_end
    p_end = [jnp.exp(t) for t in lasts]

    lane = lax.broadcasted_iota(jnp.int32, (1, LANES), 1)
    first = lane < HEAD_DIM

    def stacked(x, s, p):
        x = x[s * ch:(s + 1) * ch, p * LANES:(p + 1) * LANES]
        zero = jnp.zeros_like(x)
        return jnp.concatenate([jnp.where(first, x, zero), jnp.where(first, zero, x)], axis=0)

    row = lax.broadcasted_iota(jnp.int32, (2 * ch, 2 * ch), 0)
    col = lax.broadcasted_iota(jnp.int32, (2 * ch, 2 * ch), 1)
    strict = (row & (ch - 1)) > (col & (ch - 1))
    row2 = lax.broadcasted_iota(jnp.int32, (2 * ch, 4 * ch), 0)
    col2 = lax.broadcasted_iota(jnp.int32, (2 * ch, 4 * ch), 1)
    lower2 = (row2 & (ch - 1)) >= (col2 & (ch - 1))
    diag = row == col
    eye = jnp.where(diag, 1.0, 0.0)
    n_levels = int(math.log2(ch))

    def sub_diag(level):
        size, half = 1 << level, 1 << (level - 1)
        inside = (row >> level) == (col >> level)
        return inside & ((row & (size - 1)) >= half) & ((col & (size - 1)) < half)

    units = [(s, p) for s in range(n_sub) for p in range(n_pairs)]
    at_s, rt_s, bt_s, kt_s, bh_s, kh_s, v_s = (
        [stacked(t, s, p) for s, p in units] for t in (at, rt, bt, kt, bh, kh, v))
    idx = range(len(units))
    prod = [_mm(jnp.concatenate([at_s[u], rt_s[u]], axis=0).astype(BF16),
                jnp.concatenate([bt_s[u], kt_s[u]], axis=0).astype(BF16), _NT) for u in idx]
    a_ab = [jnp.where(strict, prod[u][:2 * ch, :2 * ch], 0.0) for u in idx]
    a_ak = [jnp.where(strict, prod[u][:2 * ch, 2 * ch:], 0.0).astype(BF16) for u in idx]
    a_rbk = [jnp.where(lower2, prod[u][2 * ch:, :], 0.0).astype(BF16) for u in idx]
    lvl = sub_diag(1)
    tinv = [eye + jnp.where(lvl, a_ab[u], 0.0) for u in idx]
    for level in range(2, n_levels + 1):
        lvl = sub_diag(level)
        tinv_b = [tinv[u].astype(BF16) for u in idx]
        right = [_mm(jnp.where(lvl, a_ab[u], 0.0).astype(BF16), tinv_b[u]) for u in idx]
        tinv = [tinv[u] + _mm(tinv_b[u], right[u].astype(BF16)) for u in idx]
    v_b = [v_s[u].astype(BF16) for u in idx]
    akv = [_mm(a_ak[u], v_b[u]) for u in idx]
    wx = [_mm(tinv[u].astype(BF16),
              jnp.concatenate([at_s[u], akv[u]], axis=1).astype(BF16)) for u in idx]
    zero_b = jnp.zeros((2 * ch, LANES), BF16)
    wxv = [jnp.concatenate([wx[u].astype(BF16), jnp.concatenate([zero_b, v_b[u]], axis=1)], axis=0)
           for u in idx]
    qy = [_mm(a_rbk[u], wxv[u]) for u in idx]
    mn = [_mm(jnp.concatenate([bh_s[u], kh_s[u]], axis=0).T.astype(BF16), wxv[u]) for u in idx]
    ys = []
    for s in range(n_sub):
        rows = []
        for p in range(n_pairs):
            u = s * n_pairs + p
            st = st_ref[p]
            y_st = _mm((rt_s[u] + qy[u][:, :LANES]).astype(BF16), st.astype(BF16)) + qy[u][:, LANES:]
            rows.append(y_st[:ch] + y_st[ch:])
            m_t = jnp.where(diag, p_end[s][:, p * LANES:(p + 1) * LANES], 0.0) + mn[u][:, :LANES]
            st_ref[p] = _mm3(m_t, st) + mn[u][:, LANES:]
        ys.append(jnp.concatenate(rows, axis=1))
    y = jnp.concatenate(ys, axis=0)

    gmean = gmean_ref[...]
    y_hi, y_lo = _split2(y)
    mean = _groups(y_hi, gmean) + _groups(y_lo, gmean)
    dev = y - mean
    var = _groups((dev * dev).astype(BF16), gmean)
    yn = dev * lax.rsqrt(var + GN_EPS) * lnw_ref[...] + lnb_ref[...]
    bonus = _groups((r * k * rk_ref[...]).astype(BF16), gmean) * float(HEAD_DIM) * v
    o_ref[0] = ((yn + bonus) * g_ref[0].astype(F32)).astype(o_ref.dtype)


def _rwkv_chunk_call(r, lw, k, v, a, b, g, gmean, rk, lnw, lnb, *, n_sub):
    bsz, slen, hd = r.shape
    rows = n_sub * RWKV_CHUNK
    idx = jnp.arange(rows)
    same_chunk = idx[:, None] // RWKV_CHUNK == idx[None, :] // RWKV_CHUNK
    tri = (same_chunk & (idx[:, None] >= idx[None, :])).astype(BF16)
    tok = pl.BlockSpec((1, rows, hd), lambda bi, ci: (bi, ci, 0))
    consts = [tri, gmean, rk, lnw, lnb]
    return pl.pallas_call(
        functools.partial(_rwkv_chunk_body, n_sub=n_sub),
        grid=(bsz, slen // rows),
        in_specs=[tok] * 7 + [_resident(c.shape) for c in consts],
        out_specs=tok,
        out_shape=jax.ShapeDtypeStruct((bsz, slen, hd), BF16),
        scratch_shapes=[pltpu.VMEM((hd // LANES, LANES, LANES), F32)],
        compiler_params=_params("parallel", "arbitrary"),
        name="rwkv_chunk_scan",
    )(r, lw, k, v, a, b, g, *consts)


def _block_diag_const(width, group, value, dtype):
    idx = jnp.arange(width) // group
    return jnp.where(idx[:, None] == idx[None, :], value, 0.0).astype(dtype)


def kernel(x, c, mod_w, mod_b, norm_g, ffn1_wg, ffn1_wu, ffn1_wd, w_in_first, w_in_rest, shift_mu, decay_w0, decay_w2, aaa_a0, aaa_a2, gate_g2, k_k, k_a, r_k, lnx_w, lnx_b, vres_v0, vres_v2, q_norm_g, k_norm_g, w_up_a, w_up_b, w_out, ffn2_wg, ffn2_wu, ffn2_wd):
    bsz, slen, d = x.shape
    n_layers = mod_w.shape[0]
    hd = decay_w0.shape[-1]
    n_heads = hd // HEAD_DIM
    decay_rank, aaa_rank = decay_w2.shape[1], aaa_a2.shape[1]
    gate_rank, vres_rank = gate_g2.shape[1], vres_v2.shape[1]
    assert decay_rank + aaa_rank == LANES and gate_rank == LANES and vres_rank <= LANES
    assert w_up_b.shape[1] == hd and hd % LANES == 0
    rw = 3 * hd + decay_rank + aaa_rank + gate_rank
    cols_first = rw + 3 * hd + 2 * d
    assert w_in_first.shape[1] == cols_first

    tm = min(512, slen)
    tq_attn = min(512, slen)

    gmean = _block_diag_const(MXU_WIDTH, HEAD_DIM, 1.0 / HEAD_DIM, BF16)
    gsum = _block_diag_const(MXU_WIDTH, HEAD_DIM, 1.0, BF16)

    mod_all = _mod_call(c, mod_w, mod_b).reshape(n_layers, bsz, N_MOD, d)

    ffn1 = tuple(t.astype(BF16) for t in (ffn1_wg, ffn1_wu, ffn1_wd))
    ffn2 = tuple(t.astype(BF16) for t in (ffn2_wg, ffn2_wu, ffn2_wd))
    ups = tuple(t.astype(BF16) for t in (w_up_a, w_up_b, w_out))
    w_in = jnp.concatenate([
        jnp.pad(w_in_first, ((0, 0), (0, LANES)))[None],
        jnp.pad(w_in_rest, ((0, 0), (0, 0), (0, LANES - vres_rank)))], axis=0).astype(BF16)

    h = x
    v_first = None
    for l in range(n_layers):
        h = _ffn_call(h, norm_g, mod_all, *ffn1, layer=l, sub=0, tm=tm, tf=256)

        has_vres = l > 0
        gq = jnp.tile(q_norm_g[l] * (LOG2E / math.sqrt(HEAD_DIM)), n_heads)[None, :]
        gk = jnp.tile(k_norm_g[l], n_heads)[None, :]
        lora_w = jnp.zeros((LANES, 2 * hd), F32)
        lora_w = lora_w.at[:decay_rank, :hd].set(decay_w2[l]).at[decay_rank:, hd:].set(aaa_a2[l])
        prep_consts = {
            "mu": shift_mu[l][None, :], "w0": decay_w0[l][None, :], "lora": lora_w,
            "a0": aaa_a0[l][None, :], "g2": gate_g2[l].astype(BF16), "k_k": k_k[l][None, :],
            "k_a": k_a[l][None, :], "gsum": gsum,
        }
        if has_vres:
            prep_consts["v0"] = vres_v0[l - 1][None, :]
            prep_consts["v2"] = jnp.pad(vres_v2[l - 1], ((0, LANES - vres_rank), (0, 0))).astype(BF16)
        q, k, v, gates, r_, lw_, k_, v_, a_, b_, g_ = _inproj_call(
            h, norm_g, mod_all, w_in, gq, gk, gmean, v_first, prep_consts,
            layer=l, rw=rw, hd=hd, decay_rank=decay_rank, tm=tm)
        if not has_vres:
            v_first = v_
        y_a = _rwkv_chunk_call(r_, lw_, k_, v_, a_, b_, g_, gmean, r_k[l].reshape(1, hd),
                               lnx_w[l][None, :], lnx_b[l][None, :], n_sub=RWKV_SUBCHUNKS)

        y_b = _attn_call(q, k, v, tq=tq_attn, n_pairs=hd // LANES)

        h = _ffn_call(h, norm_g, mod_all, *ffn2, layer=l, sub=2, tm=tm, tf=256,
                      branches=(y_a, y_b, gates, *ups))
    return h
```

```python
import functools
import math

import jax
import jax.numpy as jnp
from jax import lax
from jax.experimental import pallas as pl
from jax.experimental.pallas import tpu as pltpu

F32 = jnp.float32
BF16 = jnp.bfloat16

HEAD_DIM = 64
RMS_EPS = 1e-6
GN_EPS = 64e-5
N_MOD = 9
LANES = 128
MXU_WIDTH = 256
VMEM_LIMIT = 56 * 1024 * 1024
RWKV_CHUNK = 64
RWKV_SUBCHUNKS = 4

_NN = (((1,), (0,)), ((), ()))
_NT = (((1,), (1,)), ((), ()))


def _mm(a, b, dims=_NN):
    return lax.dot_general(a, b, dims, preferred_element_type=F32)


def _split2(x):
    hi = x.astype(BF16)
    lo = (x - hi.astype(F32)).astype(BF16)
    return hi, lo


def _split3(x):
    hi = x.astype(BF16)
    r1 = x - hi.astype(F32)
    mid = r1.astype(BF16)
    lo = (r1 - mid.astype(F32)).astype(BF16)
    return hi, mid, lo


def _mm3(a, b, dims=_NN):
    ah, al = _split2(a)
    bh, bl = _split2(b)
    return _mm(ah, bh, dims) + (_mm(ah, bl, dims) + _mm(al, bh, dims))


def _mm_lhs_exact(a_bf16, b):
    hi, mid, lo = _split3(b)
    return _mm(a_bf16, hi) + (_mm(a_bf16, mid) + _mm(a_bf16, lo))


def _groups(x_bf16, g):
    n = g.shape[0]
    slabs = [_mm(x_bf16[:, i * n:(i + 1) * n], g) for i in range(x_bf16.shape[1] // n)]
    return jnp.concatenate(slabs, axis=1)


def _params(*sem):
    return pltpu.CompilerParams(dimension_semantics=sem, vmem_limit_bytes=VMEM_LIMIT)


def _resident(shape):
    zeros = (0,) * len(shape)
    return pl.BlockSpec(shape, lambda *_: zeros, pipeline_mode=pl.Buffered(1))


def _layer(arr, l):
    zeros = (0,) * (arr.ndim - 1)
    return pl.BlockSpec((1,) + arr.shape[1:], lambda *_: (l,) + zeros, pipeline_mode=pl.Buffered(1))


def _norm_mod(x, g, sh, sc):
    ms = jnp.mean(x * x, axis=-1, keepdims=True)
    return (x * lax.rsqrt(ms + RMS_EPS)) * (g * (1.0 + sc)) + sh


LOG2E = 1.4426950408889634
ATTN_DEAD_LOG2 = -160.0


def _log2(x):
    return jnp.log(x) * LOG2E


def _softplus(x):
    return jnp.maximum(x, 0.0) + jnp.log(1.0 + jnp.exp(-jnp.abs(x)))


def _mod_body(c_ref, w_ref, b_ref, o_ref):
    c = c_ref[...]
    ca = (c * jax.nn.sigmoid(c)).astype(BF16)
    o_ref[0] = _mm(ca, w_ref[0].astype(BF16)) + b_ref[0]


def _mod_call(c, mod_w, mod_b):
    n_layers, d, nd = mod_w.shape
    bsz = c.shape[0]
    tn = (N_MOD // 3) * d
    return pl.pallas_call(
        _mod_body,
        grid=(n_layers, nd // tn),
        in_specs=[
            pl.BlockSpec((bsz, d), lambda l, j: (0, 0)),
            pl.BlockSpec((1, d, tn), lambda l, j: (l, 0, j)),
            pl.BlockSpec((1, 1, tn), lambda l, j: (l, 0, j)),
        ],
        out_specs=pl.BlockSpec((1, bsz, tn), lambda l, j: (l, 0, j)),
        out_shape=jax.ShapeDtypeStruct((n_layers, bsz, nd), F32),
        compiler_params=_params("parallel", "parallel"),
        name="adaln_mod",
    )(c, mod_w, mod_b.reshape(n_layers, 1, nd))


def _ffn_body(*refs, sub, tf, merge):
    if merge:
        h_ref, ya_ref, yb_ref, gate_ref, wa_ref, wb_ref, wo_ref = refs[:7]
        refs = refs[7:]
    else:
        h_ref = refs[0]
        refs = refs[1:]
    g_ref, mod_ref, wg_ref, wu_ref, wd_ref, o_ref = refs
    x = h_ref[0]
    mod = mod_ref[0, 0]
    if merge:
        d = x.shape[-1]
        ua = _mm(ya_ref[0], wa_ref[0])
        ub = _mm(yb_ref[0], wb_ref[0])
        gate_a = jax.nn.sigmoid(gate_ref[0, :, 0:d].astype(F32))
        gate_b = jax.nn.sigmoid(gate_ref[0, :, d:2 * d].astype(F32))
        merged = gate_a * ua + gate_b * ub
        x = x + mod[5:6, :] * _mm(merged.astype(BF16), wo_ref[0])
    g = g_ref[0, sub:sub + 1, :]
    sh = mod[3 * sub:3 * sub + 1, :]
    sc = mod[3 * sub + 1:3 * sub + 2, :]
    gt = mod[3 * sub + 2:3 * sub + 3, :]
    n = _norm_mod(x, g, sh, sc).astype(BF16)
    d_ff = wg_ref.shape[2]
    acc = jnp.zeros_like(x)
    for j in range(d_ff // tf):
        gj = _mm(n, wg_ref[0, :, j * tf:(j + 1) * tf])
        uj = _mm(n, wu_ref[0, :, j * tf:(j + 1) * tf])
        a = (gj * jax.nn.sigmoid(gj) * uj).astype(BF16)
        acc = acc + _mm(a, wd_ref[0, j * tf:(j + 1) * tf, :])
    o_ref[0] = x + (0.5 * gt) * acc


def _ffn_call(h, norm_g, mod, wg, wu, wd, *, layer, sub, tm, tf, branches=None):
    bsz, slen, d = h.shape

    def tok(width):
        return pl.BlockSpec((1, tm, width), lambda b, i: (b, i, 0))

    args, specs = [h], [tok(d)]
    if branches is not None:
        ya, yb, gates, wa, wb, wo = branches
        args += [ya, yb, gates, wa, wb, wo]
        specs += [tok(ya.shape[-1]), tok(yb.shape[-1]), tok(2 * d),
                  _layer(wa, layer), _layer(wb, layer), _layer(wo, layer)]
    args += [norm_g, mod, wg, wu, wd]
    specs += [_layer(norm_g, layer), pl.BlockSpec((1, 1, N_MOD, d), lambda b, i: (layer, b, 0, 0)),
              _layer(wg, layer), _layer(wu, layer), _layer(wd, layer)]
    return pl.pallas_call(
        functools.partial(_ffn_body, sub=sub, tf=tf, merge=branches is not None),
        grid=(bsz, slen // tm),
        in_specs=specs,
        out_specs=tok(d),
        out_shape=jax.ShapeDtypeStruct(h.shape, F32),
        compiler_params=_params("parallel", "parallel"),
        name="merge_ffn" if branches is not None else "ffn",
    )(*args)


def _head_rms(t, g, gmean):
    ms = _groups((t * t).astype(BF16), gmean)
    return t * lax.rsqrt(ms + RMS_EPS) * g


def _rwkv_prep(p, prev, tail, v_first, c, *, hd, decay_rank):
    tm = p.shape[0]
    shifted = pltpu.roll(p, 1, axis=0)
    rows = lax.broadcasted_iota(jnp.int32, (tm, 1), 0)
    shifted = jnp.where(rows == 0, prev, shifted)
    p = p + (shifted - p) * c["mu"]
    r = p[:, 0:hd]
    k = p[:, hd:2 * hd]
    v = p[:, 2 * hd:3 * hd]
    wa = p[:, 3 * hd:3 * hd + LANES]
    gl = p[:, 3 * hd + LANES:3 * hd + 2 * LANES]
    lane = lax.broadcasted_iota(jnp.int32, (1, LANES), 1)
    wa = jnp.where(lane < decay_rank, jnp.tanh(wa), wa)
    lora = _mm3(wa, c["lora"])
    w = -_softplus(-(c["w0"] + lora[:, 0:hd])) - 0.5
    log_decay = -jnp.exp(w)
    a = jax.nn.sigmoid(c["a0"] + lora[:, hd:2 * hd])
    g = _mm(jax.nn.sigmoid(gl).astype(BF16), c["g2"])
    if tail is not None:
        mix = jax.nn.sigmoid(c["v0"] + _mm(tail.astype(BF16), c["v2"]))
        v = v + (v_first - v) * mix
    kk = k * c["k_k"]
    sq_hi, sq_lo = _split2(kk * kk)
    ss = _groups(sq_hi, c["gsum"]) + _groups(sq_lo, c["gsum"])
    kk = kk / jnp.maximum(jnp.sqrt(ss), 1e-12)
    return r, log_decay, k * (1.0 + (a - 1.0) * c["k_a"]), v, -kk, kk * a, g


_PREP_CONSTS = ("mu", "w0", "lora", "a0", "g2", "k_k", "k_a", "gsum")
_VRES_CONSTS = ("v0", "v2")


def _inproj_body(*refs, rw, hd, d, has_vres, decay_rank):
    names = ("h", "g", "mod", "w", "gq", "gk", "gmean") + (("vf",) if has_vres else ())
    names += _PREP_CONSTS + (_VRES_CONSTS if has_vres else ())
    ins = dict(zip(names, refs))
    q_ref, k_ref, v_ref, gate_ref = refs[len(names):len(names) + 4]
    rwkv_outs = refs[len(names) + 4:len(names) + 11]
    prev_ref = refs[len(names) + 11]
    i = pl.program_id(1)
    x = ins["h"][0]
    tm = x.shape[0]
    mod = ins["mod"][0, 0]
    n = _norm_mod(x, ins["g"][0, 1:2, :], mod[3:4, :], mod[4:5, :]).astype(BF16)
    w_ref = ins["w"].at[0]
    gmean = ins["gmean"][...]
    p = _mm(n, w_ref[:, 0:rw])
    og = rw + 3 * hd
    tail = _mm(n, w_ref[:, og + 2 * d:og + 2 * d + LANES]) if has_vres else None
    q = _mm(n, w_ref[:, rw:rw + hd])
    q_ref[0] = _head_rms(q, ins["gq"][...], gmean).astype(BF16)
    k = _mm(n, w_ref[:, rw + hd:rw + 2 * hd])
    k_ref[0] = _head_rms(k, ins["gk"][...], gmean).astype(BF16)
    v_ref[0] = _mm(n, w_ref[:, rw + 2 * hd:rw + 3 * hd]).astype(BF16)
    gate_ref[0] = _mm(n, w_ref[:, og:og + 2 * d]).astype(BF16)

    prev = prev_ref[7:8, :]
    prev = jnp.where(i == 0, jnp.zeros_like(prev), prev)
    prev_ref[...] = p[tm - 8:tm, :]
    consts = {name: ins[name][...] for name in _PREP_CONSTS + (_VRES_CONSTS if has_vres else ())}
    v_first = ins["vf"][0].astype(F32) if has_vres else None
    outs = _rwkv_prep(p, prev, tail, v_first, consts, hd=hd, decay_rank=decay_rank)
    for o_ref, val in zip(rwkv_outs, outs):
        o_ref[0] = val.astype(o_ref.dtype)


def _inproj_call(h, norm_g, mod, w, gq, gk, gmean, v_first, prep_consts, *, layer, rw, hd,
                 decay_rank, tm):
    bsz, slen, d = h.shape
    has_vres = v_first is not None

    def tok(width):
        return pl.BlockSpec((1, tm, width), lambda b, i: (b, i, 0))

    args = [h, norm_g, mod, w, gq, gk, gmean]
    specs = [tok(d), _layer(norm_g, layer), pl.BlockSpec((1, 1, N_MOD, d), lambda b, i: (layer, b, 0, 0)),
             _layer(w, layer), _resident(gq.shape), _resident(gk.shape), _resident(gmean.shape)]
    if has_vres:
        args.append(v_first)
        specs.append(tok(hd))
    names = _PREP_CONSTS + (_VRES_CONSTS if has_vres else ())
    args += [prep_consts[name] for name in names]
    specs += [_resident(prep_consts[name].shape) for name in names]
    outs = [(hd, BF16)] * 3 + [(2 * d, BF16)] + [(hd, F32 if j == 1 else BF16) for j in range(7)]
    return pl.pallas_call(
        functools.partial(_inproj_body, rw=rw, hd=hd, d=d, has_vres=has_vres, decay_rank=decay_rank),
        grid=(bsz, slen // tm),
        in_specs=specs,
        out_specs=[tok(wd) for wd, _ in outs],
        out_shape=[jax.ShapeDtypeStruct((bsz, slen, wd), dt) for wd, dt in outs],
        scratch_shapes=[pltpu.VMEM((8, rw), F32)],
        compiler_params=_params("parallel", "arbitrary"),
        name="in_proj",
    )(*args)


def _attn_body(q_ref, k_ref, v_ref, o_ref, *, tq, n_pairs):
    tk = LANES
    qi = pl.program_id(2)
    q0 = qi * tq
    pairs = range(n_pairs)
    lanes = [slice(p * LANES, (p + 1) * LANES) for p in pairs]
    lane = lax.broadcasted_iota(jnp.int32, (1, LANES), 1)
    first = lane < HEAD_DIM
    row = lax.broadcasted_iota(jnp.int32, (tq, 2 * tk), 0)
    col = lax.broadcasted_iota(jnp.int32, (tq, 2 * tk), 1)
    col = jnp.where(col >= tk, col - tk, col)
    jj = lax.broadcasted_iota(jnp.int32, (2 * tk, 2 * tk), 0)
    ss = lax.broadcasted_iota(jnp.int32, (2 * tk, 2 * tk), 1)
    jj = jnp.where(jj >= tk, jj - tk, jj)
    suffix_total = jnp.where((ss >= tk) | (jj > ss), -1.0, 0.0).astype(BF16)
    halves = [slice(e * tk, (e + 1) * tk) for e in range(2)]

    def both_heads(x):
        zero = jnp.zeros_like(x)
        return jnp.concatenate([jnp.where(first, x, zero), jnp.where(first, zero, x)], axis=0)

    def block(k0, carry, diag_off):
        r0 = 0 if diag_off is None else diag_off
        new = []
        for p in pairs:
            acc, logsurv = carry[2 * p][r0:], carry[2 * p + 1][r0:]
            k2 = both_heads(k_ref[0, pl.ds(k0, tk), lanes[p]])
            v2 = both_heads(v_ref[0, pl.ds(k0, tk), lanes[p]])
            z = _mm(q_ref[0, r0:, lanes[p]], k2, _NT)
            neg_abs = lax.bitcast_convert_type(
                lax.bitcast_convert_type(z, jnp.uint32) | jnp.uint32(0x80000000), F32)
            sp = jnp.maximum(z, 0.0) + _log2(1.0 + jnp.exp2(neg_abs))
            if diag_off is not None:
                causal = (col + diag_off < row)[r0:]
                sp = jnp.where(causal, sp, 0.0)
            hi, lo = _split2(sp)
            st = [_mm(jnp.concatenate([hi[:, h], lo[:, h]], axis=1), suffix_total) for h in halves]
            suffix = jnp.concatenate([st[0][:, :tk], st[1][:, :tk]], axis=1)
            total = jnp.concatenate([st[0][:, tk:], st[1][:, tk:]], axis=1)
            w = jnp.exp2((z - sp) + logsurv + suffix)
            if diag_off is not None:
                w = jnp.where(causal, w, 0.0)
            acc = acc + _mm(w.astype(BF16), v2)
            logsurv = logsurv + total
            if r0:
                acc = jnp.concatenate([carry[2 * p][:r0], acc], axis=0)
                logsurv = jnp.concatenate([carry[2 * p + 1][:r0], logsurv], axis=0)
            new += [acc, logsurv]
        return tuple(new)

    carry = (jnp.zeros((tq, LANES), F32), jnp.zeros((tq, 2 * tk), F32)) * n_pairs
    n_diag = tq // tk
    for j in reversed(range(n_diag)):
        carry = block(pl.multiple_of(q0 + j * tk, tk), carry, j * tk)

    def alive(carry):
        worst = carry[1]
        for p in range(1, n_pairs):
            worst = jnp.maximum(worst, carry[2 * p + 1])
        return (jnp.max(worst) > ATTN_DEAD_LOG2).astype(jnp.int32)

    n_off = qi * n_diag

    def cond(state):
        return (state[0] < n_off) & (state[1] > 0)

    def body(state):
        i = state[0]
        carry = block(pl.multiple_of(q0 - (i + 1) * tk, tk), state[2:], None)
        return (i + 1, alive(carry)) + carry

    carry = lax.while_loop(cond, body, (jnp.int32(0), alive(carry)) + carry)[2:]
    for p in pairs:
        o_ref[0, :, lanes[p]] = carry[2 * p].astype(o_ref.dtype)


def _attn_call(q, k, v, *, tq, n_pairs):
    bsz, slen, hd = q.shape
    width = n_pairs * LANES
    qblk = pl.BlockSpec((1, tq, width), lambda b, p, i: (b, i, p))
    kvblk = pl.BlockSpec((1, slen, width), lambda b, p, i: (b, 0, p))
    return pl.pallas_call(
        functools.partial(_attn_body, tq=tq, n_pairs=n_pairs),
        grid=(bsz, hd // width, slen // tq),
        in_specs=[qblk, kvblk, kvblk],
        out_specs=qblk,
        out_shape=jax.ShapeDtypeStruct(q.shape, BF16),
        compiler_params=_params("parallel", "parallel", "arbitrary"),
        name="stick_breaking_attn",
    )(q, k, v)


def _rwkv_chunk_body(r_ref, lw_ref, k_ref, v_ref, a_ref, b_ref, g_ref, tri_ref, gmean_ref,
                     rk_ref, lnw_ref, lnb_ref, o_ref, st_ref, *, n_sub):
    c = pl.program_id(1)

    @pl.when(c == 0)
    def _():
        st_ref[...] = jnp.zeros_like(st_ref)

    r, lw, k, v = r_ref[0].astype(F32), lw_ref[0], k_ref[0].astype(F32), v_ref[0].astype(F32)
    a, b = a_ref[0].astype(F32), b_ref[0].astype(F32)
    ch = RWKV_CHUNK
    n_pairs = r.shape[1] // LANES
    cum = _mm_lhs_exact(tri_ref[...], lw)
    cum_prev = cum - lw
    lasts = [cum[(s + 1) * ch - 1:(s + 1) * ch, :] for s in range(n_sub)]
    last = jnp.concatenate([jnp.broadcast_to(t, (ch, t.shape[1])) for t in lasts], axis=0)
    e_pos = jnp.exp(cum)
    e_neg = jnp.exp(-cum)
    e_end = jnp.exp(last - cum)
    at = a * jnp.exp(cum_prev)
    rt = r * e_pos
    bt = b * e_neg
    kt = k * e_neg
    bh = b * e_end
    kh = k * e_end
    p_end = [jnp.exp(t) for t in lasts]

    lane = lax.broadcasted_iota(jnp.int32, (1, LANES), 1)
    first = lane < HEAD_DIM

    def stacked(x, s, p):
        x = x[s * ch:(s + 1) * ch, p * LANES:(p + 1) * LANES]
        zero = jnp.zeros_like(x)
        return jnp.concatenate([jnp.where(first, x, zero), jnp.where(first, zero, x)], axis=0)

    row = lax.broadcasted_iota(jnp.int32, (2 * ch, 2 * ch), 0)
    col = lax.broadcasted_iota(jnp.int32, (2 * ch, 2 * ch), 1)
    strict = (row & (ch - 1)) > (col & (ch - 1))
    row2 = lax.broadcasted_iota(jnp.int32, (2 * ch, 4 * ch), 0)
    col2 = lax.broadcasted_iota(jnp.int32, (2 * ch, 4 * ch), 1)
    lower2 = (row2 & (ch - 1)) >= (col2 & (ch - 1))
    diag = row == col
    eye = jnp.where(diag, 1.0, 0.0)
    n_levels = int(math.log2(ch))

    def sub_diag(level):
        size, half = 1 << level, 1 << (level - 1)
        inside = (row >> level) == (col >> level)
        return inside & ((row & (size - 1)) >= half) & ((col & (size - 1)) < half)

    units = [(s, p) for s in range(n_sub) for p in range(n_pairs)]
    at_s, rt_s, bt_s, kt_s, bh_s, kh_s, v_s = (
        [stacked(t, s, p) for s, p in units] for t in (at, rt, bt, kt, bh, kh, v))
    idx = range(len(units))
    prod = [_mm(jnp.concatenate([at_s[u], rt_s[u]], axis=0).astype(BF16),
                jnp.concatenate([bt_s[u], kt_s[u]], axis=0).astype(BF16), _NT) for u in idx]
    a_ab = [jnp.where(strict, prod[u][:2 * ch, :2 * ch], 0.0) for u in idx]
    a_ak = [jnp.where(strict, prod[u][:2 * ch, 2 * ch:], 0.0).astype(BF16) for u in idx]
    a_rbk = [jnp.where(lower2, prod[u][2 * ch:, :], 0.0).astype(BF16) for u in idx]
    lvl = sub_diag(1)
    tinv = [eye + jnp.where(lvl, a_ab[u], 0.0) for u in idx]
    for level in range(2, n_levels + 1):
        lvl = sub_diag(level)
        tinv_b = [tinv[u].astype(BF16) for u in idx]
        right = [_mm(jnp.where(lvl, a_ab[u], 0.0).astype(BF16), tinv_b[u]) for u in idx]
        tinv = [tinv[u] + _mm(tinv_b[u], right[u].astype(BF16)) for u in idx]
    v_b = [v_s[u].astype(BF16) for u in idx]
    akv = [_mm(a_ak[u], v_b[u]) for u in idx]
    wx = [_mm(tinv[u].astype(BF16),
              jnp.concatenate([at_s[u], akv[u]], axis=1).astype(BF16)) for u in idx]
    zero_b = jnp.zeros((2 * ch, LANES), BF16)
    wxv = [jnp.concatenate([wx[u].astype(BF16), jnp.concatenate([zero_b, v_b[u]], axis=1)], axis=0)
           for u in idx]
    qy = [_mm(a_rbk[u], wxv[u]) for u in idx]
    mn = [_mm(jnp.concatenate([bh_s[u], kh_s[u]], axis=0).T.astype(BF16), wxv[u]) for u in idx]
    ys = []
    for s in range(n_sub):
        rows = []
        for p in range(n_pairs):
            u = s * n_pairs + p
            st = st_ref[p]
            y_st = _mm((rt_s[u] + qy[u][:, :LANES]).astype(BF16), st.astype(BF16)) + qy[u][:, LANES:]
            rows.append(y_st[:ch] + y_st[ch:])
            m_t = jnp.where(diag, p_end[s][:, p * LANES:(p + 1) * LANES], 0.0) + mn[u][:, :LANES]
            st_ref[p] = _mm3(m_t, st) + mn[u][:, LANES:]
        ys.append(jnp.concatenate(rows, axis=1))
    y = jnp.concatenate(ys, axis=0)

    gmean = gmean_ref[...]
    y_hi, y_lo = _split2(y)
    mean = _groups(y_hi, gmean) + _groups(y_lo, gmean)
    dev = y - mean
    var = _groups((dev * dev).astype(BF16), gmean)
    yn = dev * lax.rsqrt(var + GN_EPS) * lnw_ref[...] + lnb_ref[...]
    bonus = _groups((r * k * rk_ref[...]).astype(BF16), gmean) * float(HEAD_DIM) * v
    o_ref[0] = ((yn + bonus) * g_ref[0].astype(F32)).astype(o_ref.dtype)


def _rwkv_chunk_call(r, lw, k, v, a, b, g, gmean, rk, lnw, lnb, *, n_sub):
    bsz, slen, hd = r.shape
    rows = n_sub * RWKV_CHUNK
    idx = jnp.arange(rows)
    same_chunk = idx[:, None] // RWKV_CHUNK == idx[None, :] // RWKV_CHUNK
    tri = (same_chunk & (idx[:, None] >= idx[None, :])).astype(BF16)
    tok = pl.BlockSpec((1, rows, hd), lambda bi, ci: (bi, ci, 0))
    consts = [tri, gmean, rk, lnw, lnb]
    return pl.pallas_call(
        functools.partial(_rwkv_chunk_body, n_sub=n_sub),
        grid=(bsz, slen // rows),
        in_specs=[tok] * 7 + [_resident(c.shape) for c in consts],
        out_specs=tok,
        out_shape=jax.ShapeDtypeStruct((bsz, slen, hd), BF16),
        scratch_shapes=[pltpu.VMEM((hd // LANES, LANES, LANES), F32)],
        compiler_params=_params("parallel", "arbitrary"),
        name="rwkv_chunk_scan",
    )(r, lw, k, v, a, b, g, *consts)


def _block_diag_const(width, group, value, dtype):
    idx = jnp.arange(width) // group
    return jnp.where(idx[:, None] == idx[None, :], value, 0.0).astype(dtype)


def kernel(x, c, mod_w, mod_b, norm_g, ffn1_wg, ffn1_wu, ffn1_wd, w_in_first, w_in_rest, shift_mu, decay_w0, decay_w2, aaa_a0, aaa_a2, gate_g2, k_k, k_a, r_k, lnx_w, lnx_b, vres_v0, vres_v2, q_norm_g, k_norm_g, w_up_a, w_up_b, w_out, ffn2_wg, ffn2_wu, ffn2_wd):
    bsz, slen, d = x.shape
    n_layers = mod_w.shape[0]
    hd = decay_w0.shape[-1]
    n_heads = hd // HEAD_DIM
    decay_rank, aaa_rank = decay_w2.shape[1], aaa_a2.shape[1]
    gate_rank, vres_rank = gate_g2.shape[1], vres_v2.shape[1]
    assert decay_rank + aaa_rank == LANES and gate_rank == LANES and vres_rank <= LANES
    assert w_up_b.shape[1] == hd and hd % LANES == 0
    rw = 3 * hd + decay_rank + aaa_rank + gate_rank
    cols_first = rw + 3 * hd + 2 * d
    assert w_in_first.shape[1] == cols_first

    tm = min(512, slen)
    tq_attn = min(512, slen)

    gmean = _block_diag_const(MXU_WIDTH, HEAD_DIM, 1.0 / HEAD_DIM, BF16)
    gsum = _block_diag_const(MXU_WIDTH, HEAD_DIM, 1.0, BF16)

    mod_all = _mod_call(c, mod_w, mod_b).reshape(n_layers, bsz, N_MOD, d)

    ffn1 = tuple(t.astype(BF16) for t in (ffn1_wg, ffn1_wu, ffn1_wd))
    ffn2 = tuple(t.astype(BF16) for t in (ffn2_wg, ffn2_wu, ffn2_wd))
    ups = tuple(t.astype(BF16) for t in (w_up_a, w_up_b, w_out))
    w_in = jnp.concatenate([
        jnp.pad(w_in_first, ((0, 0), (0, LANES)))[None],
        jnp.pad(w_in_rest, ((0, 0), (0, 0), (0, LANES - vres_rank)))], axis=0).astype(BF16)

    h = x
    v_first = None
    for l in range(n_layers):
        h = _ffn_call(h, norm_g, mod_all, *ffn1, layer=l, sub=0, tm=tm, tf=256)

        has_vres = l > 0
        gq = jnp.tile(q_norm_g[l] * (LOG2E / math.sqrt(HEAD_DIM)), n_heads)[None, :]
        gk = jnp.tile(k_norm_g[l], n_heads)[None, :]
        lora_w = jnp.zeros((LANES, 2 * hd), F32)
        lora_w = lora_w.at[:decay_rank, :hd].set(decay_w2[l]).at[decay_rank:, hd:].set(aaa_a2[l])
        prep_consts = {
            "mu": shift_mu[l][None, :], "w0": decay_w0[l][None, :], "lora": lora_w,
            "a0": aaa_a0[l][None, :], "g2": gate_g2[l].astype(BF16), "k_k": k_k[l][None, :],
            "k_a": k_a[l][None, :], "gsum": gsum,
        }
        if has_vres:
            prep_consts["v0"] = vres_v0[l - 1][None, :]
            prep_consts["v2"] = jnp.pad(vres_v2[l - 1], ((0, LANES - vres_rank), (0, 0))).astype(BF16)
        q, k, v, gates, r_, lw_, k_, v_, a_, b_, g_ = _inproj_call(
            h, norm_g, mod_all, w_in, gq, gk, gmean, v_first, prep_consts,
            layer=l, rw=rw, hd=hd, decay_rank=decay_rank, tm=tm)
        if not has_vres:
            v_first = v_
        y_a = _rwkv_chunk_call(r_, lw_, k_, v_, a_, b_, g_, gmean, r_k[l].reshape(1, hd),
                               lnx_w[l][None, :], lnx_b[l][None, :], n_sub=RWKV_SUBCHUNKS)

        y_b = _attn_call(q, k, v, tq=tq_attn, n_pairs=hd // LANES)

        h = _ffn_call(h, norm_g, mod_all, *ffn2, layer=l, sub=2, tm=tm, tf=256,
                      branches=(y_a, y_b, gates, *ups))
    return h
```

```python
import functools
import math

import jax
import jax.numpy as jnp
from jax import lax
from jax.experimental import pallas as pl
from jax.experimental.pallas import tpu as pltpu

F32 = jnp.float32
BF16 = jnp.bfloat16

HEAD_DIM = 64
RMS_EPS = 1e-6
GN_EPS = 64e-5
N_MOD = 9
LANES = 128
MXU_WIDTH = 256
VMEM_LIMIT = 56 * 1024 * 1024
RWKV_CHUNK = 64
RWKV_SUBCHUNKS = 4

_NN = (((1,), (0,)), ((), ()))
_NT = (((1,), (1,)), ((), ()))


def _mm(a, b, dims=_NN):
    return lax.dot_general(a, b, dims, preferred_element_type=F32)


def _split2(x):
    hi = x.astype(BF16)
    lo = (x - hi.astype(F32)).astype(BF16)
    return hi, lo


def _split3(x):
    hi = x.astype(BF16)
    r1 = x - hi.astype(F32)
    mid = r1.astype(BF16)
    lo = (r1 - mid.astype(F32)).astype(BF16)
    return hi, mid, lo


def _mm3(a, b, dims=_NN):
    ah, al = _split2(a)
    bh, bl = _split2(b)
    return _mm(ah, bh, dims) + (_mm(ah, bl, dims) + _mm(al, bh, dims))


def _mm_lhs_exact(a_bf16, b):
    hi, mid, lo = _split3(b)
    return _mm(a_bf16, hi) + (_mm(a_bf16, mid) + _mm(a_bf16, lo))


def _groups(x_bf16, g):
    n = g.shape[0]
    slabs = [_mm(x_bf16[:, i * n:(i + 1) * n], g) for i in range(x_bf16.shape[1] // n)]
    return jnp.concatenate(slabs, axis=1)


def _params(*sem):
    return pltpu.CompilerParams(dimension_semantics=sem, vmem_limit_bytes=VMEM_LIMIT)


def _resident(shape):
    zeros = (0,) * len(shape)
    return pl.BlockSpec(shape, lambda *_: zeros, pipeline_mode=pl.Buffered(1))


def _layer(arr, l):
    zeros = (0,) * (arr.ndim - 1)
    return pl.BlockSpec((1,) + arr.shape[1:], lambda *_: (l,) + zeros, pipeline_mode=pl.Buffered(1))


def _norm_mod(x, g, sh, sc):
    ms = jnp.mean(x * x, axis=-1, keepdims=True)
    return (x * lax.rsqrt(ms + RMS_EPS)) * (g * (1.0 + sc)) + sh


LOG2E = 1.4426950408889634
ATTN_DEAD_LOG2 = -160.0


def _log2(x):
    return jnp.log(x) * LOG2E


def _softplus(x):
    return jnp.maximum(x, 0.0) + jnp.log(1.0 + jnp.exp(-jnp.abs(x)))


def _mod_body(c_ref, w_ref, b_ref, o_ref):
    c = c_ref[...]
    ca = (c * jax.nn.sigmoid(c)).astype(BF16)
    o_ref[0] = _mm(ca, w_ref[0].astype(BF16)) + b_ref[0]


def _mod_call(c, mod_w, mod_b):
    n_layers, d, nd = mod_w.shape
    bsz = c.shape[0]
    tn = (N_MOD // 3) * d
    return pl.pallas_call(
        _mod_body,
        grid=(n_layers, nd // tn),
        in_specs=[
            pl.BlockSpec((bsz, d), lambda l, j: (0, 0)),
            pl.BlockSpec((1, d, tn), lambda l, j: (l, 0, j)),
            pl.BlockSpec((1, 1, tn), lambda l, j: (l, 0, j)),
        ],
        out_specs=pl.BlockSpec((1, bsz, tn), lambda l, j: (l, 0, j)),
        out_shape=jax.ShapeDtypeStruct((n_layers, bsz, nd), F32),
        compiler_params=_params("parallel", "parallel"),
        name="adaln_mod",
    )(c, mod_w, mod_b.reshape(n_layers, 1, nd))


def _ffn_body(*refs, sub, tf, merge):
    if merge:
        h_ref, ya_ref, yb_ref, gate_ref, wa_ref, wb_ref, wo_ref = refs[:7]
        refs = refs[7:]
    else:
        h_ref = refs[0]
        refs = refs[1:]
    g_ref, mod_ref, wg_ref, wu_ref, wd_ref, o_ref = refs
    x = h_ref[0]
    mod = mod_ref[0, 0]
    if merge:
        d = x.shape[-1]
        ua = _mm(ya_ref[0], wa_ref[0])
        ub = _mm(yb_ref[0], wb_ref[0])
        gate_a = jax.nn.sigmoid(gate_ref[0, :, 0:d].astype(F32))
        gate_b = jax.nn.sigmoid(gate_ref[0, :, d:2 * d].astype(F32))
        merged = gate_a * ua + gate_b * ub
        x = x + mod[5:6, :] * _mm(merged.astype(BF16), wo_ref[0])
    g = g_ref[0, sub:sub + 1, :]
    sh = mod[3 * sub:3 * sub + 1, :]
    sc = mod[3 * sub + 1:3 * sub + 2, :]
    gt = mod[3 * sub + 2:3 * sub + 3, :]
    n = _norm_mod(x, g, sh, sc).astype(BF16)
    d_ff = wg_ref.shape[2]
    acc = jnp.zeros_like(x)
    for j in range(d_ff // tf):
        gj = _mm(n, wg_ref[0, :, j * tf:(j + 1) * tf])
        uj = _mm(n, wu_ref[0, :, j * tf:(j + 1) * tf])
        a = (gj * jax.nn.sigmoid(gj) * uj).astype(BF16)
        acc = acc + _mm(a, wd_ref[0, j * tf:(j + 1) * tf, :])
    o_ref[0] = x + (0.5 * gt) * acc


def _ffn_call(h, norm_g, mod, wg, wu, wd, *, layer, sub, tm, tf, branches=None):
    bsz, slen, d = h.shape

    def tok(width):
        return pl.BlockSpec((1, tm, width), lambda b, i: (b, i, 0))

    args, specs = [h], [tok(d)]
    if branches is not None:
        ya, yb, gates, wa, wb, wo = branches
        args += [ya, yb, gates, wa, wb, wo]
        specs += [tok(ya.shape[-1]), tok(yb.shape[-1]), tok(2 * d),
                  _layer(wa, layer), _layer(wb, layer), _layer(wo, layer)]
    args += [norm_g, mod, wg, wu, wd]
    specs += [_layer(norm_g, layer), pl.BlockSpec((1, 1, N_MOD, d), lambda b, i: (layer, b, 0, 0)),
              _layer(wg, layer), _layer(wu, layer), _layer(wd, layer)]
    return pl.pallas_call(
        functools.partial(_ffn_body, sub=sub, tf=tf, merge=branches is not None),
        grid=(bsz, slen // tm),
        in_specs=specs,
        out_specs=tok(d),
        out_shape=jax.ShapeDtypeStruct(h.shape, F32),
        compiler_params=_params("parallel", "parallel"),
        name="merge_ffn" if branches is not None else "ffn",
    )(*args)


def _head_rms(t, g, gmean):
    ms = _groups((t * t).astype(BF16), gmean)
    return t * lax.rsqrt(ms + RMS_EPS) * g


def _rwkv_prep(p, prev, tail, v_first, c, *, hd, decay_rank):
    tm = p.shape[0]
    shifted = pltpu.roll(p, 1, axis=0)
    rows = lax.broadcasted_iota(jnp.int32, (tm, 1), 0)
    shifted = jnp.where(rows == 0, prev, shifted)
    p = p + (shifted - p) * c["mu"]
    r = p[:, 0:hd]
    k = p[:, hd:2 * hd]
    v = p[:, 2 * hd:3 * hd]
    wa = p[:, 3 * hd:3 * hd + LANES]
    gl = p[:, 3 * hd + LANES:3 * hd + 2 * LANES]
    lane = lax.broadcasted_iota(jnp.int32, (1, LANES), 1)
    wa = jnp.where(lane < decay_rank, jnp.tanh(wa), wa)
    lora = _mm3(wa, c["lora"])
    w = -_softplus(-(c["w0"] + lora[:, 0:hd])) - 0.5
    log_decay = -jnp.exp(w)
    a = jax.nn.sigmoid(c["a0"] + lora[:, hd:2 * hd])
    g = _mm(jax.nn.sigmoid(gl).astype(BF16), c["g2"])
    if tail is not None:
        mix = jax.nn.sigmoid(c["v0"] + _mm(tail.astype(BF16), c["v2"]))
        v = v + (v_first - v) * mix
    kk = k * c["k_k"]
    sq_hi, sq_lo = _split2(kk * kk)
    ss = _groups(sq_hi, c["gsum"]) + _groups(sq_lo, c["gsum"])
    kk = kk * lax.rsqrt(jnp.maximum(ss, 1e-24))
    return r, log_decay, k * (1.0 + (a - 1.0) * c["k_a"]), v, -kk, kk * a, g


_PREP_CONSTS = ("mu", "w0", "lora", "a0", "g2", "k_k", "k_a", "gsum")
_VRES_CONSTS = ("v0", "v2")


def _inproj_body(*refs, rw, hd, d, has_vres, decay_rank):
    names = ("h", "g", "mod", "w", "gq", "gk", "gmean") + (("vf",) if has_vres else ())
    names += _PREP_CONSTS + (_VRES_CONSTS if has_vres else ())
    ins = dict(zip(names, refs))
    q_ref, k_ref, v_ref, gate_ref = refs[len(names):len(names) + 4]
    rwkv_outs = refs[len(names) + 4:len(names) + 11]
    prev_ref = refs[len(names) + 11]
    i = pl.program_id(1)
    x = ins["h"][0]
    tm = x.shape[0]
    mod = ins["mod"][0, 0]
    n = _norm_mod(x, ins["g"][0, 1:2, :], mod[3:4, :], mod[4:5, :]).astype(BF16)
    w_ref = ins["w"].at[0]
    gmean = ins["gmean"][...]
    p = _mm(n, w_ref[:, 0:rw])
    og = rw + 3 * hd
    tail = _mm(n, w_ref[:, og + 2 * d:og + 2 * d + LANES]) if has_vres else None
    q = _mm(n, w_ref[:, rw:rw + hd])
    q_ref[0] = _head_rms(q, ins["gq"][...], gmean).astype(BF16)
    k = _mm(n, w_ref[:, rw + hd:rw + 2 * hd])
    k_ref[0] = _head_rms(k, ins["gk"][...], gmean).astype(BF16)
    v_ref[0] = _mm(n, w_ref[:, rw + 2 * hd:rw + 3 * hd]).astype(BF16)
    gate_ref[0] = _mm(n, w_ref[:, og:og + 2 * d]).astype(BF16)

    prev = prev_ref[7:8, :]
    prev = jnp.where(i == 0, jnp.zeros_like(prev), prev)
    prev_ref[...] = p[tm - 8:tm, :]
    consts = {name: ins[name][...] for name in _PREP_CONSTS + (_VRES_CONSTS if has_vres else ())}
    v_first = ins["vf"][0].astype(F32) if has_vres else None
    outs = _rwkv_prep(p, prev, tail, v_first, consts, hd=hd, decay_rank=decay_rank)
    for o_ref, val in zip(rwkv_outs, outs):
        o_ref[0] = val.astype(o_ref.dtype)


def _inproj_call(h, norm_g, mod, w, gq, gk, gmean, v_first, prep_consts, *, layer, rw, hd,
                 decay_rank, tm):
    bsz, slen, d = h.shape
    has_vres = v_first is not None

    def tok(width):
        return pl.BlockSpec((1, tm, width), lambda b, i: (b, i, 0))

    args = [h, norm_g, mod, w, gq, gk, gmean]
    specs = [tok(d), _layer(norm_g, layer), pl.BlockSpec((1, 1, N_MOD, d), lambda b, i: (layer, b, 0, 0)),
             _layer(w, layer), _resident(gq.shape), _resident(gk.shape), _resident(gmean.shape)]
    if has_vres:
        args.append(v_first)
        specs.append(tok(hd))
    names = _PREP_CONSTS + (_VRES_CONSTS if has_vres else ())
    args += [prep_consts[name] for name in names]
    specs += [_resident(prep_consts[name].shape) for name in names]
    outs = [(hd, BF16)] * 3 + [(2 * d, BF16)] + [(hd, F32 if j == 1 else BF16) for j in range(7)]
    return pl.pallas_call(
        functools.partial(_inproj_body, rw=rw, hd=hd, d=d, has_vres=has_vres, decay_rank=decay_rank),
        grid=(bsz, slen // tm),
        in_specs=specs,
        out_specs=[tok(wd) for wd, _ in outs],
        out_shape=[jax.ShapeDtypeStruct((bsz, slen, wd), dt) for wd, dt in outs],
        scratch_shapes=[pltpu.VMEM((8, rw), F32)],
        compiler_params=_params("parallel", "arbitrary"),
        name="in_proj",
    )(*args)


def _attn_body(q_ref, k_ref, v_ref, o_ref, *, tq, n_pairs):
    tk = LANES
    qi = pl.program_id(2)
    q0 = qi * tq
    pairs = range(n_pairs)
    lanes = [slice(p * LANES, (p + 1) * LANES) for p in pairs]
    lane = lax.broadcasted_iota(jnp.int32, (1, LANES), 1)
    first = lane < HEAD_DIM
    row = lax.broadcasted_iota(jnp.int32, (tq, 2 * tk), 0)
    col = lax.broadcasted_iota(jnp.int32, (tq, 2 * tk), 1)
    col = jnp.where(col >= tk, col - tk, col)
    jj = lax.broadcasted_iota(jnp.int32, (2 * tk, 2 * tk), 0)
    ss = lax.broadcasted_iota(jnp.int32, (2 * tk, 2 * tk), 1)
    jj = jnp.where(jj >= tk, jj - tk, jj)
    suffix_total = jnp.where((ss >= tk) | (jj > ss), -1.0, 0.0).astype(BF16)
    halves = [slice(e * tk, (e + 1) * tk) for e in range(2)]

    def both_heads(x):
        zero = jnp.zeros_like(x)
        return jnp.concatenate([jnp.where(first, x, zero), jnp.where(first, zero, x)], axis=0)

    def block(k0, carry, diag_off):
        r0 = 0 if diag_off is None else diag_off
        new = []
        for p in pairs:
            acc, logsurv = carry[2 * p][r0:], carry[2 * p + 1][r0:]
            k2 = both_heads(k_ref[0, pl.ds(k0, tk), lanes[p]])
            v2 = both_heads(v_ref[0, pl.ds(k0, tk), lanes[p]])
            z = _mm(q_ref[0, r0:, lanes[p]], k2, _NT)
            neg_abs = lax.bitcast_convert_type(
                lax.bitcast_convert_type(z, jnp.uint32) | jnp.uint32(0x80000000), F32)
            sp = jnp.maximum(z, 0.0) + _log2(1.0 + jnp.exp2(neg_abs))
            if diag_off is not None:
                causal = (col + diag_off < row)[r0:]
                sp = jnp.where(causal, sp, 0.0)
            hi, lo = _split2(sp)
            st = [_mm(jnp.concatenate([hi[:, h], lo[:, h]], axis=1), suffix_total) for h in halves]
            suffix = jnp.concatenate([st[0][:, :tk], st[1][:, :tk]], axis=1)
            total = jnp.concatenate([st[0][:, tk:], st[1][:, tk:]], axis=1)
            w = jnp.exp2((z - sp) + logsurv + suffix)
            if diag_off is not None:
                w = jnp.where(causal, w, 0.0)
            acc = acc + _mm(w.astype(BF16), v2)
            logsurv = logsurv + total
            if r0:
                acc = jnp.concatenate([carry[2 * p][:r0], acc], axis=0)
                logsurv = jnp.concatenate([carry[2 * p + 1][:r0], logsurv], axis=0)
            new += [acc, logsurv]
        return tuple(new)

    carry = (jnp.zeros((tq, LANES), F32), jnp.zeros((tq, 2 * tk), F32)) * n_pairs
    n_diag = tq // tk
    for j in reversed(range(n_diag)):
        carry = block(pl.multiple_of(q0 + j * tk, tk), carry, j * tk)

    def alive(carry):
        worst = carry[1]
        for p in range(1, n_pairs):
            worst = jnp.maximum(worst, carry[2 * p + 1])
        return (jnp.max(worst) > ATTN_DEAD_LOG2).astype(jnp.int32)

    n_off = qi * n_diag

    def cond(state):
        return (state[0] < n_off) & (state[1] > 0)

    def body(state):
        i = state[0]
        carry = block(pl.multiple_of(q0 - (i + 1) * tk, tk), state[2:], None)
        return (i + 1, alive(carry)) + carry

    carry = lax.while_loop(cond, body, (jnp.int32(0), alive(carry)) + carry)[2:]
    for p in pairs:
        o_ref[0, :, lanes[p]] = carry[2 * p].astype(o_ref.dtype)


def _attn_call(q, k, v, *, tq, n_pairs):
    bsz, slen, hd = q.shape
    width = n_pairs * LANES
    qblk = pl.BlockSpec((1, tq, width), lambda b, p, i: (b, i, p))
    kvblk = pl.BlockSpec((1, slen, width), lambda b, p, i: (b, 0, p))
    return pl.pallas_call(
        functools.partial(_attn_body, tq=tq, n_pairs=n_pairs),
        grid=(bsz, hd // width, slen // tq),
        in_specs=[qblk, kvblk, kvblk],
        out_specs=qblk,
        out_shape=jax.ShapeDtypeStruct(q.shape, BF16),
        compiler_params=_params("parallel", "parallel", "arbitrary"),
        name="stick_breaking_attn",
    )(q, k, v)


def _rwkv_chunk_body(r_ref, lw_ref, k_ref, v_ref, a_ref, b_ref, g_ref, tri_ref, gmean_ref,
                     rk_ref, lnw_ref, lnb_ref, o_ref, st_ref, *, n_sub):
    c = pl.program_id(1)

    @pl.when(c == 0)
    def _():
        st_ref[...] = jnp.zeros_like(st_ref)

    r, lw, k, v = r_ref[0].astype(F32), lw_ref[0], k_ref[0].astype(F32), v_ref[0].astype(F32)
    a, b = a_ref[0].astype(F32), b_ref[0].astype(F32)
    ch = RWKV_CHUNK
    n_pairs = r.shape[1] // LANES
    cum = _mm_lhs_exact(tri_ref[...], lw)
    cum_prev = cum - lw
    lasts = [cum[(s + 1) * ch - 1:(s + 1) * ch, :] for s in range(n_sub)]
    last = jnp.concatenate([jnp.broadcast_to(t, (ch, t.shape[1])) for t in lasts], axis=0)
    e_pos = jnp.exp(cum)
    e_neg = jnp.exp(-cum)
    e_end = jnp.exp(last - cum)
    at = a * jnp.exp(cum_prev)
    rt = r * e_pos
    bt = b * e_neg
    kt = k * e_neg
    bh = b * e_end
    kh = k * e_end
    p_end = [jnp.exp(t) for t in lasts]

    lane = lax.broadcasted_iota(jnp.int32, (1, LANES), 1)
    first = lane < HEAD_DIM

    def stacked(x, s, p):
        x = x[s * ch:(s + 1) * ch, p * LANES:(p + 1) * LANES]
        zero = jnp.zeros_like(x)
        return jnp.concatenate([jnp.where(first, x, zero), jnp.where(first, zero, x)], axis=0)

    row = lax.broadcasted_iota(jnp.int32, (2 * ch, 2 * ch), 0)
    col = lax.broadcasted_iota(jnp.int32, (2 * ch, 2 * ch), 1)
    strict = (row & (ch - 1)) > (col & (ch - 1))
    row2 = lax.broadcasted_iota(jnp.int32, (2 * ch, 4 * ch), 0)
    col2 = lax.broadcasted_iota(jnp.int32, (2 * ch, 4 * ch), 1)
    lower2 = (row2 & (ch - 1)) >= (col2 & (ch - 1))
    diag = row == col
    eye = jnp.where(diag, 1.0, 0.0)
    n_levels = int(math.log2(ch))

    def sub_diag(level):
        size, half = 1 << level, 1 << (level - 1)
        inside = (row >> level) == (col >> level)
        return inside & ((row & (size - 1)) >= half) & ((col & (size - 1)) < half)

    units = [(s, p) for s in range(n_sub) for p in range(n_pairs)]
    at_s, rt_s, bt_s, kt_s, bh_s, kh_s, v_s = (
        [stacked(t, s, p) for s, p in units] for t in (at, rt, bt, kt, bh, kh, v))
    idx = range(len(units))
    prod = [_mm(jnp.concatenate([at_s[u], rt_s[u]], axis=0).astype(BF16),
                jnp.concatenate([bt_s[u], kt_s[u]], axis=0).astype(BF16), _NT) for u in idx]
    a_ab = [jnp.where(strict, prod[u][:2 * ch, :2 * ch], 0.0) for u in idx]
    a_ak = [jnp.where(strict, prod[u][:2 * ch, 2 * ch:], 0.0).astype(BF16) for u in idx]
    a_rbk = [jnp.where(lower2, prod[u][2 * ch:, :], 0.0).astype(BF16) for u in idx]
    lvl = sub_diag(1)
    tinv = [eye + jnp.where(lvl, a_ab[u], 0.0) for u in idx]
    for level in range(2, n_levels + 1):
        lvl = sub_diag(level)
        tinv_b = [tinv[u].astype(BF16) for u in idx]
        right = [_mm(jnp.where(lvl, a_ab[u], 0.0).astype(BF16), tinv_b[u]) for u in idx]
        tinv = [tinv[u] + _mm(tinv_b[u], right[u].astype(BF16)) for u in idx]
    v_b = [v_s[u].astype(BF16) for u in idx]
    akv = [_mm(a_ak[u], v_b[u]) for u in idx]
    wx = [_mm(tinv[u].astype(BF16),
              jnp.concatenate([at_s[u], akv[u]], axis=1).astype(BF16)) for u in idx]
    zero_b = jnp.zeros((2 * ch, LANES), BF16)
    wxv = [jnp.concatenate([wx[u].astype(BF16), jnp.concatenate([zero_b, v_b[u]], axis=1)], axis=0)
           for u in idx]
    qy = [_mm(a_rbk[u], wxv[u]) for u in idx]
    mn = [_mm(jnp.concatenate([bh_s[u], kh_s[u]], axis=0).T.astype(BF16), wxv[u]) for u in idx]
    ys = []
    for s in range(n_sub):
        rows = []
        for p in range(n_pairs):
            u = s * n_pairs + p
            st = st_ref[p]
            y_st = _mm((rt_s[u] + qy[u][:, :LANES]).astype(BF16), st.astype(BF16)) + qy[u][:, LANES:]
            rows.append(y_st[:ch] + y_st[ch:])
            m_t = jnp.where(diag, p_end[s][:, p * LANES:(p + 1) * LANES], 0.0) + mn[u][:, :LANES]
            st_ref[p] = _mm3(m_t, st) + mn[u][:, LANES:]
        ys.append(jnp.concatenate(rows, axis=1))
    y = jnp.concatenate(ys, axis=0)

    gmean = gmean_ref[...]
    y_hi, y_lo = _split2(y)
    mean = _groups(y_hi, gmean) + _groups(y_lo, gmean)
    dev = y - mean
    var = _groups((dev * dev).astype(BF16), gmean)
    yn = dev * lax.rsqrt(var + GN_EPS) * lnw_ref[...] + lnb_ref[...]
    bonus = _groups((r * k * rk_ref[...]).astype(BF16), gmean) * float(HEAD_DIM) * v
    o_ref[0] = ((yn + bonus) * g_ref[0].astype(F32)).astype(o_ref.dtype)


def _rwkv_chunk_call(r, lw, k, v, a, b, g, gmean, rk, lnw, lnb, *, n_sub):
    bsz, slen, hd = r.shape
    rows = n_sub * RWKV_CHUNK
    idx = jnp.arange(rows)
    same_chunk = idx[:, None] // RWKV_CHUNK == idx[None, :] // RWKV_CHUNK
    tri = (same_chunk & (idx[:, None] >= idx[None, :])).astype(BF16)
    tok = pl.BlockSpec((1, rows, hd), lambda bi, ci: (bi, ci, 0))
    consts = [tri, gmean, rk, lnw, lnb]
    return pl.pallas_call(
        functools.partial(_rwkv_chunk_body, n_sub=n_sub),
        grid=(bsz, slen // rows),
        in_specs=[tok] * 7 + [_resident(c.shape) for c in consts],
        out_specs=tok,
        out_shape=jax.ShapeDtypeStruct((bsz, slen, hd), BF16),
        scratch_shapes=[pltpu.VMEM((hd // LANES, LANES, LANES), F32)],
        compiler_params=_params("parallel", "arbitrary"),
        name="rwkv_chunk_scan",
    )(r, lw, k, v, a, b, g, *consts)


def _block_diag_const(width, group, value, dtype):
    idx = jnp.arange(width) // group
    return jnp.where(idx[:, None] == idx[None, :], value, 0.0).astype(dtype)


def kernel(x, c, mod_w, mod_b, norm_g, ffn1_wg, ffn1_wu, ffn1_wd, w_in_first, w_in_rest, shift_mu, decay_w0, decay_w2, aaa_a0, aaa_a2, gate_g2, k_k, k_a, r_k, lnx_w, lnx_b, vres_v0, vres_v2, q_norm_g, k_norm_g, w_up_a, w_up_b, w_out, ffn2_wg, ffn2_wu, ffn2_wd):
    bsz, slen, d = x.shape
    n_layers = mod_w.shape[0]
    hd = decay_w0.shape[-1]
    n_heads = hd // HEAD_DIM
    decay_rank, aaa_rank = decay_w2.shape[1], aaa_a2.shape[1]
    gate_rank, vres_rank = gate_g2.shape[1], vres_v2.shape[1]
    assert decay_rank + aaa_rank == LANES and gate_rank == LANES and vres_rank <= LANES
    assert w_up_b.shape[1] == hd and hd % LANES == 0
    rw = 3 * hd + decay_rank + aaa_rank + gate_rank
    cols_first = rw + 3 * hd + 2 * d
    assert w_in_first.shape[1] == cols_first

    tm = min(512, slen)
    tq_attn = min(512, slen)

    gmean = _block_diag_const(MXU_WIDTH, HEAD_DIM, 1.0 / HEAD_DIM, BF16)
    gsum = _block_diag_const(MXU_WIDTH, HEAD_DIM, 1.0, BF16)

    mod_all = _mod_call(c, mod_w, mod_b).reshape(n_layers, bsz, N_MOD, d)

    ffn1 = tuple(t.astype(BF16) for t in (ffn1_wg, ffn1_wu, ffn1_wd))
    ffn2 = tuple(t.astype(BF16) for t in (ffn2_wg, ffn2_wu, ffn2_wd))
    ups = tuple(t.astype(BF16) for t in (w_up_a, w_up_b, w_out))
    w_in = jnp.concatenate([
        jnp.pad(w_in_first, ((0, 0), (0, LANES)))[None],
        jnp.pad(w_in_rest, ((0, 0), (0, 0), (0, LANES - vres_rank)))], axis=0).astype(BF16)

    h = x
    v_first = None
    for l in range(n_layers):
        h = _ffn_call(h, norm_g, mod_all, *ffn1, layer=l, sub=0, tm=tm, tf=256)

        has_vres = l > 0
        gq = jnp.tile(q_norm_g[l] * (LOG2E / math.sqrt(HEAD_DIM)), n_heads)[None, :]
        gk = jnp.tile(k_norm_g[l], n_heads)[None, :]
        lora_w = jnp.zeros((LANES, 2 * hd), F32)
        lora_w = lora_w.at[:decay_rank, :hd].set(decay_w2[l]).at[decay_rank:, hd:].set(aaa_a2[l])
        prep_consts = {
            "mu": shift_mu[l][None, :], "w0": decay_w0[l][None, :], "lora": lora_w,
            "a0": aaa_a0[l][None, :], "g2": gate_g2[l].astype(BF16), "k_k": k_k[l][None, :],
            "k_a": k_a[l][None, :], "gsum": gsum,
        }
        if has_vres:
            prep_consts["v0"] = vres_v0[l - 1][None, :]
            prep_consts["v2"] = jnp.pad(vres_v2[l - 1], ((0, LANES - vres_rank), (0, 0))).astype(BF16)
        q, k, v, gates, r_, lw_, k_, v_, a_, b_, g_ = _inproj_call(
            h, norm_g, mod_all, w_in, gq, gk, gmean, v_first, prep_consts,
            layer=l, rw=rw, hd=hd, decay_rank=decay_rank, tm=tm)
        if not has_vres:
            v_first = v_
        y_a = _rwkv_chunk_call(r_, lw_, k_, v_, a_, b_, g_, gmean, r_k[l].reshape(1, hd),
                               lnx_w[l][None, :], lnx_b[l][None, :], n_sub=RWKV_SUBCHUNKS)

        y_b = _attn_call(q, k, v, tq=tq_attn, n_pairs=hd // LANES)

        h = _ffn_call(h, norm_g, mod_all, *ffn2, layer=l, sub=2, tm=tm, tf=256,
                      branches=(y_a, y_b, gates, *ups))
    return h
```
